```python
import math
import jax
import jax.numpy as jnp
from jax import lax
import numpy as np

D_MODEL = 2048
BATCH = 16
SEQ = 2048
DEPTH = 1
DEC_BATCH = 8
DEC_SEQ = 64
PAST_LEN = 4096

CHUNK = 64
Q_BLOCK = 128
EPS = 1e-6
NEG_INF = -1e30
D_FF = 5632
DA_HEADS = 8
DA_HALF = 64
DA_HEAD_DIM = 2 * DA_HALF
DA_WIDTH = DA_HEADS * DA_HEAD_DIM
MLA_HEADS = 8
Q_LORA = 512
KV_LORA = 256
QK_NOPE = 128
QK_ROPE = 64
QK_HEAD = QK_NOPE + QK_ROPE
V_HEAD = 128
MLA_WIDTH = MLA_HEADS * V_HEAD
MIX_WIDTH = DA_WIDTH + MLA_WIDTH
IN_COLS = 3 * DA_WIDTH + Q_LORA + KV_LORA + QK_ROPE
ROPE_THETA = 10000.0

kernel_name = "hymba_diffattn_mla_macaron_stream"


def _rmsnorm(x, g):
    xf = x.astype(jnp.float32)
    y = xf * lax.rsqrt(jnp.mean(xf * xf, axis=-1, keepdims=True) + EPS)
    return (y * g.astype(jnp.float32)).astype(x.dtype)


def _swiglu(x, w_gate, w_up, w_down):
    return (jax.nn.silu(x @ w_gate) * (x @ w_up)) @ w_down


def _alibi_slopes(n):
    return jnp.asarray(np.array([2.0 ** (-8.0 * (i + 1) / n) for i in range(n)], np.float32))


def _rope(x, pos):
    half = x.shape[-1] // 2
    inv = ROPE_THETA ** (-jnp.arange(half, dtype=jnp.float32) / half)
    ang = pos.astype(jnp.float32)[:, None] * inv[None, :]
    cos = jnp.cos(ang)[:, None, :]
    sin = jnp.sin(ang)[:, None, :]
    xf = x.astype(jnp.float32)
    x1, x2 = xf[..., :half], xf[..., half:]
    return jnp.concatenate([x1 * cos - x2 * sin, x1 * sin + x2 * cos], axis=-1).astype(x.dtype)


def _chunk_mask(qpos, kpos):
    return (kpos[None, :] // CHUNK) <= (qpos[:, None] // CHUNK)


def _diff_attn_core(q, qpos, k, v, kpos, lam, slopes):
    mask = _chunk_mask(qpos, kpos)
    dist = jnp.abs(qpos[:, None] - kpos[None, :]).astype(jnp.float32)
    bias = -slopes[:, None, None] * dist[None]
    scale = DA_HALF ** -0.5

    def probs(qa, ka):
        s = jnp.einsum("bqhd,bkhd->bhqk", qa, ka, preferred_element_type=jnp.float32) * scale + bias
        return jax.nn.softmax(jnp.where(mask, s, NEG_INF), axis=-1)

    a = probs(q[..., :DA_HALF], k[..., :DA_HALF]) - lam * probs(q[..., DA_HALF:], k[..., DA_HALF:])
    return jnp.einsum("bhqk,bkhd->bqhd", a.astype(v.dtype), v)


def _mla_core(q, qpos, k, v, kpos):
    mask = _chunk_mask(qpos, kpos)
    s = jnp.einsum("bqhd,bkhd->bhqk", q, k, preferred_element_type=jnp.float32) * (QK_HEAD ** -0.5)
    p = jax.nn.softmax(jnp.where(mask, s, NEG_INF), axis=-1)
    return jnp.einsum("bhqk,bkhd->bqhd", p.astype(v.dtype), v)


def _attend(fn, q, qpos, blocked):
    if not blocked:
        return fn(q, qpos)
    B, T = q.shape[0], q.shape[1]
    nb = T // Q_BLOCK
    qb = jnp.moveaxis(q.reshape((B, nb, Q_BLOCK) + q.shape[2:]), 1, 0)
    pb = qpos.reshape(nb, Q_BLOCK)
    out = lax.map(lambda a: fn(a[0], a[1]), (qb, pb))
    return jnp.moveaxis(out, 0, 1).reshape((B, T) + out.shape[3:])


def _token_mix(h, qpos, past, w_in, lq1, lk1, lq2, lk2, da_subln, q_norm, w_uq,
               kv_norm, w_ukv, w_out, layer_idx, blocked):
    B, T, _ = h.shape
    z = h @ w_in
    o1, o2, o3 = DA_WIDTH, 2 * DA_WIDTH, 3 * DA_WIDTH
    o4, o5 = o3 + Q_LORA, o3 + Q_LORA + KV_LORA
    da_q = z[..., :o1].reshape(B, T, DA_HEADS, DA_HEAD_DIM)
    da_k = z[..., o1:o2].reshape(B, T, DA_HEADS, DA_HEAD_DIM)
    da_v = z[..., o2:o3].reshape(B, T, DA_HEADS, DA_HEAD_DIM)
    c_q = z[..., o3:o4]
    c_kv = _rmsnorm(z[..., o4:o5], kv_norm)
    k_r = _rope(z[..., o5:][:, :, None, :], qpos)[:, :, 0, :]

    q_m = (_rmsnorm(c_q, q_norm) @ w_uq).reshape(B, T, MLA_HEADS, QK_HEAD)
    q_m = jnp.concatenate([q_m[..., :QK_NOPE], _rope(q_m[..., QK_NOPE:], qpos)], axis=-1)

    if past is None:
        k_all, v_all, ckv_all, kr_all = da_k, da_v, c_kv, k_r
        kpos = qpos
    else:
        ck, cv, cckv, ckr = past
        k_all = jnp.concatenate([ck, da_k], axis=1)
        v_all = jnp.concatenate([cv, da_v], axis=1)
        ckv_all = jnp.concatenate([cckv, c_kv], axis=1)
        kr_all = jnp.concatenate([ckr, k_r], axis=1)
        kpos = jnp.arange(ck.shape[1] + T, dtype=jnp.int32)
    Tk = k_all.shape[1]

    lam_init = 0.8 - 0.6 * math.exp(-0.3 * layer_idx)
    lam = (jnp.exp(jnp.sum(lq1.astype(jnp.float32) * lk1.astype(jnp.float32)))
           - jnp.exp(jnp.sum(lq2.astype(jnp.float32) * lk2.astype(jnp.float32))) + lam_init)
    slopes = _alibi_slopes(DA_HEADS)
    out_da = _attend(lambda qb, pb: _diff_attn_core(qb, pb, k_all, v_all, kpos, lam, slopes),
                     da_q, qpos, blocked)
    out_da = _rmsnorm(out_da, da_subln) * (1.0 - lam_init)

    kv = (ckv_all @ w_ukv).reshape(B, Tk, MLA_HEADS, QK_NOPE + V_HEAD)
    k_m = jnp.concatenate(
        [kv[..., :QK_NOPE], jnp.broadcast_to(kr_all[:, :, None, :], (B, Tk, MLA_HEADS, QK_ROPE))], axis=-1)
    v_m = kv[..., QK_NOPE:]
    out_mla = _attend(lambda qb, pb: _mla_core(qb, pb, k_m, v_m, kpos), q_m, qpos, blocked)

    merged = jnp.concatenate([out_da.reshape(B, T, DA_WIDTH), out_mla.reshape(B, T, MLA_WIDTH)], axis=-1)
    return merged @ w_out, (da_k, da_v, c_kv, k_r)


def _trunk(x, qpos, caches, w, blocked):
    (ffn1_norm, ffn1_w_gate, ffn1_w_up, ffn1_w_down, mix_norm, w_in,
     lq1, lk1, lq2, lk2, da_subln, q_norm, w_uq, kv_norm, w_ukv, w_out,
     ffn2_norm, ffn2_w_gate, ffn2_w_up, ffn2_w_down, final_norm) = w
    states = []
    for l in range(DEPTH):
        x = x + 0.5 * _swiglu(_rmsnorm(x, ffn1_norm[l]), ffn1_w_gate[l], ffn1_w_up[l], ffn1_w_down[l])
        past = None if caches is None else (caches[0][l], caches[1][l], caches[2][l], caches[3][l])
        y, st = _token_mix(_rmsnorm(x, mix_norm[l]), qpos, past, w_in[l], lq1[l], lk1[l], lq2[l], lk2[l],
                           da_subln[l], q_norm[l], w_uq[l], kv_norm[l], w_ukv[l], w_out[l], l, blocked)
        x = x + y
        x = x + 0.5 * _swiglu(_rmsnorm(x, ffn2_norm[l]), ffn2_w_gate[l], ffn2_w_up[l], ffn2_w_down[l])
        states.append(st)
    y = _rmsnorm(x, final_norm)
    new_state = tuple(jnp.stack([s[i] for s in states], axis=0) for i in range(4))
    return y, new_state


def setup_inputs(seed: int = 0) -> dict:
    key = jax.random.key(seed)
    ks = jax.random.split(key, 32)
    f32 = jnp.float32
    L = DEPTH

    def nrm(k, shape, scale=1.0):
        return jax.random.normal(k, shape, f32) * scale

    def gain(k, n):
        return 1.0 + 0.05 * nrm(k, (L, n))

    return {
        "x_prompt": nrm(ks[0], (BATCH, SEQ, D_MODEL)),
        "x_sample": nrm(ks[1], (DEC_BATCH, DEC_SEQ, D_MODEL)),
        "cache_da_k": nrm(ks[2], (L, DEC_BATCH, PAST_LEN, DA_HEADS, DA_HEAD_DIM)),
        "cache_da_v": nrm(ks[3], (L, DEC_BATCH, PAST_LEN, DA_HEADS, DA_HEAD_DIM)),
        "cache_mla_ckv": nrm(ks[4], (L, DEC_BATCH, PAST_LEN, KV_LORA)),
        "cache_mla_krope": nrm(ks[5], (L, DEC_BATCH, PAST_LEN, QK_ROPE)),
        "ffn1_norm": gain(ks[6], D_MODEL),
        "ffn1_w_gate": nrm(ks[7], (L, D_MODEL, D_FF), D_MODEL ** -0.5),
        "ffn1_w_up": nrm(ks[8], (L, D_MODEL, D_FF), D_MODEL ** -0.5),
        "ffn1_w_down": nrm(ks[9], (L, D_FF, D_MODEL), D_FF ** -0.5),
        "mix_norm": gain(ks[10], D_MODEL),
        "w_in": nrm(ks[11], (L, D_MODEL, IN_COLS), D_MODEL ** -0.5),
        "da_lambda_q1": nrm(ks[12], (L, DA_HALF), 0.1),
        "da_lambda_k1": nrm(ks[13], (L, DA_HALF), 0.1),
        "da_lambda_q2": nrm(ks[14], (L, DA_HALF), 0.1),
        "da_lambda_k2": nrm(ks[15], (L, DA_HALF), 0.1),
        "da_subln": gain(ks[16], DA_HEAD_DIM),
        "mla_q_norm": gain(ks[17], Q_LORA),
        "mla_w_uq": nrm(ks[18], (L, Q_LORA, MLA_HEADS * QK_HEAD), Q_LORA ** -0.5),
        "mla_kv_norm": gain(ks[19], KV_LORA),
        "mla_w_ukv": nrm(ks[20], (L, KV_LORA, MLA_HEADS * (QK_NOPE + V_HEAD)), KV_LORA ** -0.5),
        "w_out": nrm(ks[21], (L, MIX_WIDTH, D_MODEL), MIX_WIDTH ** -0.5),
        "ffn2_norm": gain(ks[22], D_MODEL),
        "ffn2_w_gate": nrm(ks[23], (L, D_MODEL, D_FF), D_MODEL ** -0.5),
        "ffn2_w_up": nrm(ks[24], (L, D_MODEL, D_FF), D_MODEL ** -0.5),
        "ffn2_w_down": nrm(ks[25], (L, D_FF, D_MODEL), D_FF ** -0.5),
        "final_norm": 1.0 + 0.05 * nrm(ks[26], (D_MODEL,)),
    }


def reference(x_prompt, x_sample, cache_da_k, cache_da_v, cache_mla_ckv, cache_mla_krope,
              ffn1_norm, ffn1_w_gate, ffn1_w_up, ffn1_w_down, mix_norm, w_in,
              da_lambda_q1, da_lambda_k1, da_lambda_q2, da_lambda_k2, da_subln,
              mla_q_norm, mla_w_uq, mla_kv_norm, mla_w_ukv, w_out,
              ffn2_norm, ffn2_w_gate, ffn2_w_up, ffn2_w_down, final_norm):
    w = (ffn1_norm, ffn1_w_gate, ffn1_w_up, ffn1_w_down, mix_norm, w_in,
         da_lambda_q1, da_lambda_k1, da_lambda_q2, da_lambda_k2, da_subln,
         mla_q_norm, mla_w_uq, mla_kv_norm, mla_w_ukv, w_out,
         ffn2_norm, ffn2_w_gate, ffn2_w_up, ffn2_w_down, final_norm)

    pos_p = jnp.arange(x_prompt.shape[1], dtype=jnp.int32)
    y_prompt, st_p = _trunk(x_prompt, pos_p, None, w, True)

    past_len = cache_da_k.shape[2]
    pos_s = past_len + jnp.arange(x_sample.shape[1], dtype=jnp.int32)
    y_sample, st_s = _trunk(x_sample, pos_s,
                            (cache_da_k, cache_da_v, cache_mla_ckv, cache_mla_krope), w, False)

    new_da_k_p, new_da_v_p, new_ckv_p, new_kr_p = st_p
    new_da_k_s, new_da_v_s, new_ckv_s, new_kr_s = st_s
    return (y_prompt, y_sample, new_da_k_p, new_da_v_p, new_ckv_p, new_kr_p,
            new_da_k_s, new_da_v_s, new_ckv_s, new_kr_s)
```

```python
import functools
import math

import jax
import jax.numpy as jnp
import numpy as np
from jax import lax
from jax.experimental import pallas as pl
from jax.experimental.pallas import tpu as pltpu

F32 = jnp.float32
BF16 = jnp.bfloat16

CHUNK = 64
CHUNK_SHIFT = 6
assert 1 << CHUNK_SHIFT == CHUNK
EPS = 1e-6
NEG_INF = -1e30
DA_HEADS = 8
DA_HALF = 64
DA_HEAD_DIM = 2 * DA_HALF
DA_WIDTH = DA_HEADS * DA_HEAD_DIM
MLA_HEADS = 8
Q_LORA = 512
KV_LORA = 256
QK_NOPE = 128
QK_ROPE = 64
QK_HEAD = QK_NOPE + QK_ROPE
V_HEAD = 128
MLA_WIDTH = MLA_HEADS * V_HEAD
ROPE_THETA = 10000.0
LANES = 128
assert DA_HEAD_DIM == LANES and QK_NOPE == LANES and V_HEAD == LANES and 2 * QK_ROPE == LANES

VMEM_LIMIT_BYTES = 56 * 1024 * 1024


def _params(*semantics):
    return pltpu.CompilerParams(dimension_semantics=semantics, vmem_limit_bytes=VMEM_LIMIT_BYTES)


def _rms(x, g):
    return x * lax.rsqrt(jnp.mean(x * x, axis=-1, keepdims=True) + EPS) * g


def _dot(a, b):
    return jnp.dot(a, b, preferred_element_type=F32)


def _dot_nt(a, b):
    return lax.dot_general(a, b, (((1,), (1,)), ((), ())), preferred_element_type=F32)


def _resident(shape):
    return pl.BlockSpec(shape, lambda *_: (0,) * len(shape), pipeline_mode=pl.Buffered(1))


def _ffn_kernel(x_ref, g_ref, wg_ref, wu_ref, wd_ref, fn_ref, o_ref, xn_ref, *, final_norm):
    j = pl.program_id(1)

    @pl.when(j == 0)
    def _():
        xn_ref[...] = _rms(x_ref[...], g_ref[...]).astype(BF16)
        o_ref[...] = jnp.zeros_like(o_ref)

    xn = xn_ref[...]
    gate = _dot(xn, wg_ref[...])
    up = _dot(xn, wu_ref[...])
    h = (gate * jax.nn.sigmoid(gate) * up).astype(BF16)
    o_ref[...] += _dot(h, wd_ref[...])

    @pl.when(j == pl.num_programs(1) - 1)
    def _():
        y = x_ref[...] + 0.5 * o_ref[...]
        if final_norm:
            y = _rms(y, fn_ref[...])
        o_ref[...] = y


def _ffn(x, norm, wg, wu, wd, fnorm, *, final_norm, tm, tf):
    n, d = x.shape
    f = wg.shape[1]
    tm = min(tm, n)
    tf = min(tf, f)
    assert n % tm == 0 and f % tf == 0
    return pl.pallas_call(
        functools.partial(_ffn_kernel, final_norm=final_norm),
        grid=(n // tm, f // tf),
        in_specs=[
            pl.BlockSpec((tm, d), lambda i, j: (i, 0)),
            pl.BlockSpec((1, d), lambda i, j: (0, 0)),
            pl.BlockSpec((d, tf), lambda i, j: (0, j)),
            pl.BlockSpec((d, tf), lambda i, j: (0, j)),
            pl.BlockSpec((tf, d), lambda i, j: (j, 0)),
            pl.BlockSpec((1, d), lambda i, j: (0, 0)),
        ],
        out_specs=pl.BlockSpec((tm, d), lambda i, j: (i, 0)),
        out_shape=jax.ShapeDtypeStruct((n, d), F32),
        scratch_shapes=[pltpu.VMEM((tm, d), BF16)],
        compiler_params=_params("parallel", "arbitrary"),
        name="ffn",
    )(x, norm, wg, wu, wd, fnorm)


def _proj_in_kernel(x_ref, g_ref, wq_ref, wk_ref, wv_ref, wcq_ref, wckv_ref, wkr_ref, wkrs_ref,
                    qn_ref, kvn_ref, wuqn_ref, wuqr_ref, wuqrs_ref, wukv_ref,
                    c2_ref, s2_ref, c8_ref, s8_ref,
                    q_ref, kf_ref, kb_ref, vf_ref, vb_ref, ckv_ref, kr_ref, krd_ref,
                    qnope_ref, qrope_ref, kv_ref):
    h = _rms(x_ref[...], g_ref[...]).astype(BF16)
    q_ref[...] = (_dot(h, wq_ref[...]) * (DA_HALF ** -0.5)).astype(BF16)
    k = _dot(h, wk_ref[...])
    kf_ref[...] = k
    kb_ref[...] = k.astype(BF16)
    v = _dot(h, wv_ref[...])
    vf_ref[...] = v
    vb_ref[...] = v.astype(BF16)
    cq = _rms(_dot(h, wcq_ref[...]), qn_ref[...]).astype(BF16)
    ckv = _rms(_dot(h, wckv_ref[...]), kvn_ref[...])
    ckv_ref[...] = ckv
    krd = _dot(h, wkr_ref[...]) * c2_ref[...] + _dot(h, wkrs_ref[...]) * s2_ref[...]
    kr_ref[...] = krd[:, :QK_ROPE]
    krd_ref[...] = krd.astype(BF16)
    mla_scale = QK_HEAD ** -0.5
    qnope_ref[...] = (_dot(cq, wuqn_ref[...]) * mla_scale).astype(BF16)
    qr = _dot(cq, wuqr_ref[...]) * c8_ref[...] + _dot(cq, wuqrs_ref[...]) * s8_ref[...]
    qrope_ref[...] = (qr * mla_scale).astype(BF16)
    kv_ref[...] = _dot(ckv.astype(BF16), wukv_ref[...]).astype(BF16)


def _proj_in(x, w, tabs, *, tm):
    n, d = x.shape
    tm = min(tm, n)
    assert n % tm == 0
    t_rows = tabs["c2"].shape[0]
    assert t_rows % tm == 0 or tm % t_rows == 0
    if tm > t_rows:
        tm = t_rows
    nt = t_rows // tm

    def row(i):
        return (i, 0)

    def trow(i):
        return (i % nt, 0)

    weights = [w["w_q"], w["w_k"], w["w_v"], w["w_cq"], w["w_ckv"], w["w_kr"], w["w_krs"],
               w["q_norm"], w["kv_norm"], w["w_uqn"], w["w_uqr"], w["w_uqrs"], w["w_ukv"]]
    tables = [tabs["c2"], tabs["s2"], tabs["c8"], tabs["s8"]]
    outs = [
        ((n, DA_WIDTH), BF16),
        ((n, DA_WIDTH), F32),
        ((n, DA_WIDTH), BF16),
        ((n, DA_WIDTH), F32),
        ((n, DA_WIDTH), BF16),
        ((n, KV_LORA), F32),
        ((n, QK_ROPE), F32),
        ((n, LANES), BF16),
        ((n, MLA_HEADS * QK_NOPE), BF16),
        ((n, MLA_HEADS * QK_ROPE), BF16),
        ((n, MLA_HEADS * (QK_NOPE + V_HEAD)), BF16),
    ]
    return pl.pallas_call(
        _proj_in_kernel,
        grid=(n // tm,),
        in_specs=([pl.BlockSpec((tm, d), row), _resident((1, d))]
                  + [_resident(a.shape) for a in weights]
                  + [pl.BlockSpec((tm, a.shape[1]), trow) for a in tables]),
        out_specs=[pl.BlockSpec((tm, s[1]), row) for s, _ in outs],
        out_shape=[jax.ShapeDtypeStruct(s, dt) for s, dt in outs],
        compiler_params=_params("parallel"),
        name="proj_in",
    )(x, w["mix_norm"], *weights, *tables)


def _lambda(lq1_ref, lk1_ref, lq2_ref, lk2_ref, lam_init):
    a = jnp.sum(lq1_ref[...] * lk1_ref[...], axis=-1, keepdims=True)
    b = jnp.sum(lq2_ref[...] * lk2_ref[...], axis=-1, keepdims=True)
    return jnp.exp(a) - jnp.exp(b) + lam_init


def _pos_bias(qpos, kpos, slope):
    visible = lax.shift_right_arithmetic(kpos, CHUNK_SHIFT) <= lax.shift_right_arithmetic(qpos, CHUNK_SHIFT)
    if slope is None:
        return jnp.where(visible, 0.0, NEG_INF)
    dist = jnp.abs(qpos - kpos).astype(F32)
    return jnp.where(visible, -slope * dist, NEG_INF)


def _strip(tq, t, slope):
    qpos = lax.broadcasted_iota(jnp.int32, (tq, t), 0)
    kpos = lax.broadcasted_iota(jnp.int32, (tq, t), 1) - (t - tq)
    return _pos_bias(qpos, kpos, slope)


def _softmax_parts(s):
    m = jnp.max(s, axis=-1, keepdims=True)
    p = jnp.exp(s - m)
    return p, jnp.sum(p, axis=-1, keepdims=True)


def _split_halves(q):
    lane = lax.broadcasted_iota(jnp.int32, q.shape, 1)
    zero = jnp.zeros_like(q)
    return jnp.where(lane < DA_HALF, q, zero), jnp.where(lane >= DA_HALF, q, zero)


def _da_finish(o, subln, lam_init):
    return (_rms(o, subln) * (1.0 - lam_init)).astype(BF16)


def _da_prompt_kernel(q_ref, k_ref, v_ref, slope_ref, lq1_ref, lk1_ref, lq2_ref, lk2_ref, subln_ref,
                      o_ref, strip_ref, *, tq, lam_init):
    t = q_ref.shape[0]
    lam = _lambda(lq1_ref, lk1_ref, lq2_ref, lk2_ref, lam_init)
    strip_ref[...] = _strip(tq, t, slope_ref[0][:, :1])
    for qi in range(t // tq):
        ext = (qi + 1) * tq
        q_lo, q_hi = _split_halves(q_ref[qi * tq:ext, :])
        k = k_ref[0:ext, :]
        bias = strip_ref[:, t - ext:t]
        p1, l1 = _softmax_parts(_dot_nt(q_lo, k) + bias)
        p2, l2 = _softmax_parts(_dot_nt(q_hi, k) + bias)
        a = p1 * (1.0 / l1) - p2 * (lam / l2)
        o = _dot(a.astype(BF16), v_ref[0:ext, :])
        o_ref[qi * tq:ext, :] = _da_finish(o, subln_ref[...], lam_init)


def _da_prompt(q, k, v, slopes, lams, subln, *, t, tq, lam_init):
    n = q.shape[0]
    tq = min(tq, t)
    blk = pl.BlockSpec((t, LANES), lambda b, h: (b, h))
    vec = lambda a: pl.BlockSpec(a.shape, lambda b, h: (0, 0))
    return pl.pallas_call(
        functools.partial(_da_prompt_kernel, tq=tq, lam_init=lam_init),
        grid=(n // t, DA_HEADS),
        in_specs=[blk, blk, blk, pl.BlockSpec((1, 1, LANES), lambda b, h: (h, 0, 0))]
                 + [vec(a) for a in lams] + [vec(subln)],
        out_specs=blk,
        out_shape=jax.ShapeDtypeStruct((n, DA_WIDTH), BF16),
        scratch_shapes=[pltpu.VMEM((tq, t), F32)],
        compiler_params=_params("parallel", "parallel"),
        name="da_prompt",
    )(q, k, v, slopes, *lams, subln)


def _select_rope(qr, h):
    lane = lax.broadcasted_iota(jnp.int32, qr.shape, 1)
    mine = lax.shift_right_logical(lane, CHUNK_SHIFT) == (h % 2)
    return jnp.where(mine, qr, jnp.zeros_like(qr))


def _mla_prompt_kernel(qn_ref, qr_ref, kn_ref, v_ref, krd_ref, o_ref, strip_ref, kfull_ref, *, tq):
    t = qn_ref.shape[0]
    h = pl.program_id(1)
    strip_ref[...] = _strip(tq, t, None)
    kfull_ref[:, :LANES] = kn_ref[...]
    kfull_ref[:, LANES:] = krd_ref[...]
    for qi in range(t // tq):
        ext = (qi + 1) * tq
        rows = slice(qi * tq, ext)
        q = jnp.concatenate([qn_ref[rows, :], _select_rope(qr_ref[rows, :], h)], axis=1)
        s = _dot_nt(q, kfull_ref[0:ext, :]) + strip_ref[:, t - ext:t]
        p, l = _softmax_parts(s)
        o = _dot((p * (1.0 / l)).astype(BF16), v_ref[0:ext, :])
        o_ref[rows, :] = o.astype(BF16)


def _mla_prompt(qn, qr, kv, krd, *, t, tq):
    n = qn.shape[0]
    tq = min(tq, t)
    return pl.pallas_call(
        functools.partial(_mla_prompt_kernel, tq=tq),
        grid=(n // t, MLA_HEADS),
        in_specs=[
            pl.BlockSpec((t, LANES), lambda b, h: (b, h)),
            pl.BlockSpec((t, LANES), lambda b, h: (b, h // 2)),
            pl.BlockSpec((t, LANES), lambda b, h: (b, 2 * h)),
            pl.BlockSpec((t, LANES), lambda b, h: (b, 2 * h + 1)),
            pl.BlockSpec((t, LANES), lambda b, h: (b, 0)),
        ],
        out_specs=pl.BlockSpec((t, LANES), lambda b, h: (b, h)),
        out_shape=jax.ShapeDtypeStruct((n, MLA_WIDTH), BF16),
        scratch_shapes=[pltpu.VMEM((tq, t), F32), pltpu.VMEM((t, 2 * LANES), BF16)],
        compiler_params=_params("parallel", "parallel"),
        name="mla_prompt",
    )(qn, qr, kv, kv, krd)


def _positions(tq, tk, q0, k0):
    qpos = lax.broadcasted_iota(jnp.int32, (tq, tk), 0) + q0
    kpos = lax.broadcasted_iota(jnp.int32, (tq, tk), 1) + k0
    return qpos, kpos


def _da_sample_kernel(q_ref, ck_ref, cv_ref, nk_ref, nv_ref, slope_ref, lq1_ref, lk1_ref, lq2_ref, lk2_ref,
                      subln_ref, o_ref, *, lam_init):
    tq = q_ref.shape[0]
    past = ck_ref.shape[0]
    lam = _lambda(lq1_ref, lk1_ref, lq2_ref, lk2_ref, lam_init)
    slope = slope_ref[0][:, :1]
    bias_c = _pos_bias(*_positions(tq, past, past, 0), slope)
    bias_n = _pos_bias(*_positions(tq, tq, past, past), slope)
    ck = ck_ref[...].astype(BF16)
    nk = nk_ref[...]

    def parts(qh):
        sc = _dot_nt(qh, ck) + bias_c
        sn = _dot_nt(qh, nk) + bias_n
        m = jnp.maximum(jnp.max(sc, axis=-1, keepdims=True), jnp.max(sn, axis=-1, keepdims=True))
        pc = jnp.exp(sc - m)
        pn = jnp.exp(sn - m)
        l = jnp.sum(pc, axis=-1, keepdims=True) + jnp.sum(pn, axis=-1, keepdims=True)
        return pc, pn, l

    q_lo, q_hi = _split_halves(q_ref[...])
    pc1, pn1, l1 = parts(q_lo)
    pc2, pn2, l2 = parts(q_hi)
    w1 = 1.0 / l1
    w2 = lam / l2
    ac = (pc1 * w1 - pc2 * w2).astype(BF16)
    an = (pn1 * w1 - pn2 * w2).astype(BF16)
    o = _dot(ac, cv_ref[...].astype(BF16)) + _dot(an, nv_ref[...])
    o_ref[...] = _da_finish(o, subln_ref[...], lam_init)


def _da_sample(q, cache_k, cache_v, nk, nv, slopes, lams, subln, *, lam_init):
    bd, past, _ = cache_k.shape
    n = q.shape[0]
    tq = n // bd
    new = pl.BlockSpec((tq, LANES), lambda b, h: (b, h))
    cache = pl.BlockSpec((None, past, LANES), lambda b, h: (b, 0, h))
    vec = lambda a: pl.BlockSpec(a.shape, lambda b, h: (0, 0))
    return pl.pallas_call(
        functools.partial(_da_sample_kernel, lam_init=lam_init),
        grid=(bd, DA_HEADS),
        in_specs=[new, cache, cache, new, new, pl.BlockSpec((1, 1, LANES), lambda b, h: (h, 0, 0))]
                 + [vec(a) for a in lams] + [vec(subln)],
        out_specs=new,
        out_shape=jax.ShapeDtypeStruct((n, DA_WIDTH), BF16),
        compiler_params=_params("parallel", "parallel"),
        name="da_sample",
    )(q, cache_k, cache_v, nk, nv, slopes, *lams, subln)


def _mla_sample_kernel(qn_ref, qr_ref, ckn_ref, cv_ref, ckrd_ref, nkn_ref, nv_ref, nkrd_ref, o_ref):
    tq = qn_ref.shape[0]
    past = ckn_ref.shape[0]
    h = pl.program_id(1)
    q = jnp.concatenate([qn_ref[...], _select_rope(qr_ref[...], h)], axis=1)
    kc = jnp.concatenate([ckn_ref[...], ckrd_ref[...]], axis=1)
    kn = jnp.concatenate([nkn_ref[...], nkrd_ref[...]], axis=1)
    sc = _dot_nt(q, kc) + _pos_bias(*_positions(tq, past, past, 0), None)
    sn = _dot_nt(q, kn) + _pos_bias(*_positions(tq, tq, past, past), None)
    m = jnp.maximum(jnp.max(sc, axis=-1, keepdims=True), jnp.max(sn, axis=-1, keepdims=True))
    pc = jnp.exp(sc - m)
    pn = jnp.exp(sn - m)
    w = 1.0 / (jnp.sum(pc, axis=-1, keepdims=True) + jnp.sum(pn, axis=-1, keepdims=True))
    o = _dot((pc * w).astype(BF16), cv_ref[...]) + _dot((pn * w).astype(BF16), nv_ref[...])
    o_ref[...] = o.astype(BF16)


def _mla_sample(qn, qr, kv_cache, krd_cache, kv_new, krd_new, *, bd):
    n = qn.shape[0]
    tq = n // bd
    past = kv_cache.shape[0] // bd
    return pl.pallas_call(
        _mla_sample_kernel,
        grid=(bd, MLA_HEADS),
        in_specs=[
            pl.BlockSpec((tq, LANES), lambda b, h: (b, h)),
            pl.BlockSpec((tq, LANES), lambda b, h: (b, h // 2)),
            pl.BlockSpec((past, LANES), lambda b, h: (b, 2 * h)),
            pl.BlockSpec((past, LANES), lambda b, h: (b, 2 * h + 1)),
            pl.BlockSpec((past, LANES), lambda b, h: (b, 0)),
            pl.BlockSpec((tq, LANES), lambda b, h: (b, 2 * h)),
            pl.BlockSpec((tq, LANES), lambda b, h: (b, 2 * h + 1)),
            pl.BlockSpec((tq, LANES), lambda b, h: (b, 0)),
        ],
        out_specs=pl.BlockSpec((tq, LANES), lambda b, h: (b, h)),
        out_shape=jax.ShapeDtypeStruct((n, MLA_WIDTH), BF16),
        compiler_params=_params("parallel", "parallel"),
        name="mla_sample",
    )(qn, qr, kv_cache, kv_cache, krd_cache, kv_new, kv_new, krd_new)


def _kv_expand_kernel(c_ref, w_ref, o_ref):
    o_ref[...] = _dot(c_ref[...].astype(BF16), w_ref[...]).astype(BF16)


def _kv_expand(ckv, w_ukv, *, tm):
    n, r = ckv.shape
    tm = min(tm, n)
    assert n % tm == 0
    return pl.pallas_call(
        _kv_expand_kernel,
        grid=(n // tm,),
        in_specs=[pl.BlockSpec((tm, r), lambda i: (i, 0)), _resident(w_ukv.shape)],
        out_specs=pl.BlockSpec((tm, w_ukv.shape[1]), lambda i: (i, 0)),
        out_shape=jax.ShapeDtypeStruct((n, w_ukv.shape[1]), BF16),
        compiler_params=_params("parallel"),
        name="kv_expand",
    )(ckv, w_ukv)


def _proj_out_kernel(x_ref, a_ref, b_ref, wa_ref, wb_ref, o_ref):
    o_ref[...] = x_ref[...] + _dot(a_ref[...], wa_ref[...]) + _dot(b_ref[...], wb_ref[...])


def _proj_out(x, a_da, a_mla, wo_a, wo_b, *, tm):
    n, d = x.shape
    tm = min(tm, n)
    assert n % tm == 0
    row = lambda i: (i, 0)
    return pl.pallas_call(
        _proj_out_kernel,
        grid=(n // tm,),
        in_specs=[pl.BlockSpec((tm, d), row), pl.BlockSpec((tm, DA_WIDTH), row),
                  pl.BlockSpec((tm, MLA_WIDTH), row), _resident(wo_a.shape), _resident(wo_b.shape)],
        out_specs=pl.BlockSpec((tm, d), row),
        out_shape=jax.ShapeDtypeStruct((n, d), F32),
        compiler_params=_params("parallel"),
        name="proj_out",
    )(x, a_da, a_mla, wo_a, wo_b)


def _swap_halves_cols(w):
    half = w.shape[-1] // 2
    return jnp.concatenate([w[..., half:], w[..., :half]], axis=-1)


def _prep_weights(w_in, mix_norm, q_norm, w_uq, kv_norm, w_ukv, w_out):
    o1, o2, o3 = DA_WIDTH, 2 * DA_WIDTH, 3 * DA_WIDTH
    o4, o5 = o3 + Q_LORA, o3 + Q_LORA + KV_LORA
    w_in = w_in.astype(BF16)
    w_kr = w_in[:, o5:]
    w_uq = w_uq.astype(BF16).reshape(Q_LORA, MLA_HEADS, QK_HEAD)
    w_uqr = w_uq[:, :, QK_NOPE:]
    return {
        "mix_norm": mix_norm[None, :],
        "w_q": w_in[:, :o1], "w_k": w_in[:, o1:o2], "w_v": w_in[:, o2:o3],
        "w_cq": w_in[:, o3:o4], "w_ckv": w_in[:, o4:o5],
        "w_kr": jnp.concatenate([w_kr, w_kr], axis=1),
        "w_krs": jnp.concatenate([_swap_halves_cols(w_kr)] * 2, axis=1),
        "q_norm": q_norm[None, :], "kv_norm": kv_norm[None, :],
        "w_uqn": w_uq[:, :, :QK_NOPE].reshape(Q_LORA, MLA_HEADS * QK_NOPE),
        "w_uqr": w_uqr.reshape(Q_LORA, MLA_HEADS * QK_ROPE),
        "w_uqrs": _swap_halves_cols(w_uqr).reshape(Q_LORA, MLA_HEADS * QK_ROPE),
        "w_ukv": w_ukv.astype(BF16),
        "wo_a": w_out[:DA_WIDTH].astype(BF16), "wo_b": w_out[DA_WIDTH:].astype(BF16),
    }


def _rope_tables(pos):
    half = QK_ROPE // 2
    inv = ROPE_THETA ** (-jnp.arange(half, dtype=F32) / half)
    ang = pos.astype(F32)[:, None] * inv[None, :]
    c = jnp.concatenate([jnp.cos(ang)] * 2, axis=1)
    s = jnp.concatenate([-jnp.sin(ang), jnp.sin(ang)], axis=1)
    return {"c2": jnp.tile(c, (1, 2)), "s2": jnp.tile(s, (1, 2)),
            "c8": jnp.tile(c, (1, MLA_HEADS)), "s8": jnp.tile(s, (1, MLA_HEADS))}


def _layer(x, pos, past, ffn1, ffn2, wmix, slopes, lams, subln, final_norm, lam_init, *, batch):
    n, d = x.shape
    t = n // batch
    x = _ffn(x, *ffn1, final_norm, final_norm=False, tm=512, tf=512)
    (q, k_f, k_b, v_f, v_b, ckv, kr, krd, qn, qr, kv) = _proj_in(x, wmix, _rope_tables(pos), tm=256)
    if past is None:
        a_da = _da_prompt(q, k_b, v_b, slopes, lams, subln, t=t, tq=256, lam_init=lam_init)
        a_mla = _mla_prompt(qn, qr, kv, krd, t=t, tq=256)
    else:
        cache_k, cache_v, cache_ckv, cache_kr = past
        p = cache_k.shape[1]
        a_da = _da_sample(q, cache_k.reshape(batch, p, DA_WIDTH), cache_v.reshape(batch, p, DA_WIDTH),
                          k_b, v_b, slopes, lams, subln, lam_init=lam_init)
        kv_cache = _kv_expand(cache_ckv.reshape(batch * p, KV_LORA), wmix["w_ukv"], tm=1024)
        krd_cache = jnp.concatenate([cache_kr, cache_kr], axis=-1).astype(BF16).reshape(batch * p, LANES)
        a_mla = _mla_sample(qn, qr, kv_cache, krd_cache, kv, krd, bd=batch)
    x = _proj_out(x, a_da, a_mla, wmix["wo_a"], wmix["wo_b"], tm=512)
    y = _ffn(x, *ffn2, final_norm, final_norm=True, tm=512, tf=512)
    return y, (k_f, v_f, ckv, kr)


def kernel(x_prompt, x_sample, cache_da_k, cache_da_v, cache_mla_ckv, cache_mla_krope, ffn1_norm, ffn1_w_gate, ffn1_w_up, ffn1_w_down, mix_norm, w_in, da_lambda_q1, da_lambda_k1, da_lambda_q2, da_lambda_k2, da_subln, mla_q_norm, mla_w_uq, mla_kv_norm, mla_w_ukv, w_out, ffn2_norm, ffn2_w_gate, ffn2_w_up, ffn2_w_down, final_norm):
    depth = w_in.shape[0]
    assert depth == 1, "single-layer trunk"
    b, t, d = x_prompt.shape
    bd, td, _ = x_sample.shape
    past_len = cache_da_k.shape[2]
    l = 0
    lam_init = 0.8 - 0.6 * math.exp(-0.3 * l)

    ffn1 = (ffn1_norm[l][None, :], ffn1_w_gate[l].astype(BF16), ffn1_w_up[l].astype(BF16),
            ffn1_w_down[l].astype(BF16))
    ffn2 = (ffn2_norm[l][None, :], ffn2_w_gate[l].astype(BF16), ffn2_w_up[l].astype(BF16),
            ffn2_w_down[l].astype(BF16))
    wmix = _prep_weights(w_in[l], mix_norm[l], mla_q_norm[l], mla_w_uq[l], mla_kv_norm[l], mla_w_ukv[l],
                         w_out[l])
    slopes = jnp.asarray(np.broadcast_to(
        np.array([2.0 ** (-8.0 * (i + 1) / DA_HEADS) for i in range(DA_HEADS)], np.float32)[:, None, None],
        (DA_HEADS, 1, LANES)))
    lams = (da_lambda_q1[l][None, :], da_lambda_k1[l][None, :], da_lambda_q2[l][None, :],
            da_lambda_k2[l][None, :])
    subln = da_subln[l][None, :]
    fnorm = final_norm[None, :]

    y_p, st_p = _layer(x_prompt.reshape(b * t, d), jnp.arange(t, dtype=jnp.int32), None,
                       ffn1, ffn2, wmix, slopes, lams, subln, fnorm, lam_init, batch=b)
    past = (cache_da_k[l], cache_da_v[l], cache_mla_ckv[l], cache_mla_krope[l])
    y_s, st_s = _layer(x_sample.reshape(bd * td, d), past_len + jnp.arange(td, dtype=jnp.int32), past,
                       ffn1, ffn2, wmix, slopes, lams, subln, fnorm, lam_init, batch=bd)

    def state(st, nb, nt):
        k_f, v_f, ckv, kr = st
        return (k_f.reshape(1, nb, nt, DA_HEADS, DA_HEAD_DIM), v_f.reshape(1, nb, nt, DA_HEADS, DA_HEAD_DIM),
                ckv.reshape(1, nb, nt, KV_LORA), kr.reshape(1, nb, nt, QK_ROPE))

    return (y_p.reshape(b, t, d), y_s.reshape(bd, td, d)) + state(st_p, b, t) + state(st_s, bd, td)
```

```python
import functools
import math

import jax
import jax.numpy as jnp
import numpy as np
from jax import lax
from jax.experimental import pallas as pl
from jax.experimental.pallas import tpu as pltpu

F32 = jnp.float32
BF16 = jnp.bfloat16

CHUNK = 64
CHUNK_SHIFT = 6
assert 1 << CHUNK_SHIFT == CHUNK
EPS = 1e-6
NEG_INF = -1e30
DA_HEADS = 8
DA_HALF = 64
DA_HEAD_DIM = 2 * DA_HALF
DA_WIDTH = DA_HEADS * DA_HEAD_DIM
MLA_HEADS = 8
Q_LORA = 512
KV_LORA = 256
QK_NOPE = 128
QK_ROPE = 64
QK_HEAD = QK_NOPE + QK_ROPE
V_HEAD = 128
MLA_WIDTH = MLA_HEADS * V_HEAD
ROPE_THETA = 10000.0
LOG2E = math.log2(math.e)
LANES = 128
assert DA_HEAD_DIM == LANES and QK_NOPE == LANES and V_HEAD == LANES and 2 * QK_ROPE == LANES

VMEM_LIMIT_BYTES = 56 * 1024 * 1024


def _params(*semantics):
    return pltpu.CompilerParams(dimension_semantics=semantics, vmem_limit_bytes=VMEM_LIMIT_BYTES)


def _rms(x, g):
    return x * lax.rsqrt(jnp.mean(x * x, axis=-1, keepdims=True) + EPS) * g


def _dot(a, b):
    return jnp.dot(a, b, preferred_element_type=F32)


def _dot_nt(a, b):
    return lax.dot_general(a, b, (((1,), (1,)), ((), ())), preferred_element_type=F32)


def _resident(shape):
    return pl.BlockSpec(shape, lambda *_: (0,) * len(shape), pipeline_mode=pl.Buffered(1))


def _ffn_kernel(x_ref, g_ref, wg_ref, wu_ref, wd_ref, fn_ref, o_ref, xn_ref, *, final_norm, nj):
    j = pl.program_id(1)

    @pl.when(j == 0)
    def _():
        xn_ref[...] = _rms(x_ref[...], g_ref[...]).astype(BF16)

    def partial_down():
        xn = xn_ref[...]
        gate = _dot(xn, wg_ref[...])
        up = _dot(xn, wu_ref[...])
        h = (gate * jax.nn.sigmoid(gate) * up).astype(BF16)
        return _dot(h, wd_ref[...])

    def finish(acc):
        y = x_ref[...] + 0.5 * acc
        if final_norm:
            y = _rms(y, fn_ref[...])
        o_ref[...] = y

    if nj == 1:
        finish(partial_down())
        return

    @pl.when(j == 0)
    def _():
        o_ref[...] = partial_down()

    @pl.when(jnp.logical_and(j > 0, j < nj - 1))
    def _():
        o_ref[...] += partial_down()

    @pl.when(j == nj - 1)
    def _():
        finish(o_ref[...] + partial_down())


def _ffn(x, norm, wg, wu, wd, fnorm, *, final_norm, tm, tf):
    n, d = x.shape
    f = wg.shape[1]
    tm = min(tm, n)
    tf = min(tf, f)
    assert n % tm == 0 and f % tf == 0
    return pl.pallas_call(
        functools.partial(_ffn_kernel, final_norm=final_norm, nj=f // tf),
        grid=(n // tm, f // tf),
        in_specs=[
            pl.BlockSpec((tm, d), lambda i, j: (i, 0)),
            pl.BlockSpec((1, d), lambda i, j: (0, 0)),
            pl.BlockSpec((d, tf), lambda i, j: (0, j)),
            pl.BlockSpec((d, tf), lambda i, j: (0, j)),
            pl.BlockSpec((tf, d), lambda i, j: (j, 0)),
            pl.BlockSpec((1, d), lambda i, j: (0, 0)),
        ],
        out_specs=pl.BlockSpec((tm, d), lambda i, j: (i, 0)),
        out_shape=jax.ShapeDtypeStruct((n, d), F32),
        scratch_shapes=[pltpu.VMEM((tm, d), BF16)],
        compiler_params=_params("parallel", "arbitrary"),
        name="ffn",
    )(x, norm, wg, wu, wd, fnorm)


def _proj_in_kernel(x_ref, g_ref, wq_ref, wk_ref, wv_ref, wcq_ref, wckv_ref, wkr_ref, wkrs_ref,
                    qn_ref, kvn_ref, wuqn_ref, wuqr_ref, wuqrs_ref, wukv_ref,
                    c2_ref, s2_ref, c8_ref, s8_ref,
                    q_ref, kf_ref, kb_ref, vf_ref, vb_ref, ckv_ref, kr_ref, krd_ref,
                    qnope_ref, qrope_ref, kv_ref):
    h = _rms(x_ref[...], g_ref[...]).astype(BF16)
    q_ref[...] = (_dot(h, wq_ref[...]) * (DA_HALF ** -0.5 * LOG2E)).astype(BF16)
    k = _dot(h, wk_ref[...])
    kf_ref[...] = k
    kb_ref[...] = k.astype(BF16)
    v = _dot(h, wv_ref[...])
    vf_ref[...] = v
    vb_ref[...] = v.astype(BF16)
    cq = _rms(_dot(h, wcq_ref[...]), qn_ref[...]).astype(BF16)
    ckv = _rms(_dot(h, wckv_ref[...]), kvn_ref[...])
    ckv_ref[...] = ckv
    krd = _dot(h, wkr_ref[...]) * c2_ref[...] + _dot(h, wkrs_ref[...]) * s2_ref[...]
    kr_ref[...] = krd[:, :QK_ROPE]
    krd_ref[...] = krd.astype(BF16)
    mla_scale = QK_HEAD ** -0.5 * LOG2E
    qnope_ref[...] = (_dot(cq, wuqn_ref[...]) * mla_scale).astype(BF16)
    qr = _dot(cq, wuqr_ref[...]) * c8_ref[...] + _dot(cq, wuqrs_ref[...]) * s8_ref[...]
    qrope_ref[...] = (qr * mla_scale).astype(BF16)
    kv_ref[...] = _dot(ckv.astype(BF16), wukv_ref[...]).astype(BF16)


def _proj_in(x, w, tabs, *, tm):
    n, d = x.shape
    tm = min(tm, n)
    assert n % tm == 0
    t_rows = tabs["c2"].shape[0]
    assert t_rows % tm == 0 or tm % t_rows == 0
    if tm > t_rows:
        tm = t_rows
    nt = t_rows // tm

    def row(i):
        return (i, 0)

    def trow(i):
        return (i % nt, 0)

    weights = [w["w_q"], w["w_k"], w["w_v"], w["w_cq"], w["w_ckv"], w["w_kr"], w["w_krs"],
               w["q_norm"], w["kv_norm"], w["w_uqn"], w["w_uqr"], w["w_uqrs"], w["w_ukv"]]
    tables = [tabs["c2"], tabs["s2"], tabs["c8"], tabs["s8"]]
    outs = [
        ((n, DA_WIDTH), BF16),
        ((n, DA_WIDTH), F32),
        ((n, DA_WIDTH), BF16),
        ((n, DA_WIDTH), F32),
        ((n, DA_WIDTH), BF16),
        ((n, KV_LORA), F32),
        ((n, QK_ROPE), F32),
        ((n, LANES), BF16),
        ((n, MLA_HEADS * QK_NOPE), BF16),
        ((n, MLA_HEADS * QK_ROPE), BF16),
        ((n, MLA_HEADS * (QK_NOPE + V_HEAD)), BF16),
    ]
    return pl.pallas_call(
        _proj_in_kernel,
        grid=(n // tm,),
        in_specs=([pl.BlockSpec((tm, d), row), _resident((1, d))]
                  + [_resident(a.shape) for a in weights]
                  + [pl.BlockSpec((tm, a.shape[1]), trow) for a in tables]),
        out_specs=[pl.BlockSpec((tm, s[1]), row) for s, _ in outs],
        out_shape=[jax.ShapeDtypeStruct(s, dt) for s, dt in outs],
        compiler_params=_params("parallel"),
        name="proj_in",
    )(x, w["mix_norm"], *weights, *tables)


def _lambda(lq1_ref, lk1_ref, lq2_ref, lk2_ref, lam_init):
    a = jnp.sum(lq1_ref[...] * lk1_ref[...], axis=-1, keepdims=True)
    b = jnp.sum(lq2_ref[...] * lk2_ref[...], axis=-1, keepdims=True)
    return jnp.exp(a) - jnp.exp(b) + lam_init


def _pos_bias(qpos, kpos, slope):
    visible = lax.shift_right_arithmetic(kpos, CHUNK_SHIFT) <= lax.shift_right_arithmetic(qpos, CHUNK_SHIFT)
    if slope is None:
        return jnp.where(visible, 0.0, NEG_INF)
    dist = jnp.abs(qpos - kpos).astype(F32)
    return jnp.where(visible, (-LOG2E * slope) * dist, NEG_INF)


def _strip(tq, t, slope):
    qpos = lax.broadcasted_iota(jnp.int32, (tq, t), 0)
    kpos = lax.broadcasted_iota(jnp.int32, (tq, t), 1) - (t - tq)
    return _pos_bias(qpos, kpos, slope)


def _softmax_parts(s):
    m = jnp.max(s, axis=-1, keepdims=True)
    p = jnp.exp2(s - m)
    return p, jnp.sum(p, axis=-1, keepdims=True)


def _da_combine(p1, l1, p2, l2, lam):
    return (p1 - p2 * (lam * l1 / l2)).astype(BF16), 1.0 / l1


def _split_halves(q):
    lane = lax.broadcasted_iota(jnp.int32, q.shape, 1)
    zero = jnp.zeros_like(q)
    return jnp.where(lane < DA_HALF, q, zero), jnp.where(lane >= DA_HALF, q, zero)


def _da_finish(o, subln, lam_init):
    return (_rms(o, subln) * (1.0 - lam_init)).astype(BF16)


def _da_prompt_kernel(q_ref, k_ref, v_ref, slope_ref, lq1_ref, lk1_ref, lq2_ref, lk2_ref, subln_ref,
                      o_ref, strip_ref, *, tq, lam_init):
    t = q_ref.shape[0]
    lam = _lambda(lq1_ref, lk1_ref, lq2_ref, lk2_ref, lam_init)
    strip_ref[...] = _strip(tq, t, slope_ref[0][:, :1])
    def scores(qi):
        ext = (qi + 1) * tq
        q_lo, q_hi = _split_halves(q_ref[qi * tq:ext, :])
        k = k_ref[0:ext, :]
        bias = strip_ref[:, t - ext:t]
        return _dot_nt(q_lo, k) + bias, _dot_nt(q_hi, k) + bias

    nq = t // tq
    s_next = scores(0)
    for qi in range(nq):
        ext = (qi + 1) * tq
        s1, s2 = s_next
        if qi + 1 < nq:
            s_next = scores(qi + 1)
        p1, l1 = _softmax_parts(s1)
        p2, l2 = _softmax_parts(s2)
        a, w = _da_combine(p1, l1, p2, l2, lam)
        o = _dot(a, v_ref[0:ext, :]) * w
        o_ref[qi * tq:ext, :] = _da_finish(o, subln_ref[...], lam_init)


def _da_prompt(q, k, v, slopes, lams, subln, *, t, tq, lam_init):
    n = q.shape[0]
    tq = min(tq, t)
    blk = pl.BlockSpec((t, LANES), lambda b, h: (b, h))
    vec = lambda a: pl.BlockSpec(a.shape, lambda b, h: (0, 0))
    return pl.pallas_call(
        functools.partial(_da_prompt_kernel, tq=tq, lam_init=lam_init),
        grid=(n // t, DA_HEADS),
        in_specs=[blk, blk, blk, pl.BlockSpec((1, 1, LANES), lambda b, h: (h, 0, 0))]
                 + [vec(a) for a in lams] + [vec(subln)],
        out_specs=blk,
        out_shape=jax.ShapeDtypeStruct((n, DA_WIDTH), BF16),
        scratch_shapes=[pltpu.VMEM((tq, t), F32)],
        compiler_params=_params("parallel", "parallel"),
        name="da_prompt",
    )(q, k, v, slopes, *lams, subln)


def _select_rope(qr, h):
    lane = lax.broadcasted_iota(jnp.int32, qr.shape, 1)
    mine = lax.shift_right_logical(lane, CHUNK_SHIFT) == (h % 2)
    return jnp.where(mine, qr, jnp.zeros_like(qr))


def _mla_prompt_kernel(qn_ref, qr_ref, kn_ref, v_ref, krd_ref, o_ref, mask_ref, kfull_ref, *, tq):
    t = qn_ref.shape[0]
    h = pl.program_id(1)
    mask_ref[...] = _strip(tq, tq, None)
    kfull_ref[:, :LANES] = kn_ref[...]
    kfull_ref[:, LANES:] = krd_ref[...]
    def scores(qi):
        lo, ext = qi * tq, (qi + 1) * tq
        q = jnp.concatenate([qn_ref[lo:ext, :], _select_rope(qr_ref[lo:ext, :], h)], axis=1)
        sd = _dot_nt(q, kfull_ref[lo:ext, :]) + mask_ref[...]
        sp = _dot_nt(q, kfull_ref[0:lo, :]) if qi > 0 else None
        return sd, sp

    nq = t // tq
    s_next = scores(0)
    for qi in range(nq):
        lo, ext = qi * tq, (qi + 1) * tq
        sd, sp = s_next
        if qi + 1 < nq:
            s_next = scores(qi + 1)
        md = jnp.max(sd, axis=-1, keepdims=True)
        if qi == 0:
            pd = jnp.exp2(sd - md)
            l = jnp.sum(pd, axis=-1, keepdims=True)
            o = _dot(pd.astype(BF16), v_ref[lo:ext, :])
        else:
            m = jnp.maximum(md, jnp.max(sp, axis=-1, keepdims=True))
            pd = jnp.exp2(sd - m)
            pp = jnp.exp2(sp - m)
            l = jnp.sum(pd, axis=-1, keepdims=True) + jnp.sum(pp, axis=-1, keepdims=True)
            o = _dot(pp.astype(BF16), v_ref[0:lo, :]) + _dot(pd.astype(BF16), v_ref[lo:ext, :])
        o_ref[lo:ext, :] = (o * (1.0 / l)).astype(BF16)


def _mla_prompt(qn, qr, kv, krd, *, t, tq):
    n = qn.shape[0]
    tq = min(tq, t)
    return pl.pallas_call(
        functools.partial(_mla_prompt_kernel, tq=tq),
        grid=(n // t, MLA_HEADS),
        in_specs=[
            pl.BlockSpec((t, LANES), lambda b, h: (b, h)),
            pl.BlockSpec((t, LANES), lambda b, h: (b, h // 2)),
            pl.BlockSpec((t, LANES), lambda b, h: (b, 2 * h)),
            pl.BlockSpec((t, LANES), lambda b, h: (b, 2 * h + 1)),
            pl.BlockSpec((t, LANES), lambda b, h: (b, 0)),
        ],
        out_specs=pl.BlockSpec((t, LANES), lambda b, h: (b, h)),
        out_shape=jax.ShapeDtypeStruct((n, MLA_WIDTH), BF16),
        scratch_shapes=[pltpu.VMEM((tq, tq), F32), pltpu.VMEM((t, 2 * LANES), BF16)],
        compiler_params=_params("parallel", "parallel"),
        name="mla_prompt",
    )(qn, qr, kv, kv, krd)


def _positions(tq, tk, q0, k0):
    qpos = lax.broadcasted_iota(jnp.int32, (tq, tk), 0) + q0
    kpos = lax.broadcasted_iota(jnp.int32, (tq, tk), 1) + k0
    return qpos, kpos


def _da_sample_kernel(q_ref, ck_ref, cv_ref, nk_ref, nv_ref, slope_ref, lq1_ref, lk1_ref, lq2_ref, lk2_ref,
                      subln_ref, o_ref, *, lam_init):
    tq = q_ref.shape[0]
    past = ck_ref.shape[0]
    lam = _lambda(lq1_ref, lk1_ref, lq2_ref, lk2_ref, lam_init)
    slope = slope_ref[0][:, :1]
    bias_c = _pos_bias(*_positions(tq, past, past, 0), slope)
    bias_n = _pos_bias(*_positions(tq, tq, past, past), slope)
    ck = ck_ref[...].astype(BF16)
    nk = nk_ref[...]

    def parts(qh):
        sc = _dot_nt(qh, ck) + bias_c
        sn = _dot_nt(qh, nk) + bias_n
        m = jnp.maximum(jnp.max(sc, axis=-1, keepdims=True), jnp.max(sn, axis=-1, keepdims=True))
        pc = jnp.exp2(sc - m)
        pn = jnp.exp2(sn - m)
        l = jnp.sum(pc, axis=-1, keepdims=True) + jnp.sum(pn, axis=-1, keepdims=True)
        return pc, pn, l

    q_lo, q_hi = _split_halves(q_ref[...])
    pc1, pn1, l1 = parts(q_lo)
    pc2, pn2, l2 = parts(q_hi)
    ac, w = _da_combine(pc1, l1, pc2, l2, lam)
    an, _ = _da_combine(pn1, l1, pn2, l2, lam)
    o = (_dot(ac, cv_ref[...].astype(BF16)) + _dot(an, nv_ref[...])) * w
    o_ref[...] = _da_finish(o, subln_ref[...], lam_init)


def _da_sample(q, cache_k, cache_v, nk, nv, slopes, lams, subln, *, lam_init):
    bd, past, _ = cache_k.shape
    n = q.shape[0]
    tq = n // bd
    new = pl.BlockSpec((tq, LANES), lambda b, h: (b, h))
    cache = pl.BlockSpec((None, past, LANES), lambda b, h: (b, 0, h))
    vec = lambda a: pl.BlockSpec(a.shape, lambda b, h: (0, 0))
    return pl.pallas_call(
        functools.partial(_da_sample_kernel, lam_init=lam_init),
        grid=(bd, DA_HEADS),
        in_specs=[new, cache, cache, new, new, pl.BlockSpec((1, 1, LANES), lambda b, h: (h, 0, 0))]
                 + [vec(a) for a in lams] + [vec(subln)],
        out_specs=new,
        out_shape=jax.ShapeDtypeStruct((n, DA_WIDTH), BF16),
        compiler_params=_params("parallel", "parallel"),
        name="da_sample",
    )(q, cache_k, cache_v, nk, nv, slopes, *lams, subln)


def _mla_sample_kernel(qn_ref, qr_ref, ckn_ref, cv_ref, ckrd_ref, nkn_ref, nv_ref, nkrd_ref, o_ref):
    tq = qn_ref.shape[0]
    past = ckn_ref.shape[0]
    h = pl.program_id(1)
    q = jnp.concatenate([qn_ref[...], _select_rope(qr_ref[...], h)], axis=1)
    kc = jnp.concatenate([ckn_ref[...], ckrd_ref[...]], axis=1)
    kn = jnp.concatenate([nkn_ref[...], nkrd_ref[...]], axis=1)
    sc = _dot_nt(q, kc) + _pos_bias(*_positions(tq, past, past, 0), None)
    sn = _dot_nt(q, kn) + _pos_bias(*_positions(tq, tq, past, past), None)
    m = jnp.maximum(jnp.max(sc, axis=-1, keepdims=True), jnp.max(sn, axis=-1, keepdims=True))
    pc = jnp.exp2(sc - m)
    pn = jnp.exp2(sn - m)
    w = 1.0 / (jnp.sum(pc, axis=-1, keepdims=True) + jnp.sum(pn, axis=-1, keepdims=True))
    o = _dot(pc.astype(BF16), cv_ref[...]) + _dot(pn.astype(BF16), nv_ref[...])
    o_ref[...] = (o * w).astype(BF16)


def _mla_sample(qn, qr, kv_cache, krd_cache, kv_new, krd_new, *, bd):
    n = qn.shape[0]
    tq = n // bd
    past = kv_cache.shape[0] // bd
    return pl.pallas_call(
        _mla_sample_kernel,
        grid=(bd, MLA_HEADS),
        in_specs=[
            pl.BlockSpec((tq, LANES), lambda b, h: (b, h)),
            pl.BlockSpec((tq, LANES), lambda b, h: (b, h // 2)),
            pl.BlockSpec((past, LANES), lambda b, h: (b, 2 * h)),
            pl.BlockSpec((past, LANES), lambda b, h: (b, 2 * h + 1)),
            pl.BlockSpec((past, LANES), lambda b, h: (b, 0)),
            pl.BlockSpec((tq, LANES), lambda b, h: (b, 2 * h)),
            pl.BlockSpec((tq, LANES), lambda b, h: (b, 2 * h + 1)),
            pl.BlockSpec((tq, LANES), lambda b, h: (b, 0)),
        ],
        out_specs=pl.BlockSpec((tq, LANES), lambda b, h: (b, h)),
        out_shape=jax.ShapeDtypeStruct((n, MLA_WIDTH), BF16),
        compiler_params=_params("parallel", "parallel"),
        name="mla_sample",
    )(qn, qr, kv_cache, kv_cache, krd_cache, kv_new, kv_new, krd_new)


def _kv_expand_kernel(c_ref, w_ref, o_ref):
    o_ref[...] = _dot(c_ref[...].astype(BF16), w_ref[...]).astype(BF16)


def _kv_expand(ckv, w_ukv, *, tm):
    n, r = ckv.shape
    tm = min(tm, n)
    assert n % tm == 0
    return pl.pallas_call(
        _kv_expand_kernel,
        grid=(n // tm,),
        in_specs=[pl.BlockSpec((tm, r), lambda i: (i, 0)), _resident(w_ukv.shape)],
        out_specs=pl.BlockSpec((tm, w_ukv.shape[1]), lambda i: (i, 0)),
        out_shape=jax.ShapeDtypeStruct((n, w_ukv.shape[1]), BF16),
        compiler_params=_params("parallel"),
        name="kv_expand",
    )(ckv, w_ukv)


def _proj_out_kernel(x_ref, a_ref, b_ref, wa_ref, wb_ref, o_ref):
    o_ref[...] = x_ref[...] + _dot(a_ref[...], wa_ref[...]) + _dot(b_ref[...], wb_ref[...])


def _proj_out(x, a_da, a_mla, wo_a, wo_b, *, tm):
    n, d = x.shape
    tm = min(tm, n)
    assert n % tm == 0
    row = lambda i: (i, 0)
    return pl.pallas_call(
        _proj_out_kernel,
        grid=(n // tm,),
        in_specs=[pl.BlockSpec((tm, d), row), pl.BlockSpec((tm, DA_WIDTH), row),
                  pl.BlockSpec((tm, MLA_WIDTH), row), _resident(wo_a.shape), _resident(wo_b.shape)],
        out_specs=pl.BlockSpec((tm, d), row),
        out_shape=jax.ShapeDtypeStruct((n, d), F32),
        compiler_params=_params("parallel"),
        name="proj_out",
    )(x, a_da, a_mla, wo_a, wo_b)


def _swap_halves_cols(w):
    half = w.shape[-1] // 2
    return jnp.concatenate([w[..., half:], w[..., :half]], axis=-1)


def _prep_weights(w_in, mix_norm, q_norm, w_uq, kv_norm, w_ukv, w_out):
    o1, o2, o3 = DA_WIDTH, 2 * DA_WIDTH, 3 * DA_WIDTH
    o4, o5 = o3 + Q_LORA, o3 + Q_LORA + KV_LORA
    w_in = w_in.astype(BF16)
    w_kr = w_in[:, o5:]
    w_uq = w_uq.astype(BF16).reshape(Q_LORA, MLA_HEADS, QK_HEAD)
    w_uqr = w_uq[:, :, QK_NOPE:]
    return {
        "mix_norm": mix_norm[None, :],
        "w_q": w_in[:, :o1], "w_k": w_in[:, o1:o2], "w_v": w_in[:, o2:o3],
        "w_cq": w_in[:, o3:o4], "w_ckv": w_in[:, o4:o5],
        "w_kr": jnp.concatenate([w_kr, w_kr], axis=1),
        "w_krs": jnp.concatenate([_swap_halves_cols(w_kr)] * 2, axis=1),
        "q_norm": q_norm[None, :], "kv_norm": kv_norm[None, :],
        "w_uqn": w_uq[:, :, :QK_NOPE].reshape(Q_LORA, MLA_HEADS * QK_NOPE),
        "w_uqr": w_uqr.reshape(Q_LORA, MLA_HEADS * QK_ROPE),
        "w_uqrs": _swap_halves_cols(w_uqr).reshape(Q_LORA, MLA_HEADS * QK_ROPE),
        "w_ukv": w_ukv.astype(BF16),
        "wo_a": w_out[:DA_WIDTH].astype(BF16), "wo_b": w_out[DA_WIDTH:].astype(BF16),
    }


def _rope_tables(pos):
    half = QK_ROPE // 2
    inv = ROPE_THETA ** (-jnp.arange(half, dtype=F32) / half)
    ang = pos.astype(F32)[:, None] * inv[None, :]
    c = jnp.concatenate([jnp.cos(ang)] * 2, axis=1)
    s = jnp.concatenate([-jnp.sin(ang), jnp.sin(ang)], axis=1)
    return {"c2": jnp.tile(c, (1, 2)), "s2": jnp.tile(s, (1, 2)),
            "c8": jnp.tile(c, (1, MLA_HEADS)), "s8": jnp.tile(s, (1, MLA_HEADS))}


def _layer(x, pos, past, ffn1, ffn2, wmix, slopes, lams, subln, final_norm, lam_init, *, batch):
    n, d = x.shape
    t = n // batch
    x = _ffn(x, *ffn1, final_norm, final_norm=False, tm=512, tf=512)
    (q, k_f, k_b, v_f, v_b, ckv, kr, krd, qn, qr, kv) = _proj_in(x, wmix, _rope_tables(pos), tm=256)
    if past is None:
        a_da = _da_prompt(q, k_b, v_b, slopes, lams, subln, t=t, tq=256, lam_init=lam_init)
        a_mla = _mla_prompt(qn, qr, kv, krd, t=t, tq=256)
    else:
        cache_k, cache_v, cache_ckv, cache_kr = past
        p = cache_k.size // (batch * DA_WIDTH)
        a_da = _da_sample(q, cache_k.reshape(batch, p, DA_WIDTH), cache_v.reshape(batch, p, DA_WIDTH),
                          k_b, v_b, slopes, lams, subln, lam_init=lam_init)
        kv_cache = _kv_expand(cache_ckv.reshape(batch * p, KV_LORA), wmix["w_ukv"], tm=1024)
        cache_kr = cache_kr.reshape(batch * p, QK_ROPE)
        krd_cache = jnp.concatenate([cache_kr, cache_kr], axis=-1).astype(BF16)
        a_mla = _mla_sample(qn, qr, kv_cache, krd_cache, kv, krd, bd=batch)
    x = _proj_out(x, a_da, a_mla, wmix["wo_a"], wmix["wo_b"], tm=512)
    y = _ffn(x, *ffn2, final_norm, final_norm=True, tm=512, tf=512)
    return y, (k_f, v_f, ckv, kr)


def kernel(x_prompt, x_sample, cache_da_k, cache_da_v, cache_mla_ckv, cache_mla_krope, ffn1_norm, ffn1_w_gate, ffn1_w_up, ffn1_w_down, mix_norm, w_in, da_lambda_q1, da_lambda_k1, da_lambda_q2, da_lambda_k2, da_subln, mla_q_norm, mla_w_uq, mla_kv_norm, mla_w_ukv, w_out, ffn2_norm, ffn2_w_gate, ffn2_w_up, ffn2_w_down, final_norm):
    depth = w_in.shape[0]
    assert depth == 1, "single-layer trunk"
    b, t, d = x_prompt.shape
    bd, td, _ = x_sample.shape
    past_len = cache_da_k.shape[2]
    l = 0
    lam_init = 0.8 - 0.6 * math.exp(-0.3 * l)

    ffn1 = (ffn1_norm[l][None, :], ffn1_w_gate[l].astype(BF16), ffn1_w_up[l].astype(BF16),
            ffn1_w_down[l].astype(BF16))
    ffn2 = (ffn2_norm[l][None, :], ffn2_w_gate[l].astype(BF16), ffn2_w_up[l].astype(BF16),
            ffn2_w_down[l].astype(BF16))
    wmix = _prep_weights(w_in[l], mix_norm[l], mla_q_norm[l], mla_w_uq[l], mla_kv_norm[l], mla_w_ukv[l],
                         w_out[l])
    slopes = jnp.asarray(np.broadcast_to(
        np.array([2.0 ** (-8.0 * (i + 1) / DA_HEADS) for i in range(DA_HEADS)], np.float32)[:, None, None],
        (DA_HEADS, 1, LANES)))
    lams = (da_lambda_q1[l][None, :], da_lambda_k1[l][None, :], da_lambda_q2[l][None, :],
            da_lambda_k2[l][None, :])
    subln = da_subln[l][None, :]
    fnorm = final_norm[None, :]

    y_p, st_p = _layer(x_prompt.reshape(b * t, d), jnp.arange(t, dtype=jnp.int32), None,
                       ffn1, ffn2, wmix, slopes, lams, subln, fnorm, lam_init, batch=b)
    past = (cache_da_k, cache_da_v, cache_mla_ckv, cache_mla_krope)
    y_s, st_s = _layer(x_sample.reshape(bd * td, d), past_len + jnp.arange(td, dtype=jnp.int32), past,
                       ffn1, ffn2, wmix, slopes, lams, subln, fnorm, lam_init, batch=bd)

    def state(st, nb, nt):
        k_f, v_f, ckv, kr = st
        return (k_f.reshape(1, nb, nt, DA_HEADS, DA_HEAD_DIM), v_f.reshape(1, nb, nt, DA_HEADS, DA_HEAD_DIM),
                ckv.reshape(1, nb, nt, KV_LORA), kr.reshape(1, nb, nt, QK_ROPE))

    return (y_p.reshape(b, t, d), y_s.reshape(bd, td, d)) + state(st_p, b, t) + state(st_s, bd, td)
```

```python
import functools
import math

import jax
import jax.numpy as jnp
import numpy as np
from jax import lax
from jax.experimental import pallas as pl
from jax.experimental.pallas import tpu as pltpu

F32 = jnp.float32
BF16 = jnp.bfloat16

CHUNK = 64
CHUNK_SHIFT = 6
assert 1 << CHUNK_SHIFT == CHUNK
EPS = 1e-6
NEG_INF = -1e30
DA_HEADS = 8
DA_HALF = 64
DA_HEAD_DIM = 2 * DA_HALF
DA_WIDTH = DA_HEADS * DA_HEAD_DIM
MLA_HEADS = 8
Q_LORA = 512
KV_LORA = 256
QK_NOPE = 128
QK_ROPE = 64
QK_HEAD = QK_NOPE + QK_ROPE
V_HEAD = 128
MLA_WIDTH = MLA_HEADS * V_HEAD
ROPE_THETA = 10000.0
LOG2E = math.log2(math.e)
LANES = 128
assert DA_HEAD_DIM == LANES and QK_NOPE == LANES and V_HEAD == LANES and 2 * QK_ROPE == LANES

VMEM_LIMIT_BYTES = 56 * 1024 * 1024


def _params(*semantics):
    return pltpu.CompilerParams(dimension_semantics=semantics, vmem_limit_bytes=VMEM_LIMIT_BYTES)


def _rms(x, g):
    return x * lax.rsqrt(jnp.mean(x * x, axis=-1, keepdims=True) + EPS) * g


def _dot(a, b):
    return jnp.dot(a, b, preferred_element_type=F32)


def _dot_nt(a, b):
    return lax.dot_general(a, b, (((1,), (1,)), ((), ())), preferred_element_type=F32)


def _resident(shape):
    return pl.BlockSpec(shape, lambda *_: (0,) * len(shape), pipeline_mode=pl.Buffered(1))


def _ffn_kernel(x_ref, g_ref, wg_ref, wu_ref, wd_ref, fn_ref, o_ref, xn_ref, *, final_norm, nj):
    j = pl.program_id(1)

    @pl.when(j == 0)
    def _():
        xn_ref[...] = _rms(x_ref[...], g_ref[...]).astype(BF16)

    def partial_down():
        xn = xn_ref[...]
        gate = _dot(xn, wg_ref[...])
        up = _dot(xn, wu_ref[...])
        h = (gate * jax.nn.sigmoid(gate) * up).astype(BF16)
        return _dot(h, wd_ref[...])

    def finish(acc):
        y = x_ref[...] + 0.5 * acc
        if final_norm:
            y = _rms(y, fn_ref[...])
        o_ref[...] = y

    if nj == 1:
        finish(partial_down())
        return

    @pl.when(j == 0)
    def _():
        o_ref[...] = partial_down()

    @pl.when(jnp.logical_and(j > 0, j < nj - 1))
    def _():
        o_ref[...] += partial_down()

    @pl.when(j == nj - 1)
    def _():
        finish(o_ref[...] + partial_down())


def _ffn(x, norm, wg, wu, wd, fnorm, *, final_norm, tm, tf):
    n, d = x.shape
    f = wg.shape[1]
    tm = min(tm, n)
    tf = min(tf, f)
    assert n % tm == 0 and f % tf == 0
    return pl.pallas_call(
        functools.partial(_ffn_kernel, final_norm=final_norm, nj=f // tf),
        grid=(n // tm, f // tf),
        in_specs=[
            pl.BlockSpec((tm, d), lambda i, j: (i, 0)),
            pl.BlockSpec((1, d), lambda i, j: (0, 0)),
            pl.BlockSpec((d, tf), lambda i, j: (0, j)),
            pl.BlockSpec((d, tf), lambda i, j: (0, j)),
            pl.BlockSpec((tf, d), lambda i, j: (j, 0)),
            pl.BlockSpec((1, d), lambda i, j: (0, 0)),
        ],
        out_specs=pl.BlockSpec((tm, d), lambda i, j: (i, 0)),
        out_shape=jax.ShapeDtypeStruct((n, d), F32),
        scratch_shapes=[pltpu.VMEM((tm, d), BF16)],
        compiler_params=_params("parallel", "arbitrary"),
        name="ffn",
    )(x, norm, wg, wu, wd, fnorm)


def _proj_in_kernel(x_ref, g_ref, wq_ref, wk_ref, wv_ref, wcq_ref, wckv_ref, wkr_ref, wkrs_ref,
                    qn_ref, kvn_ref, wuqn_ref, wuqr_ref, wuqrs_ref, wukv_ref,
                    c2_ref, s2_ref, c8_ref, s8_ref,
                    q_ref, kf_ref, kb_ref, vf_ref, vb_ref, ckv_ref, kr_ref, krd_ref,
                    qnope_ref, qrope_ref, kv_ref):
    h = _rms(x_ref[...], g_ref[...]).astype(BF16)
    q_ref[...] = (_dot(h, wq_ref[...]) * (DA_HALF ** -0.5 * LOG2E)).astype(BF16)
    k = _dot(h, wk_ref[...])
    kf_ref[...] = k
    kb_ref[...] = k.astype(BF16)
    v = _dot(h, wv_ref[...])
    vf_ref[...] = v
    vb_ref[...] = v.astype(BF16)
    cq = _rms(_dot(h, wcq_ref[...]), qn_ref[...]).astype(BF16)
    ckv = _rms(_dot(h, wckv_ref[...]), kvn_ref[...])
    ckv_ref[...] = ckv
    krd = _dot(h, wkr_ref[...]) * c2_ref[...] + _dot(h, wkrs_ref[...]) * s2_ref[...]
    kr_ref[...] = krd[:, :QK_ROPE]
    krd_ref[...] = krd.astype(BF16)
    mla_scale = QK_HEAD ** -0.5 * LOG2E
    qnope_ref[...] = (_dot(cq, wuqn_ref[...]) * mla_scale).astype(BF16)
    qr = _dot(cq, wuqr_ref[...]) * c8_ref[...] + _dot(cq, wuqrs_ref[...]) * s8_ref[...]
    qrope_ref[...] = (qr * mla_scale).astype(BF16)
    kv_ref[...] = _dot(ckv.astype(BF16), wukv_ref[...]).astype(BF16)


def _proj_in(x, w, tabs, *, tm):
    n, d = x.shape
    tm = min(tm, n)
    assert n % tm == 0
    t_rows = tabs["c2"].shape[0]
    assert t_rows % tm == 0 or tm % t_rows == 0
    if tm > t_rows:
        tm = t_rows
    nt = t_rows // tm

    def row(i):
        return (i, 0)

    def trow(i):
        return (i % nt, 0)

    weights = [w["w_q"], w["w_k"], w["w_v"], w["w_cq"], w["w_ckv"], w["w_kr"], w["w_krs"],
               w["q_norm"], w["kv_norm"], w["w_uqn"], w["w_uqr"], w["w_uqrs"], w["w_ukv"]]
    tables = [tabs["c2"], tabs["s2"], tabs["c8"], tabs["s8"]]
    outs = [
        ((n, DA_WIDTH), BF16),
        ((n, DA_WIDTH), F32),
        ((n, DA_WIDTH), BF16),
        ((n, DA_WIDTH), F32),
        ((n, DA_WIDTH), BF16),
        ((n, KV_LORA), F32),
        ((n, QK_ROPE), F32),
        ((n, LANES), BF16),
        ((n, MLA_HEADS * QK_NOPE), BF16),
        ((n, MLA_HEADS * QK_ROPE), BF16),
        ((n, MLA_HEADS * (QK_NOPE + V_HEAD)), BF16),
    ]
    return pl.pallas_call(
        _proj_in_kernel,
        grid=(n // tm,),
        in_specs=([pl.BlockSpec((tm, d), row), _resident((1, d))]
                  + [_resident(a.shape) for a in weights]
                  + [pl.BlockSpec((tm, a.shape[1]), trow) for a in tables]),
        out_specs=[pl.BlockSpec((tm, s[1]), row) for s, _ in outs],
        out_shape=[jax.ShapeDtypeStruct(s, dt) for s, dt in outs],
        compiler_params=_params("parallel"),
        name="proj_in",
    )(x, w["mix_norm"], *weights, *tables)


def _lambda(lq1_ref, lk1_ref, lq2_ref, lk2_ref, lam_init):
    a = jnp.sum(lq1_ref[...] * lk1_ref[...], axis=-1, keepdims=True)
    b = jnp.sum(lq2_ref[...] * lk2_ref[...], axis=-1, keepdims=True)
    return jnp.exp(a) - jnp.exp(b) + lam_init


def _pos_bias(qpos, kpos, slope):
    visible = lax.shift_right_arithmetic(kpos, CHUNK_SHIFT) <= lax.shift_right_arithmetic(qpos, CHUNK_SHIFT)
    if slope is None:
        return jnp.where(visible, 0.0, NEG_INF)
    dist = jnp.abs(qpos - kpos).astype(F32)
    return jnp.where(visible, (-LOG2E * slope) * dist, NEG_INF)


def _strip(tq, t, slope):
    qpos = lax.broadcasted_iota(jnp.int32, (tq, t), 0)
    kpos = lax.broadcasted_iota(jnp.int32, (tq, t), 1) - (t - tq)
    return _pos_bias(qpos, kpos, slope)


def _softmax_parts(s):
    m = jnp.max(s, axis=-1, keepdims=True)
    p = jnp.exp2(s - m)
    return p, jnp.sum(p, axis=-1, keepdims=True)


def _da_combine(p1, l1, p2, l2, lam):
    return (p1 - p2 * (lam * l1 / l2)).astype(BF16), 1.0 / l1


def _split_halves(q):
    lane = lax.broadcasted_iota(jnp.int32, q.shape, 1)
    zero = jnp.zeros_like(q)
    return jnp.where(lane < DA_HALF, q, zero), jnp.where(lane >= DA_HALF, q, zero)


def _da_finish(o, subln, lam_init):
    return (_rms(o, subln) * (1.0 - lam_init)).astype(BF16)


def _da_prompt_kernel(q_ref, k_ref, v_ref, slope_ref, lq1_ref, lk1_ref, lq2_ref, lk2_ref, subln_ref,
                      o_ref, strip_ref, *, tq, lam_init):
    t = q_ref.shape[0]
    lam = _lambda(lq1_ref, lk1_ref, lq2_ref, lk2_ref, lam_init)
    strip_ref[...] = _strip(tq, t, slope_ref[0][:, :1])
    def scores(qi):
        ext = (qi + 1) * tq
        q_lo, q_hi = _split_halves(q_ref[qi * tq:ext, :])
        k = k_ref[0:ext, :]
        bias = strip_ref[:, t - ext:t]
        return _dot_nt(q_lo, k) + bias, _dot_nt(q_hi, k) + bias

    nq = t // tq
    s_next = scores(0)
    for qi in range(nq):
        ext = (qi + 1) * tq
        s1, s2 = s_next
        if qi + 1 < nq:
            s_next = scores(qi + 1)
        p1, l1 = _softmax_parts(s1)
        p2, l2 = _softmax_parts(s2)
        a, w = _da_combine(p1, l1, p2, l2, lam)
        o = _dot(a, v_ref[0:ext, :]) * w
        o_ref[qi * tq:ext, :] = _da_finish(o, subln_ref[...], lam_init)


def _da_prompt(q, k, v, slopes, lams, subln, *, t, tq, lam_init):
    n = q.shape[0]
    tq = min(tq, t)
    blk = pl.BlockSpec((t, LANES), lambda b, h: (b, h))
    vec = lambda a: pl.BlockSpec(a.shape, lambda b, h: (0, 0))
    return pl.pallas_call(
        functools.partial(_da_prompt_kernel, tq=tq, lam_init=lam_init),
        grid=(n // t, DA_HEADS),
        in_specs=[blk, blk, blk, pl.BlockSpec((1, 1, LANES), lambda b, h: (h, 0, 0))]
                 + [vec(a) for a in lams] + [vec(subln)],
        out_specs=blk,
        out_shape=jax.ShapeDtypeStruct((n, DA_WIDTH), BF16),
        scratch_shapes=[pltpu.VMEM((tq, t), F32)],
        compiler_params=_params("parallel", "parallel"),
        name="da_prompt",
    )(q, k, v, slopes, *lams, subln)


def _select_rope(qr, h):
    lane = lax.broadcasted_iota(jnp.int32, qr.shape, 1)
    mine = lax.shift_right_logical(lane, CHUNK_SHIFT) == (h % 2)
    return jnp.where(mine, qr, jnp.zeros_like(qr))


def _mla_prompt_kernel(qn_ref, qr_ref, kn_ref, v_ref, krd_ref, o_ref, mask_ref, kfull_ref, *, tq):
    t = qn_ref.shape[0]
    h = pl.program_id(1)
    mask_ref[...] = _strip(tq, tq, None)
    kfull_ref[:, :LANES] = kn_ref[...]
    kfull_ref[:, LANES:] = krd_ref[...]
    def scores(qi):
        lo, ext = qi * tq, (qi + 1) * tq
        q = jnp.concatenate([qn_ref[lo:ext, :], _select_rope(qr_ref[lo:ext, :], h)], axis=1)
        sd = _dot_nt(q, kfull_ref[lo:ext, :]) + mask_ref[...]
        sp = _dot_nt(q, kfull_ref[0:lo, :]) if qi > 0 else None
        return sd, sp

    nq = t // tq
    s_next = scores(0)
    for qi in range(nq):
        lo, ext = qi * tq, (qi + 1) * tq
        sd, sp = s_next
        if qi + 1 < nq:
            s_next = scores(qi + 1)
        md = jnp.max(sd, axis=-1, keepdims=True)
        if qi == 0:
            pd = jnp.exp2(sd - md)
            l = jnp.sum(pd, axis=-1, keepdims=True)
            o = _dot(pd.astype(BF16), v_ref[lo:ext, :])
        else:
            m = jnp.maximum(md, jnp.max(sp, axis=-1, keepdims=True))
            pd = jnp.exp2(sd - m)
            pp = jnp.exp2(sp - m)
            l = jnp.sum(pd, axis=-1, keepdims=True) + jnp.sum(pp, axis=-1, keepdims=True)
            o = _dot(pp.astype(BF16), v_ref[0:lo, :]) + _dot(pd.astype(BF16), v_ref[lo:ext, :])
        o_ref[lo:ext, :] = (o * (1.0 / l)).astype(BF16)


def _mla_prompt(qn, qr, kv, krd, *, t, tq):
    n = qn.shape[0]
    tq = min(tq, t)
    return pl.pallas_call(
        functools.partial(_mla_prompt_kernel, tq=tq),
        grid=(n // t, MLA_HEADS),
        in_specs=[
            pl.BlockSpec((t, LANES), lambda b, h: (b, h)),
            pl.BlockSpec((t, LANES), lambda b, h: (b, h // 2)),
            pl.BlockSpec((t, LANES), lambda b, h: (b, 2 * h)),
            pl.BlockSpec((t, LANES), lambda b, h: (b, 2 * h + 1)),
            pl.BlockSpec((t, LANES), lambda b, h: (b, 0)),
        ],
        out_specs=pl.BlockSpec((t, LANES), lambda b, h: (b, h)),
        out_shape=jax.ShapeDtypeStruct((n, MLA_WIDTH), BF16),
        scratch_shapes=[pltpu.VMEM((tq, tq), F32), pltpu.VMEM((t, 2 * LANES), BF16)],
        compiler_params=_params("parallel", "parallel"),
        name="mla_prompt",
    )(qn, qr, kv, kv, krd)


def _positions(tq, tk, q0, k0):
    qpos = lax.broadcasted_iota(jnp.int32, (tq, tk), 0) + q0
    kpos = lax.broadcasted_iota(jnp.int32, (tq, tk), 1) + k0
    return qpos, kpos


def _da_sample_kernel(q_ref, ck_ref, cv_ref, nk_ref, nv_ref, slope_ref, lq1_ref, lk1_ref, lq2_ref, lk2_ref,
                      subln_ref, o_ref, *, lam_init):
    tq = q_ref.shape[0]
    past = ck_ref.shape[0]
    lam = _lambda(lq1_ref, lk1_ref, lq2_ref, lk2_ref, lam_init)
    slope = slope_ref[0][:, :1]
    bias_c = _pos_bias(*_positions(tq, past, past, 0), slope)
    bias_n = _pos_bias(*_positions(tq, tq, past, past), slope)
    ck = ck_ref[...]
    nk = nk_ref[...]

    def parts(qh):
        sc = _dot_nt(qh, ck) + bias_c
        sn = _dot_nt(qh, nk) + bias_n
        m = jnp.maximum(jnp.max(sc, axis=-1, keepdims=True), jnp.max(sn, axis=-1, keepdims=True))
        pc = jnp.exp2(sc - m)
        pn = jnp.exp2(sn - m)
        l = jnp.sum(pc, axis=-1, keepdims=True) + jnp.sum(pn, axis=-1, keepdims=True)
        return pc, pn, l

    q_lo, q_hi = _split_halves(q_ref[...])
    pc1, pn1, l1 = parts(q_lo)
    pc2, pn2, l2 = parts(q_hi)
    ac, w = _da_combine(pc1, l1, pc2, l2, lam)
    an, _ = _da_combine(pn1, l1, pn2, l2, lam)
    o = (_dot(ac, cv_ref[...]) + _dot(an, nv_ref[...])) * w
    o_ref[...] = _da_finish(o, subln_ref[...], lam_init)


def _da_sample(q, cache_k, cache_v, nk, nv, slopes, lams, subln, *, lam_init):
    bd, past, _ = cache_k.shape
    n = q.shape[0]
    tq = n // bd
    new = pl.BlockSpec((tq, LANES), lambda b, h: (b, h))
    cache = pl.BlockSpec((None, past, LANES), lambda b, h: (b, 0, h))
    vec = lambda a: pl.BlockSpec(a.shape, lambda b, h: (0, 0))
    return pl.pallas_call(
        functools.partial(_da_sample_kernel, lam_init=lam_init),
        grid=(bd, DA_HEADS),
        in_specs=[new, cache, cache, new, new, pl.BlockSpec((1, 1, LANES), lambda b, h: (h, 0, 0))]
                 + [vec(a) for a in lams] + [vec(subln)],
        out_specs=new,
        out_shape=jax.ShapeDtypeStruct((n, DA_WIDTH), BF16),
        compiler_params=_params("parallel", "parallel"),
        name="da_sample",
    )(q, cache_k, cache_v, nk, nv, slopes, *lams, subln)


def _mla_sample_kernel(qn_ref, qr_ref, ckn_ref, cv_ref, ckrd_ref, nkn_ref, nv_ref, nkrd_ref, o_ref):
    tq = qn_ref.shape[0]
    past = ckn_ref.shape[0]
    h = pl.program_id(1)
    q = jnp.concatenate([qn_ref[...], _select_rope(qr_ref[...], h)], axis=1)
    kc = jnp.concatenate([ckn_ref[...], ckrd_ref[...]], axis=1)
    kn = jnp.concatenate([nkn_ref[...], nkrd_ref[...]], axis=1)
    sc = _dot_nt(q, kc) + _pos_bias(*_positions(tq, past, past, 0), None)
    sn = _dot_nt(q, kn) + _pos_bias(*_positions(tq, tq, past, past), None)
    m = jnp.maximum(jnp.max(sc, axis=-1, keepdims=True), jnp.max(sn, axis=-1, keepdims=True))
    pc = jnp.exp2(sc - m)
    pn = jnp.exp2(sn - m)
    w = 1.0 / (jnp.sum(pc, axis=-1, keepdims=True) + jnp.sum(pn, axis=-1, keepdims=True))
    o = _dot(pc.astype(BF16), cv_ref[...]) + _dot(pn.astype(BF16), nv_ref[...])
    o_ref[...] = (o * w).astype(BF16)


def _mla_sample(qn, qr, kv_cache, krd_cache, kv_new, krd_new, *, bd):
    n = qn.shape[0]
    tq = n // bd
    past = kv_cache.shape[0] // bd
    return pl.pallas_call(
        _mla_sample_kernel,
        grid=(bd, MLA_HEADS),
        in_specs=[
            pl.BlockSpec((tq, LANES), lambda b, h: (b, h)),
            pl.BlockSpec((tq, LANES), lambda b, h: (b, h // 2)),
            pl.BlockSpec((past, LANES), lambda b, h: (b, 2 * h)),
            pl.BlockSpec((past, LANES), lambda b, h: (b, 2 * h + 1)),
            pl.BlockSpec((past, LANES), lambda b, h: (b, 0)),
            pl.BlockSpec((tq, LANES), lambda b, h: (b, 2 * h)),
            pl.BlockSpec((tq, LANES), lambda b, h: (b, 2 * h + 1)),
            pl.BlockSpec((tq, LANES), lambda b, h: (b, 0)),
        ],
        out_specs=pl.BlockSpec((tq, LANES), lambda b, h: (b, h)),
        out_shape=jax.ShapeDtypeStruct((n, MLA_WIDTH), BF16),
        compiler_params=_params("parallel", "parallel"),
        name="mla_sample",
    )(qn, qr, kv_cache, kv_cache, krd_cache, kv_new, kv_new, krd_new)


def _kv_expand_kernel(c_ref, w_ref, o_ref):
    o_ref[...] = _dot(c_ref[...].astype(BF16), w_ref[...]).astype(BF16)


def _kv_expand(ckv, w_ukv, *, tm):
    n, r = ckv.shape
    tm = min(tm, n)
    assert n % tm == 0
    return pl.pallas_call(
        _kv_expand_kernel,
        grid=(n // tm,),
        in_specs=[pl.BlockSpec((tm, r), lambda i: (i, 0)), _resident(w_ukv.shape)],
        out_specs=pl.BlockSpec((tm, w_ukv.shape[1]), lambda i: (i, 0)),
        out_shape=jax.ShapeDtypeStruct((n, w_ukv.shape[1]), BF16),
        compiler_params=_params("parallel"),
        name="kv_expand",
    )(ckv, w_ukv)


def _proj_out_kernel(x_ref, a_ref, b_ref, wa_ref, wb_ref, o_ref):
    o_ref[...] = x_ref[...] + _dot(a_ref[...], wa_ref[...]) + _dot(b_ref[...], wb_ref[...])


def _proj_out(x, a_da, a_mla, wo_a, wo_b, *, tm):
    n, d = x.shape
    tm = min(tm, n)
    assert n % tm == 0
    row = lambda i: (i, 0)
    return pl.pallas_call(
        _proj_out_kernel,
        grid=(n // tm,),
        in_specs=[pl.BlockSpec((tm, d), row), pl.BlockSpec((tm, DA_WIDTH), row),
                  pl.BlockSpec((tm, MLA_WIDTH), row), _resident(wo_a.shape), _resident(wo_b.shape)],
        out_specs=pl.BlockSpec((tm, d), row),
        out_shape=jax.ShapeDtypeStruct((n, d), F32),
        compiler_params=_params("parallel"),
        name="proj_out",
    )(x, a_da, a_mla, wo_a, wo_b)


def _swap_halves_cols(w):
    half = w.shape[-1] // 2
    return jnp.concatenate([w[..., half:], w[..., :half]], axis=-1)


def _prep_weights(w_in, mix_norm, q_norm, w_uq, kv_norm, w_ukv, w_out):
    o1, o2, o3 = DA_WIDTH, 2 * DA_WIDTH, 3 * DA_WIDTH
    o4, o5 = o3 + Q_LORA, o3 + Q_LORA + KV_LORA
    w_in = w_in.astype(BF16)
    w_kr = w_in[:, o5:]
    w_uq = w_uq.astype(BF16).reshape(Q_LORA, MLA_HEADS, QK_HEAD)
    w_uqr = w_uq[:, :, QK_NOPE:]
    return {
        "mix_norm": mix_norm[None, :],
        "w_q": w_in[:, :o1], "w_k": w_in[:, o1:o2], "w_v": w_in[:, o2:o3],
        "w_cq": w_in[:, o3:o4], "w_ckv": w_in[:, o4:o5],
        "w_kr": jnp.concatenate([w_kr, w_kr], axis=1),
        "w_krs": jnp.concatenate([_swap_halves_cols(w_kr)] * 2, axis=1),
        "q_norm": q_norm[None, :], "kv_norm": kv_norm[None, :],
        "w_uqn": w_uq[:, :, :QK_NOPE].reshape(Q_LORA, MLA_HEADS * QK_NOPE),
        "w_uqr": w_uqr.reshape(Q_LORA, MLA_HEADS * QK_ROPE),
        "w_uqrs": _swap_halves_cols(w_uqr).reshape(Q_LORA, MLA_HEADS * QK_ROPE),
        "w_ukv": w_ukv.astype(BF16),
        "wo_a": w_out[:DA_WIDTH].astype(BF16), "wo_b": w_out[DA_WIDTH:].astype(BF16),
    }


def _rope_tables(pos):
    half = QK_ROPE // 2
    inv = ROPE_THETA ** (-jnp.arange(half, dtype=F32) / half)
    ang = pos.astype(F32)[:, None] * inv[None, :]
    c = jnp.concatenate([jnp.cos(ang)] * 2, axis=1)
    s = jnp.concatenate([-jnp.sin(ang), jnp.sin(ang)], axis=1)
    return {"c2": jnp.tile(c, (1, 2)), "s2": jnp.tile(s, (1, 2)),
            "c8": jnp.tile(c, (1, MLA_HEADS)), "s8": jnp.tile(s, (1, MLA_HEADS))}


def _layer(x, pos, past, ffn1, ffn2, wmix, slopes, lams, subln, final_norm, lam_init, *, batch):
    n, d = x.shape
    t = n // batch
    x = _ffn(x, *ffn1, final_norm, final_norm=False, tm=1024, tf=256)
    (q, k_f, k_b, v_f, v_b, ckv, kr, krd, qn, qr, kv) = _proj_in(x, wmix, _rope_tables(pos), tm=256)
    if past is None:
        a_da = _da_prompt(q, k_b, v_b, slopes, lams, subln, t=t, tq=256, lam_init=lam_init)
        a_mla = _mla_prompt(qn, qr, kv, krd, t=t, tq=256)
    else:
        cache_k, cache_v, cache_ckv, cache_kr = past
        p = cache_k.size // (batch * DA_WIDTH)
        a_da = _da_sample(q, cache_k.astype(BF16).reshape(batch, p, DA_WIDTH),
                          cache_v.astype(BF16).reshape(batch, p, DA_WIDTH),
                          k_b, v_b, slopes, lams, subln, lam_init=lam_init)
        kv_cache = _kv_expand(cache_ckv.reshape(batch * p, KV_LORA), wmix["w_ukv"], tm=1024)
        cache_kr = cache_kr.reshape(batch * p, QK_ROPE)
        krd_cache = jnp.concatenate([cache_kr, cache_kr], axis=-1).astype(BF16)
        a_mla = _mla_sample(qn, qr, kv_cache, krd_cache, kv, krd, bd=batch)
    x = _proj_out(x, a_da, a_mla, wmix["wo_a"], wmix["wo_b"], tm=512)
    y = _ffn(x, *ffn2, final_norm, final_norm=True, tm=1024, tf=256)
    return y, (k_f, v_f, ckv, kr)


def kernel(x_prompt, x_sample, cache_da_k, cache_da_v, cache_mla_ckv, cache_mla_krope, ffn1_norm, ffn1_w_gate, ffn1_w_up, ffn1_w_down, mix_norm, w_in, da_lambda_q1, da_lambda_k1, da_lambda_q2, da_lambda_k2, da_subln, mla_q_norm, mla_w_uq, mla_kv_norm, mla_w_ukv, w_out, ffn2_norm, ffn2_w_gate, ffn2_w_up, ffn2_w_down, final_norm):
    depth = w_in.shape[0]
    assert depth == 1, "single-layer trunk"
    b, t, d = x_prompt.shape
    bd, td, _ = x_sample.shape
    past_len = cache_da_k.shape[2]
    l = 0
    lam_init = 0.8 - 0.6 * math.exp(-0.3 * l)

    ffn1 = (ffn1_norm[l][None, :], ffn1_w_gate[l].astype(BF16), ffn1_w_up[l].astype(BF16),
            ffn1_w_down[l].astype(BF16))
    ffn2 = (ffn2_norm[l][None, :], ffn2_w_gate[l].astype(BF16), ffn2_w_up[l].astype(BF16),
            ffn2_w_down[l].astype(BF16))
    wmix = _prep_weights(w_in[l], mix_norm[l], mla_q_norm[l], mla_w_uq[l], mla_kv_norm[l], mla_w_ukv[l],
                         w_out[l])
    slopes = jnp.asarray(np.broadcast_to(
        np.array([2.0 ** (-8.0 * (i + 1) / DA_HEADS) for i in range(DA_HEADS)], np.float32)[:, None, None],
        (DA_HEADS, 1, LANES)))
    lams = (da_lambda_q1[l][None, :], da_lambda_k1[l][None, :], da_lambda_q2[l][None, :],
            da_lambda_k2[l][None, :])
    subln = da_subln[l][None, :]
    fnorm = final_norm[None, :]

    y_p, st_p = _layer(x_prompt.reshape(b * t, d), jnp.arange(t, dtype=jnp.int32), None,
                       ffn1, ffn2, wmix, slopes, lams, subln, fnorm, lam_init, batch=b)
    past = (cache_da_k, cache_da_v, cache_mla_ckv, cache_mla_krope)
    y_s, st_s = _layer(x_sample.reshape(bd * td, d), past_len + jnp.arange(td, dtype=jnp.int32), past,
                       ffn1, ffn2, wmix, slopes, lams, subln, fnorm, lam_init, batch=bd)

    def state(st, nb, nt):
        k_f, v_f, ckv, kr = st
        return (k_f.reshape(1, nb, nt, DA_HEADS, DA_HEAD_DIM), v_f.reshape(1, nb, nt, DA_HEADS, DA_HEAD_DIM),
                ckv.reshape(1, nb, nt, KV_LORA), kr.reshape(1, nb, nt, QK_ROPE))

    return (y_p.reshape(b, t, d), y_s.reshape(bd, td, d)) + state(st_p, b, t) + state(st_s, bd, td)
```

```python
import functools
import math

import jax
import jax.numpy as jnp
import numpy as np
from jax import lax
from jax.experimental import pallas as pl
from jax.experimental.pallas import tpu as pltpu

F32 = jnp.float32
BF16 = jnp.bfloat16

CHUNK = 64
CHUNK_SHIFT = 6
assert 1 << CHUNK_SHIFT == CHUNK
EPS = 1e-6
NEG_INF = -1e30
DA_HEADS = 8
DA_HALF = 64
DA_HEAD_DIM = 2 * DA_HALF
DA_WIDTH = DA_HEADS * DA_HEAD_DIM
MLA_HEADS = 8
Q_LORA = 512
KV_LORA = 256
QK_NOPE = 128
QK_ROPE = 64
QK_HEAD = QK_NOPE + QK_ROPE
V_HEAD = 128
MLA_WIDTH = MLA_HEADS * V_HEAD
ROPE_THETA = 10000.0
LOG2E = math.log2(math.e)
LANES = 128
assert DA_HEAD_DIM == LANES and QK_NOPE == LANES and V_HEAD == LANES and 2 * QK_ROPE == LANES

VMEM_LIMIT_BYTES = 56 * 1024 * 1024


def _params(*semantics):
    return pltpu.CompilerParams(dimension_semantics=semantics, vmem_limit_bytes=VMEM_LIMIT_BYTES)


def _rms(x, g):
    return x * lax.rsqrt(jnp.mean(x * x, axis=-1, keepdims=True) + EPS) * g


def _dot(a, b):
    return jnp.dot(a, b, preferred_element_type=F32)


def _dot_nt(a, b):
    return lax.dot_general(a, b, (((1,), (1,)), ((), ())), preferred_element_type=F32)


def _resident(shape):
    return pl.BlockSpec(shape, lambda *_: (0,) * len(shape), pipeline_mode=pl.Buffered(1))


def _ffn_kernel(x_ref, g_ref, wg_ref, wu_ref, wd_ref, fn_ref, o_ref, xn_ref, *, final_norm, nj):
    j = pl.program_id(1)

    @pl.when(j == 0)
    def _():
        xn_ref[...] = _rms(x_ref[...], g_ref[...]).astype(BF16)

    def partial_down():
        xn = xn_ref[...]
        gate = _dot(xn, wg_ref[...])
        up = _dot(xn, wu_ref[...])
        h = (gate * jax.nn.sigmoid(gate) * up).astype(BF16)
        return _dot(h, wd_ref[...])

    def finish(acc):
        y = x_ref[...] + 0.5 * acc
        if final_norm:
            y = _rms(y, fn_ref[...])
        o_ref[...] = y

    if nj == 1:
        finish(partial_down())
        return

    @pl.when(j == 0)
    def _():
        o_ref[...] = partial_down()

    @pl.when(jnp.logical_and(j > 0, j < nj - 1))
    def _():
        o_ref[...] += partial_down()

    @pl.when(j == nj - 1)
    def _():
        finish(o_ref[...] + partial_down())


def _ffn(x, norm, wg, wu, wd, fnorm, *, final_norm, tm, tf):
    n, d = x.shape
    f = wg.shape[1]
    tm = min(tm, n)
    tf = min(tf, f)
    assert n % tm == 0 and f % tf == 0
    return pl.pallas_call(
        functools.partial(_ffn_kernel, final_norm=final_norm, nj=f // tf),
        grid=(n // tm, f // tf),
        in_specs=[
            pl.BlockSpec((tm, d), lambda i, j: (i, 0)),
            pl.BlockSpec((1, d), lambda i, j: (0, 0)),
            pl.BlockSpec((d, tf), lambda i, j: (0, j)),
            pl.BlockSpec((d, tf), lambda i, j: (0, j)),
            pl.BlockSpec((tf, d), lambda i, j: (j, 0)),
            pl.BlockSpec((1, d), lambda i, j: (0, 0)),
        ],
        out_specs=pl.BlockSpec((tm, d), lambda i, j: (i, 0)),
        out_shape=jax.ShapeDtypeStruct((n, d), F32),
        scratch_shapes=[pltpu.VMEM((tm, d), BF16)],
        compiler_params=_params("parallel", "arbitrary"),
        name="ffn",
    )(x, norm, wg, wu, wd, fnorm)


def _proj_in_kernel(x_ref, g_ref, wq_ref, wk_ref, wv_ref, wcq_ref, wckv_ref, wkr_ref, wkrs_ref,
                    qn_ref, kvn_ref, wuqn_ref, wuqr_ref, wuqrs_ref, wukv_ref,
                    c2_ref, s2_ref, c8_ref, s8_ref,
                    q_ref, kf_ref, kb_ref, vf_ref, vb_ref, ckv_ref, kr_ref, krd_ref,
                    qnope_ref, qrope_ref, kv_ref):
    h = _rms(x_ref[...], g_ref[...]).astype(BF16)
    q_ref[...] = (_dot(h, wq_ref[...]) * (DA_HALF ** -0.5 * LOG2E)).astype(BF16)
    k = _dot(h, wk_ref[...])
    kf_ref[...] = k
    kb_ref[...] = k.astype(BF16)
    v = _dot(h, wv_ref[...])
    vf_ref[...] = v
    vb_ref[...] = v.astype(BF16)
    cq = _rms(_dot(h, wcq_ref[...]), qn_ref[...]).astype(BF16)
    ckv = _rms(_dot(h, wckv_ref[...]), kvn_ref[...])
    ckv_ref[...] = ckv
    krd = _dot(h, wkr_ref[...]) * c2_ref[...] + _dot(h, wkrs_ref[...]) * s2_ref[...]
    kr_ref[...] = krd[:, :QK_ROPE]
    krd_ref[...] = krd.astype(BF16)
    mla_scale = QK_HEAD ** -0.5 * LOG2E
    qnope_ref[...] = (_dot(cq, wuqn_ref[...]) * mla_scale).astype(BF16)
    qr = _dot(cq, wuqr_ref[...]) * c8_ref[...] + _dot(cq, wuqrs_ref[...]) * s8_ref[...]
    qrope_ref[...] = (qr * mla_scale).astype(BF16)
    kv_ref[...] = _dot(ckv.astype(BF16), wukv_ref[...]).astype(BF16)


def _proj_in(x, w, tabs, *, tm):
    n, d = x.shape
    tm = min(tm, n)
    assert n % tm == 0
    t_rows = tabs["c2"].shape[0]
    assert t_rows % tm == 0 or tm % t_rows == 0
    if tm > t_rows:
        tm = t_rows
    nt = t_rows // tm

    def row(i):
        return (i, 0)

    def trow(i):
        return (i % nt, 0)

    weights = [w["w_q"], w["w_k"], w["w_v"], w["w_cq"], w["w_ckv"], w["w_kr"], w["w_krs"],
               w["q_norm"], w["kv_norm"], w["w_uqn"], w["w_uqr"], w["w_uqrs"], w["w_ukv"]]
    tables = [tabs["c2"], tabs["s2"], tabs["c8"], tabs["s8"]]
    outs = [
        ((n, DA_WIDTH), BF16),
        ((n, DA_WIDTH), F32),
        ((n, DA_WIDTH), BF16),
        ((n, DA_WIDTH), F32),
        ((n, DA_WIDTH), BF16),
        ((n, KV_LORA), F32),
        ((n, QK_ROPE), F32),
        ((n, LANES), BF16),
        ((n, MLA_HEADS * QK_NOPE), BF16),
        ((n, MLA_HEADS * QK_ROPE), BF16),
        ((n, MLA_HEADS * (QK_NOPE + V_HEAD)), BF16),
    ]
    return pl.pallas_call(
        _proj_in_kernel,
        grid=(n // tm,),
        in_specs=([pl.BlockSpec((tm, d), row), _resident((1, d))]
                  + [_resident(a.shape) for a in weights]
                  + [pl.BlockSpec((tm, a.shape[1]), trow) for a in tables]),
        out_specs=[pl.BlockSpec((tm, s[1]), row) for s, _ in outs],
        out_shape=[jax.ShapeDtypeStruct(s, dt) for s, dt in outs],
        compiler_params=_params("parallel"),
        name="proj_in",
    )(x, w["mix_norm"], *weights, *tables)


def _lambda(lq1_ref, lk1_ref, lq2_ref, lk2_ref, lam_init):
    a = jnp.sum(lq1_ref[...] * lk1_ref[...], axis=-1, keepdims=True)
    b = jnp.sum(lq2_ref[...] * lk2_ref[...], axis=-1, keepdims=True)
    return jnp.exp(a) - jnp.exp(b) + lam_init


def _pos_bias(qpos, kpos, slope):
    visible = lax.shift_right_arithmetic(kpos, CHUNK_SHIFT) <= lax.shift_right_arithmetic(qpos, CHUNK_SHIFT)
    if slope is None:
        return jnp.where(visible, 0.0, NEG_INF)
    dist = jnp.abs(qpos - kpos).astype(F32)
    return jnp.where(visible, (-LOG2E * slope) * dist, NEG_INF)


def _strip(tq, t, slope):
    qpos = lax.broadcasted_iota(jnp.int32, (tq, t), 0)
    kpos = lax.broadcasted_iota(jnp.int32, (tq, t), 1) - (t - tq)
    return _pos_bias(qpos, kpos, slope)


def _softmax_parts(s):
    m = jnp.max(s, axis=-1, keepdims=True)
    p = jnp.exp2(s - m)
    return p, jnp.sum(p, axis=-1, keepdims=True)


def _da_combine(p1, l1, p2, l2, lam):
    return (p1 - p2 * (lam * l1 / l2)).astype(BF16), 1.0 / l1


def _split_halves(q):
    lane = lax.broadcasted_iota(jnp.int32, q.shape, 1)
    zero = jnp.zeros_like(q)
    return jnp.where(lane < DA_HALF, q, zero), jnp.where(lane >= DA_HALF, q, zero)


def _da_finish(o, subln, lam_init):
    return (_rms(o, subln) * (1.0 - lam_init)).astype(BF16)


def _da_prompt_kernel(q_ref, k_ref, v_ref, slope_ref, lq1_ref, lk1_ref, lq2_ref, lk2_ref, subln_ref,
                      o_ref, strip_ref, *, tq, lam_init):
    t = q_ref.shape[0]
    lam = _lambda(lq1_ref, lk1_ref, lq2_ref, lk2_ref, lam_init)
    strip_ref[...] = _strip(tq, t, slope_ref[0][:, :1])
    def scores(qi):
        ext = (qi + 1) * tq
        q_lo, q_hi = _split_halves(q_ref[qi * tq:ext, :])
        k = k_ref[0:ext, :]
        bias = strip_ref[:, t - ext:t]
        return _dot_nt(q_lo, k) + bias, _dot_nt(q_hi, k) + bias

    nq = t // tq
    s_next = scores(0)
    for qi in range(nq):
        ext = (qi + 1) * tq
        s1, s2 = s_next
        if qi + 1 < nq:
            s_next = scores(qi + 1)
        p1, l1 = _softmax_parts(s1)
        p2, l2 = _softmax_parts(s2)
        a, w = _da_combine(p1, l1, p2, l2, lam)
        o = _dot(a, v_ref[0:ext, :]) * w
        o_ref[qi * tq:ext, :] = _da_finish(o, subln_ref[...], lam_init)


def _da_prompt(q, k, v, slopes, lams, subln, *, t, tq, lam_init):
    n = q.shape[0]
    tq = min(tq, t)
    blk = pl.BlockSpec((t, LANES), lambda b, h: (b, h))
    vec = lambda a: pl.BlockSpec(a.shape, lambda b, h: (0, 0))
    return pl.pallas_call(
        functools.partial(_da_prompt_kernel, tq=tq, lam_init=lam_init),
        grid=(n // t, DA_HEADS),
        in_specs=[blk, blk, blk, pl.BlockSpec((1, 1, LANES), lambda b, h: (h, 0, 0))]
                 + [vec(a) for a in lams] + [vec(subln)],
        out_specs=blk,
        out_shape=jax.ShapeDtypeStruct((n, DA_WIDTH), BF16),
        scratch_shapes=[pltpu.VMEM((tq, t), F32)],
        compiler_params=_params("parallel", "parallel"),
        name="da_prompt",
    )(q, k, v, slopes, *lams, subln)


def _select_rope(qr, h):
    lane = lax.broadcasted_iota(jnp.int32, qr.shape, 1)
    mine = lax.shift_right_logical(lane, CHUNK_SHIFT) == (h % 2)
    return jnp.where(mine, qr, jnp.zeros_like(qr))


def _mla_prompt_kernel(qn_ref, qr_ref, kn_ref, v_ref, krd_ref, o_ref, mask_ref, kfull_ref, *, tq):
    t = qn_ref.shape[0]
    h = pl.program_id(1)
    mask_ref[...] = _strip(tq, tq, None)
    kfull_ref[:, :LANES] = kn_ref[...]
    kfull_ref[:, LANES:] = krd_ref[...]
    def scores(qi):
        lo, ext = qi * tq, (qi + 1) * tq
        q = jnp.concatenate([qn_ref[lo:ext, :], _select_rope(qr_ref[lo:ext, :], h)], axis=1)
        sd = _dot_nt(q, kfull_ref[lo:ext, :]) + mask_ref[...]
        sp = _dot_nt(q, kfull_ref[0:lo, :]) if qi > 0 else None
        return sd, sp

    nq = t // tq
    s_next = scores(0)
    for qi in range(nq):
        lo, ext = qi * tq, (qi + 1) * tq
        sd, sp = s_next
        if qi + 1 < nq:
            s_next = scores(qi + 1)
        md = jnp.max(sd, axis=-1, keepdims=True)
        if qi == 0:
            pd = jnp.exp2(sd - md)
            l = jnp.sum(pd, axis=-1, keepdims=True)
            o = _dot(pd.astype(BF16), v_ref[lo:ext, :])
        else:
            m = jnp.maximum(md, jnp.max(sp, axis=-1, keepdims=True))
            pd = jnp.exp2(sd - m)
            pp = jnp.exp2(sp - m)
            l = jnp.sum(pd, axis=-1, keepdims=True) + jnp.sum(pp, axis=-1, keepdims=True)
            o = _dot(pp.astype(BF16), v_ref[0:lo, :]) + _dot(pd.astype(BF16), v_ref[lo:ext, :])
        o_ref[lo:ext, :] = (o * (1.0 / l)).astype(BF16)


def _mla_prompt(qn, qr, kv, krd, *, t, tq):
    n = qn.shape[0]
    tq = min(tq, t)
    return pl.pallas_call(
        functools.partial(_mla_prompt_kernel, tq=tq),
        grid=(n // t, MLA_HEADS),
        in_specs=[
            pl.BlockSpec((t, LANES), lambda b, h: (b, h)),
            pl.BlockSpec((t, LANES), lambda b, h: (b, h // 2)),
            pl.BlockSpec((t, LANES), lambda b, h: (b, 2 * h)),
            pl.BlockSpec((t, LANES), lambda b, h: (b, 2 * h + 1)),
            pl.BlockSpec((t, LANES), lambda b, h: (b, 0)),
        ],
        out_specs=pl.BlockSpec((t, LANES), lambda b, h: (b, h)),
        out_shape=jax.ShapeDtypeStruct((n, MLA_WIDTH), BF16),
        scratch_shapes=[pltpu.VMEM((tq, tq), F32), pltpu.VMEM((t, 2 * LANES), BF16)],
        compiler_params=_params("parallel", "parallel"),
        name="mla_prompt",
    )(qn, qr, kv, kv, krd)


def _positions(tq, tk, q0, k0):
    qpos = lax.broadcasted_iota(jnp.int32, (tq, tk), 0) + q0
    kpos = lax.broadcasted_iota(jnp.int32, (tq, tk), 1) + k0
    return qpos, kpos


def _da_sample_kernel(q_ref, ck_ref, cv_ref, nk_ref, nv_ref, slope_ref, lq1_ref, lk1_ref, lq2_ref, lk2_ref,
                      subln_ref, o_ref, acc_ref, m_ref, l_ref, *, lam_init):
    c = pl.program_id(1)
    nc = pl.num_programs(1)
    tq = q_ref.shape[0]
    pc = ck_ref.shape[0] // DA_HEADS
    past = pc * nc

    @pl.when(c == 0)
    def _():
        m_ref[...] = jnp.full_like(m_ref, NEG_INF)
        l_ref[...] = jnp.zeros_like(l_ref)
        acc_ref[...] = jnp.zeros_like(acc_ref)

    def attend(keys, values, unit_bias):
        unit2 = jnp.concatenate([unit_bias, unit_bias], axis=0)
        scores = []
        for h in range(DA_HEADS):
            q2 = jnp.concatenate(_split_halves(q_ref[:, h * LANES:(h + 1) * LANES]), axis=0)
            scores.append(_dot_nt(q2, keys(h)) + unit2 * slope_ref[h][:, :1])
        probs = []
        for h, s in enumerate(scores):
            rows = slice(2 * h * tq, (2 * h + 2) * tq)
            m_old = m_ref[rows, :]
            m_new = jnp.maximum(m_old, jnp.max(s, axis=-1, keepdims=True))
            alpha = jnp.exp2(m_old - m_new)
            p = jnp.exp2(s - m_new)
            m_ref[rows, :] = m_new
            l_ref[rows, :] = alpha * l_ref[rows, :] + jnp.sum(p, axis=-1, keepdims=True)
            probs.append((alpha, p.astype(BF16)))
        for h, (alpha, p) in enumerate(probs):
            rows = slice(2 * h * tq, (2 * h + 2) * tq)
            acc_ref[rows, :] = alpha * acc_ref[rows, :] + _dot(p, values(h))

    def head_rows(h):
        return pl.ds(h, pc, stride=DA_HEADS)

    attend(lambda h: ck_ref[head_rows(h), :].astype(BF16), lambda h: cv_ref[head_rows(h), :].astype(BF16),
           _pos_bias(*_positions(tq, pc, past, c * pc), 1.0))

    @pl.when(c == nc - 1)
    def _():
        lam = _lambda(lq1_ref, lk1_ref, lq2_ref, lk2_ref, lam_init)
        attend(lambda h: nk_ref[:, h * LANES:(h + 1) * LANES], lambda h: nv_ref[:, h * LANES:(h + 1) * LANES],
               _pos_bias(*_positions(tq, tq, past, past), 1.0))
        for h in range(DA_HEADS):
            r1 = slice(2 * h * tq, (2 * h + 1) * tq)
            r2 = slice((2 * h + 1) * tq, (2 * h + 2) * tq)
            o = acc_ref[r1, :] / l_ref[r1, :] - acc_ref[r2, :] * (lam / l_ref[r2, :])
            o_ref[:, h * LANES:(h + 1) * LANES] = _da_finish(o, subln_ref[...], lam_init)


def _da_sample(q, cache_k, cache_v, nk, nv, slopes, lams, subln, *, lam_init, pc):
    bd, rows, _ = cache_k.shape
    past = rows // DA_HEADS
    pc = min(pc, past)
    assert past % pc == 0
    n = q.shape[0]
    tq = n // bd
    new = pl.BlockSpec((tq, DA_WIDTH), lambda b, c: (b, 0))
    cache = pl.BlockSpec((None, pc * DA_HEADS, LANES), lambda b, c: (b, c, 0))
    whole = lambda a: pl.BlockSpec(a.shape, lambda b, c: (0,) * a.ndim)
    return pl.pallas_call(
        functools.partial(_da_sample_kernel, lam_init=lam_init),
        grid=(bd, past // pc),
        in_specs=[new, cache, cache, new, new, whole(slopes)] + [whole(a) for a in lams] + [whole(subln)],
        out_specs=new,
        out_shape=jax.ShapeDtypeStruct((n, DA_WIDTH), BF16),
        scratch_shapes=[pltpu.VMEM((2 * DA_HEADS * tq, LANES), F32),
                        pltpu.VMEM((2 * DA_HEADS * tq, 1), F32),
                        pltpu.VMEM((2 * DA_HEADS * tq, 1), F32)],
        compiler_params=_params("parallel", "arbitrary"),
        name="da_sample",
    )(q, cache_k, cache_v, nk, nv, slopes, *lams, subln)


MLA_GROUP = 4


def _mla_sample_kernel(qn_ref, qr_ref, cc_ref, ckrd_ref, nc_ref, nkrd_ref, wukv_ref, o_ref):
    tq = qn_ref.shape[0]
    past = cc_ref.shape[0]
    cc = cc_ref[...].astype(BF16)
    nc = nc_ref[...].astype(BF16)
    rows = MLA_GROUP * tq
    qrow = lax.broadcasted_iota(jnp.int32, (rows, 1), 0) & (tq - 1)
    bias_c = _pos_bias(qrow + past, lax.broadcasted_iota(jnp.int32, (rows, past), 1), None)
    bias_n = _pos_bias(qrow + past, lax.broadcasted_iota(jnp.int32, (rows, tq), 1) + past, None)
    kv_cols = QK_NOPE + V_HEAD
    for g in range(MLA_HEADS // MLA_GROUP):
        heads = range(g * MLA_GROUP, (g + 1) * MLA_GROUP)
        qa = jnp.concatenate(
            [_dot_nt(qn_ref[:, h * QK_NOPE:(h + 1) * QK_NOPE],
                     wukv_ref[:, h * kv_cols:h * kv_cols + QK_NOPE]).astype(BF16) for h in heads], axis=0)
        qr = jnp.concatenate(
            [_select_rope(qr_ref[:, (h // 2) * LANES:(h // 2 + 1) * LANES], h) for h in heads], axis=0)
        sc = _dot_nt(qa, cc) + _dot_nt(qr, ckrd_ref[...]) + bias_c
        sn = _dot_nt(qa, nc) + _dot_nt(qr, nkrd_ref[...]) + bias_n
        m = jnp.maximum(jnp.max(sc, axis=-1, keepdims=True), jnp.max(sn, axis=-1, keepdims=True))
        pc = jnp.exp2(sc - m)
        pn = jnp.exp2(sn - m)
        w = 1.0 / (jnp.sum(pc, axis=-1, keepdims=True) + jnp.sum(pn, axis=-1, keepdims=True))
        lat = ((_dot(pc.astype(BF16), cc) + _dot(pn.astype(BF16), nc)) * w).astype(BF16)
        for j, h in enumerate(heads):
            w_uv = wukv_ref[:, h * kv_cols + QK_NOPE:(h + 1) * kv_cols]
            o_ref[:, h * V_HEAD:(h + 1) * V_HEAD] = _dot(lat[j * tq:(j + 1) * tq, :], w_uv).astype(BF16)


def _mla_sample(qn, qr, cache_ckv, krd_cache, ckv_new, krd_new, w_ukv):
    bd, past, _ = cache_ckv.shape
    n = qn.shape[0]
    tq = n // bd
    assert tq & (tq - 1) == 0
    row = lambda b: (b, 0)
    return pl.pallas_call(
        _mla_sample_kernel,
        grid=(bd,),
        in_specs=[
            pl.BlockSpec((tq, MLA_HEADS * QK_NOPE), row),
            pl.BlockSpec((tq, MLA_HEADS * QK_ROPE), row),
            pl.BlockSpec((None, past, KV_LORA), lambda b: (b, 0, 0)),
            pl.BlockSpec((None, past, LANES), lambda b: (b, 0, 0)),
            pl.BlockSpec((tq, KV_LORA), row),
            pl.BlockSpec((tq, LANES), row),
            _resident(w_ukv.shape),
        ],
        out_specs=pl.BlockSpec((tq, MLA_WIDTH), row),
        out_shape=jax.ShapeDtypeStruct((n, MLA_WIDTH), BF16),
        compiler_params=_params("parallel"),
        name="mla_sample",
    )(qn, qr, cache_ckv, krd_cache, ckv_new, krd_new, w_ukv)


def _proj_out_kernel(x_ref, a_ref, b_ref, wa_ref, wb_ref, o_ref):
    o_ref[...] = x_ref[...] + _dot(a_ref[...], wa_ref[...]) + _dot(b_ref[...], wb_ref[...])


def _proj_out(x, a_da, a_mla, wo_a, wo_b, *, tm):
    n, d = x.shape
    tm = min(tm, n)
    assert n % tm == 0
    row = lambda i: (i, 0)
    return pl.pallas_call(
        _proj_out_kernel,
        grid=(n // tm,),
        in_specs=[pl.BlockSpec((tm, d), row), pl.BlockSpec((tm, DA_WIDTH), row),
                  pl.BlockSpec((tm, MLA_WIDTH), row), _resident(wo_a.shape), _resident(wo_b.shape)],
        out_specs=pl.BlockSpec((tm, d), row),
        out_shape=jax.ShapeDtypeStruct((n, d), F32),
        compiler_params=_params("parallel"),
        name="proj_out",
    )(x, a_da, a_mla, wo_a, wo_b)


def _swap_halves_cols(w):
    half = w.shape[-1] // 2
    return jnp.concatenate([w[..., half:], w[..., :half]], axis=-1)


def _prep_weights(w_in, mix_norm, q_norm, w_uq, kv_norm, w_ukv, w_out):
    o1, o2, o3 = DA_WIDTH, 2 * DA_WIDTH, 3 * DA_WIDTH
    o4, o5 = o3 + Q_LORA, o3 + Q_LORA + KV_LORA
    w_in = w_in.astype(BF16)
    w_kr = w_in[:, o5:]
    w_uq = w_uq.astype(BF16).reshape(Q_LORA, MLA_HEADS, QK_HEAD)
    w_uqr = w_uq[:, :, QK_NOPE:]
    return {
        "mix_norm": mix_norm[None, :],
        "w_q": w_in[:, :o1], "w_k": w_in[:, o1:o2], "w_v": w_in[:, o2:o3],
        "w_cq": w_in[:, o3:o4], "w_ckv": w_in[:, o4:o5],
        "w_kr": jnp.concatenate([w_kr, w_kr], axis=1),
        "w_krs": jnp.concatenate([_swap_halves_cols(w_kr)] * 2, axis=1),
        "q_norm": q_norm[None, :], "kv_norm": kv_norm[None, :],
        "w_uqn": w_uq[:, :, :QK_NOPE].reshape(Q_LORA, MLA_HEADS * QK_NOPE),
        "w_uqr": w_uqr.reshape(Q_LORA, MLA_HEADS * QK_ROPE),
        "w_uqrs": _swap_halves_cols(w_uqr).reshape(Q_LORA, MLA_HEADS * QK_ROPE),
        "w_ukv": w_ukv.astype(BF16),
        "wo_a": w_out[:DA_WIDTH].astype(BF16), "wo_b": w_out[DA_WIDTH:].astype(BF16),
    }


def _rope_tables(pos):
    half = QK_ROPE // 2
    inv = ROPE_THETA ** (-jnp.arange(half, dtype=F32) / half)
    ang = pos.astype(F32)[:, None] * inv[None, :]
    c = jnp.concatenate([jnp.cos(ang)] * 2, axis=1)
    s = jnp.concatenate([-jnp.sin(ang), jnp.sin(ang)], axis=1)
    return {"c2": jnp.tile(c, (1, 2)), "s2": jnp.tile(s, (1, 2)),
            "c8": jnp.tile(c, (1, MLA_HEADS)), "s8": jnp.tile(s, (1, MLA_HEADS))}


def _layer(x, pos, past, ffn1, ffn2, wmix, slopes, lams, subln, final_norm, lam_init, *, batch):
    n, d = x.shape
    t = n // batch
    x = _ffn(x, *ffn1, final_norm, final_norm=False, tm=512, tf=512)
    (q, k_f, k_b, v_f, v_b, ckv, kr, krd, qn, qr, kv) = _proj_in(x, wmix, _rope_tables(pos), tm=256)
    if past is None:
        a_da = _da_prompt(q, k_b, v_b, slopes, lams, subln, t=t, tq=256, lam_init=lam_init)
        a_mla = _mla_prompt(qn, qr, kv, krd, t=t, tq=256)
    else:
        cache_k, cache_v, cache_ckv, cache_kr = past
        p = cache_k.size // (batch * DA_WIDTH)
        a_da = _da_sample(q, cache_k.reshape(batch, p * DA_HEADS, DA_HEAD_DIM),
                          cache_v.reshape(batch, p * DA_HEADS, DA_HEAD_DIM),
                          k_b, v_b, slopes, lams, subln, lam_init=lam_init, pc=1024)
        cache_kr = cache_kr.reshape(batch, p, QK_ROPE)
        krd_cache = jnp.concatenate([cache_kr, cache_kr], axis=-1).astype(BF16)
        a_mla = _mla_sample(qn, qr, cache_ckv.reshape(batch, p, KV_LORA), krd_cache, ckv, krd, wmix["w_ukv"])
    x = _proj_out(x, a_da, a_mla, wmix["wo_a"], wmix["wo_b"], tm=512)
    y = _ffn(x, *ffn2, final_norm, final_norm=True, tm=512, tf=512)
    return y, (k_f, v_f, ckv, kr)


def kernel(x_prompt, x_sample, cache_da_k, cache_da_v, cache_mla_ckv, cache_mla_krope, ffn1_norm, ffn1_w_gate, ffn1_w_up, ffn1_w_down, mix_norm, w_in, da_lambda_q1, da_lambda_k1, da_lambda_q2, da_lambda_k2, da_subln, mla_q_norm, mla_w_uq, mla_kv_norm, mla_w_ukv, w_out, ffn2_norm, ffn2_w_gate, ffn2_w_up, ffn2_w_down, final_norm):
    depth = w_in.shape[0]
    assert depth == 1, "single-layer trunk"
    b, t, d = x_prompt.shape
    bd, td, _ = x_sample.shape
    past_len = cache_da_k.shape[2]
    l = 0
    lam_init = 0.8 - 0.6 * math.exp(-0.3 * l)

    ffn1 = (ffn1_norm[l][None, :], ffn1_w_gate[l].astype(BF16), ffn1_w_up[l].astype(BF16),
            ffn1_w_down[l].astype(BF16))
    ffn2 = (ffn2_norm[l][None, :], ffn2_w_gate[l].astype(BF16), ffn2_w_up[l].astype(BF16),
            ffn2_w_down[l].astype(BF16))
    wmix = _prep_weights(w_in[l], mix_norm[l], mla_q_norm[l], mla_w_uq[l], mla_kv_norm[l], mla_w_ukv[l],
                         w_out[l])
    slopes = jnp.asarray(np.broadcast_to(
        np.array([2.0 ** (-8.0 * (i + 1) / DA_HEADS) for i in range(DA_HEADS)], np.float32)[:, None, None],
        (DA_HEADS, 1, LANES)))
    lams = (da_lambda_q1[l][None, :], da_lambda_k1[l][None, :], da_lambda_q2[l][None, :],
            da_lambda_k2[l][None, :])
    subln = da_subln[l][None, :]
    fnorm = final_norm[None, :]

    y_p, st_p = _layer(x_prompt.reshape(b * t, d), jnp.arange(t, dtype=jnp.int32), None,
                       ffn1, ffn2, wmix, slopes, lams, subln, fnorm, lam_init, batch=b)
    past = (cache_da_k, cache_da_v, cache_mla_ckv, cache_mla_krope)
    y_s, st_s = _layer(x_sample.reshape(bd * td, d), past_len + jnp.arange(td, dtype=jnp.int32), past,
                       ffn1, ffn2, wmix, slopes, lams, subln, fnorm, lam_init, batch=bd)

    def state(st, nb, nt):
        k_f, v_f, ckv, kr = st
        return (k_f.reshape(1, nb, nt, DA_HEADS, DA_HEAD_DIM), v_f.reshape(1, nb, nt, DA_HEADS, DA_HEAD_DIM),
                ckv.reshape(1, nb, nt, KV_LORA), kr.reshape(1, nb, nt, QK_ROPE))

    return (y_p.reshape(b, t, d), y_s.reshape(bd, td, d)) + state(st_p, b, t) + state(st_s, bd, td)
```

```python
import functools
import math

import jax
import jax.numpy as jnp
import numpy as np
from jax import lax
from jax.experimental import pallas as pl
from jax.experimental.pallas import tpu as pltpu

F32 = jnp.float32
BF16 = jnp.bfloat16

CHUNK = 64
CHUNK_SHIFT = 6
assert 1 << CHUNK_SHIFT == CHUNK
EPS = 1e-6
NEG_INF = -1e30
DA_HEADS = 8
DA_HALF = 64
DA_HEAD_DIM = 2 * DA_HALF
DA_WIDTH = DA_HEADS * DA_HEAD_DIM
MLA_HEADS = 8
Q_LORA = 512
KV_LORA = 256
QK_NOPE = 128
QK_ROPE = 64
QK_HEAD = QK_NOPE + QK_ROPE
V_HEAD = 128
MLA_WIDTH = MLA_HEADS * V_HEAD
ROPE_THETA = 10000.0
LOG2E = math.log2(math.e)
LANES = 128
assert DA_HEAD_DIM == LANES and QK_NOPE == LANES and V_HEAD == LANES and 2 * QK_ROPE == LANES

VMEM_LIMIT_BYTES = 56 * 1024 * 1024


def _params(*semantics, vmem_limit_bytes=VMEM_LIMIT_BYTES):
    return pltpu.CompilerParams(dimension_semantics=semantics, vmem_limit_bytes=vmem_limit_bytes)


def _rms(x, g):
    return x * lax.rsqrt(jnp.mean(x * x, axis=-1, keepdims=True) + EPS) * g


def _dot(a, b):
    return jnp.dot(a, b, preferred_element_type=F32)


def _dot_nt(a, b):
    return lax.dot_general(a, b, (((1,), (1,)), ((), ())), preferred_element_type=F32)


def _resident(shape):
    return pl.BlockSpec(shape, lambda *_: (0,) * len(shape), pipeline_mode=pl.Buffered(1))


def _cast_kernel(w_ref, o_ref):
    o_ref[...] = w_ref[...].astype(BF16)


def _cast_bf16(w, *, block_bytes=4 * 1024 * 1024):
    r, c = w.shape
    tr = 1 << int(math.log2(max(16, min(r, block_bytes // (4 * c)))))
    while r % tr:
        tr //= 2
    assert tr % 16 == 0 or tr == r
    return pl.pallas_call(
        _cast_kernel,
        grid=(r // tr,),
        in_specs=[pl.BlockSpec((tr, c), lambda i: (i, 0))],
        out_specs=pl.BlockSpec((tr, c), lambda i: (i, 0)),
        out_shape=jax.ShapeDtypeStruct((r, c), BF16),
        compiler_params=_params("parallel"),
        name="cast_bf16",
    )(w)


def _ffn_kernel(x_ref, g_ref, wg_ref, wu_ref, wd_ref, fn_ref, o_ref, xn_ref, *, final_norm, nj):
    j = pl.program_id(1)

    @pl.when(j == 0)
    def _():
        xn_ref[...] = _rms(x_ref[...], g_ref[...]).astype(BF16)

    def half_down():
        xn = xn_ref[...]
        gate = _dot(xn, wg_ref[...])
        up = _dot(xn, wu_ref[...])
        h = (gate * jax.nn.sigmoid(gate) * (0.5 * up)).astype(BF16)
        return _dot(h, wd_ref[...])

    def finish(y):
        o_ref[...] = _rms(y, fn_ref[...]) if final_norm else y

    if nj == 1:
        finish(x_ref[...] + half_down())
        return

    @pl.when(j == 0)
    def _():
        o_ref[...] = x_ref[...] + half_down()

    @pl.when(jnp.logical_and(j > 0, j < nj - 1))
    def _():
        o_ref[...] += half_down()

    @pl.when(j == nj - 1)
    def _():
        finish(o_ref[...] + half_down())


def _ffn(x, norm, wg, wu, wd, fnorm, *, final_norm, tm, tf):
    n, d = x.shape
    f = wg.shape[1]
    tm = min(tm, n)
    tf = min(tf, f)
    assert n % tm == 0 and f % tf == 0
    return pl.pallas_call(
        functools.partial(_ffn_kernel, final_norm=final_norm, nj=f // tf),
        grid=(n // tm, f // tf),
        in_specs=[
            pl.BlockSpec((tm, d), lambda i, j: (i, 0)),
            pl.BlockSpec((1, d), lambda i, j: (0, 0)),
            pl.BlockSpec((d, tf), lambda i, j: (0, j)),
            pl.BlockSpec((d, tf), lambda i, j: (0, j)),
            pl.BlockSpec((tf, d), lambda i, j: (j, 0)),
            pl.BlockSpec((1, d), lambda i, j: (0, 0)),
        ],
        out_specs=pl.BlockSpec((tm, d), lambda i, j: (i, 0)),
        out_shape=jax.ShapeDtypeStruct((n, d), F32),
        scratch_shapes=[pltpu.VMEM((tm, d), BF16)],
        compiler_params=_params("parallel", "arbitrary", vmem_limit_bytes=62 * 1024 * 1024),
        name="ffn",
    )(x, norm, wg, wu, wd, fnorm)


def _proj_in_kernel(x_ref, g_ref, win_ref, wkr_ref, wkrs_ref,
                    qn_ref, kvn_ref, wuqn_ref, wuqr_ref, wuqrs_ref, wukv_ref,
                    c2_ref, s2_ref, c8_ref, s8_ref,
                    q_ref, kf_ref, kb_ref, vf_ref, vb_ref, ckv_ref, kr_ref, krd_ref,
                    qnope_ref, qrope_ref, kv_ref):
    h = _rms(x_ref[...], g_ref[...]).astype(BF16)
    o1, o2, o3 = DA_WIDTH, 2 * DA_WIDTH, 3 * DA_WIDTH
    o4, o5 = o3 + Q_LORA, o3 + Q_LORA + KV_LORA
    q_ref[...] = (_dot(h, win_ref[:, :o1]) * (DA_HALF ** -0.5 * LOG2E)).astype(BF16)
    k = _dot(h, win_ref[:, o1:o2])
    kf_ref[...] = k
    kb_ref[...] = k.astype(BF16)
    v = _dot(h, win_ref[:, o2:o3])
    vf_ref[...] = v
    vb_ref[...] = v.astype(BF16)
    cq = _rms(_dot(h, win_ref[:, o3:o4]), qn_ref[...]).astype(BF16)
    ckv = _rms(_dot(h, win_ref[:, o4:o5]), kvn_ref[...])
    ckv_ref[...] = ckv
    krd = _dot(h, wkr_ref[...]) * c2_ref[...] + _dot(h, wkrs_ref[...]) * s2_ref[...]
    kr_ref[...] = krd[:, :QK_ROPE]
    krd_ref[...] = krd.astype(BF16)
    mla_scale = QK_HEAD ** -0.5 * LOG2E
    qnope_ref[...] = (_dot(cq, wuqn_ref[...]) * mla_scale).astype(BF16)
    qr = _dot(cq, wuqr_ref[...]) * c8_ref[...] + _dot(cq, wuqrs_ref[...]) * s8_ref[...]
    qrope_ref[...] = (qr * mla_scale).astype(BF16)
    kv_ref[...] = _dot(ckv.astype(BF16), wukv_ref[...]).astype(BF16)


def _proj_in(x, w, tabs, *, tm):
    n, d = x.shape
    tm = min(tm, n)
    assert n % tm == 0
    t_rows = tabs["c2"].shape[0]
    assert t_rows % tm == 0 or tm % t_rows == 0
    if tm > t_rows:
        tm = t_rows
    nt = t_rows // tm

    def row(i):
        return (i, 0)

    def trow(i):
        return (i % nt, 0)

    weights = [w["w_in"], w["w_kr"], w["w_krs"],
               w["q_norm"], w["kv_norm"], w["w_uqn"], w["w_uqr"], w["w_uqrs"], w["w_ukv"]]
    tables = [tabs["c2"], tabs["s2"], tabs["c8"], tabs["s8"]]
    outs = [
        ((n, DA_WIDTH), BF16),
        ((n, DA_WIDTH), F32),
        ((n, DA_WIDTH), BF16),
        ((n, DA_WIDTH), F32),
        ((n, DA_WIDTH), BF16),
        ((n, KV_LORA), F32),
        ((n, QK_ROPE), F32),
        ((n, LANES), BF16),
        ((n, MLA_HEADS * QK_NOPE), BF16),
        ((n, MLA_HEADS * QK_ROPE), BF16),
        ((n, MLA_HEADS * (QK_NOPE + V_HEAD)), BF16),
    ]
    return pl.pallas_call(
        _proj_in_kernel,
        grid=(n // tm,),
        in_specs=([pl.BlockSpec((tm, d), row), _resident((1, d))]
                  + [_resident(a.shape) for a in weights]
                  + [pl.BlockSpec((tm, a.shape[1]), trow) for a in tables]),
        out_specs=[pl.BlockSpec((tm, s[1]), row) for s, _ in outs],
        out_shape=[jax.ShapeDtypeStruct(s, dt) for s, dt in outs],
        compiler_params=_params("parallel"),
        name="proj_in",
    )(x, w["mix_norm"], *weights, *tables)


def _lambda(lq1_ref, lk1_ref, lq2_ref, lk2_ref, lam_init):
    a = jnp.sum(lq1_ref[...] * lk1_ref[...], axis=-1, keepdims=True)
    b = jnp.sum(lq2_ref[...] * lk2_ref[...], axis=-1, keepdims=True)
    return jnp.exp(a) - jnp.exp(b) + lam_init


def _pos_bias(qpos, kpos, slope):
    visible = lax.shift_right_arithmetic(kpos, CHUNK_SHIFT) <= lax.shift_right_arithmetic(qpos, CHUNK_SHIFT)
    if slope is None:
        return jnp.where(visible, 0.0, NEG_INF)
    dist = jnp.abs(qpos - kpos).astype(F32)
    return jnp.where(visible, (-LOG2E * slope) * dist, NEG_INF)


def _strip(tq, t, slope):
    qpos = lax.broadcasted_iota(jnp.int32, (tq, t), 0)
    kpos = lax.broadcasted_iota(jnp.int32, (tq, t), 1) - (t - tq)
    return _pos_bias(qpos, kpos, slope)


def _softmax_parts(s):
    m = jnp.max(s, axis=-1, keepdims=True)
    p = jnp.exp2(s - m)
    return p, jnp.sum(p, axis=-1, keepdims=True)


def _da_combine(p1, l1, p2, l2, lam):
    return (p1 - p2 * (lam * l1 / l2)).astype(BF16), 1.0 / l1


def _split_halves(q):
    lane = lax.broadcasted_iota(jnp.int32, q.shape, 1)
    zero = jnp.zeros_like(q)
    return jnp.where(lane < DA_HALF, q, zero), jnp.where(lane >= DA_HALF, q, zero)


def _da_finish(o, subln, lam_init):
    return (_rms(o, subln) * (1.0 - lam_init)).astype(BF16)


def _da_prompt_kernel(q_ref, k_ref, v_ref, slope_ref, lq1_ref, lk1_ref, lq2_ref, lk2_ref, subln_ref,
                      o_ref, strip_ref, *, tq, lam_init):
    t = q_ref.shape[0]
    lam = _lambda(lq1_ref, lk1_ref, lq2_ref, lk2_ref, lam_init)
    strip_ref[...] = _strip(tq, t, slope_ref[0][:, :1])

    def scores(qi):
        ext = (qi + 1) * tq
        q_lo, q_hi = _split_halves(q_ref[qi * tq:ext, :])
        k = k_ref[0:ext, :]
        bias = strip_ref[:, t - ext:t]
        return _dot_nt(q_lo, k) + bias, _dot_nt(q_hi, k) + bias

    nq = t // tq
    s_next = scores(0)
    for qi in range(nq):
        ext = (qi + 1) * tq
        s1, s2 = s_next
        if qi + 1 < nq:
            s_next = scores(qi + 1)
        p1, l1 = _softmax_parts(s1)
        p2, l2 = _softmax_parts(s2)
        a, w = _da_combine(p1, l1, p2, l2, lam)
        o = _dot(a, v_ref[0:ext, :]) * w
        o_ref[qi * tq:ext, :] = _da_finish(o, subln_ref[...], lam_init)


def _da_prompt(q, k, v, slopes, lams, subln, *, t, tq, lam_init):
    n = q.shape[0]
    tq = min(tq, t)
    blk = pl.BlockSpec((t, LANES), lambda b, h: (b, h))
    vec = lambda a: pl.BlockSpec(a.shape, lambda b, h: (0, 0))
    return pl.pallas_call(
        functools.partial(_da_prompt_kernel, tq=tq, lam_init=lam_init),
        grid=(n // t, DA_HEADS),
        in_specs=[blk, blk, blk, pl.BlockSpec((1, 1, LANES), lambda b, h: (h, 0, 0))]
                 + [vec(a) for a in lams] + [vec(subln)],
        out_specs=blk,
        out_shape=jax.ShapeDtypeStruct((n, DA_WIDTH), BF16),
        scratch_shapes=[pltpu.VMEM((tq, t), F32)],
        compiler_params=_params("parallel", "parallel"),
        name="da_prompt",
    )(q, k, v, slopes, *lams, subln)


def _select_rope(qr, h):
    lane = lax.broadcasted_iota(jnp.int32, qr.shape, 1)
    mine = lax.shift_right_logical(lane, CHUNK_SHIFT) == (h % 2)
    return jnp.where(mine, qr, jnp.zeros_like(qr))


def _mla_prompt_kernel(qn_ref, qr_ref, kn_ref, v_ref, krd_ref, o_ref, mask_ref, kfull_ref, *, tq):
    t = qn_ref.shape[0]
    h = pl.program_id(1)
    mask_ref[...] = _strip(tq, tq, None)
    kfull_ref[:, :LANES] = kn_ref[...]
    kfull_ref[:, LANES:] = krd_ref[...]
    def scores(qi):
        lo, ext = qi * tq, (qi + 1) * tq
        q = jnp.concatenate([qn_ref[lo:ext, :], _select_rope(qr_ref[lo:ext, :], h)], axis=1)
        sd = _dot_nt(q, kfull_ref[lo:ext, :]) + mask_ref[...]
        sp = _dot_nt(q, kfull_ref[0:lo, :]) if qi > 0 else None
        return sd, sp

    nq = t // tq
    s_next = scores(0)
    for qi in range(nq):
        lo, ext = qi * tq, (qi + 1) * tq
        sd, sp = s_next
        if qi + 1 < nq:
            s_next = scores(qi + 1)
        md = jnp.max(sd, axis=-1, keepdims=True)
        if qi == 0:
            pd = jnp.exp2(sd - md)
            l = jnp.sum(pd, axis=-1, keepdims=True)
            o = _dot(pd.astype(BF16), v_ref[lo:ext, :])
        else:
            m = jnp.maximum(md, jnp.max(sp, axis=-1, keepdims=True))
            pd = jnp.exp2(sd - m)
            pp = jnp.exp2(sp - m)
            l = jnp.sum(pd, axis=-1, keepdims=True) + jnp.sum(pp, axis=-1, keepdims=True)
            o = _dot(pp.astype(BF16), v_ref[0:lo, :]) + _dot(pd.astype(BF16), v_ref[lo:ext, :])
        o_ref[lo:ext, :] = (o * (1.0 / l)).astype(BF16)


def _mla_prompt(qn, qr, kv, krd, *, t, tq):
    n = qn.shape[0]
    tq = min(tq, t)
    return pl.pallas_call(
        functools.partial(_mla_prompt_kernel, tq=tq),
        grid=(n // t, MLA_HEADS),
        in_specs=[
            pl.BlockSpec((t, LANES), lambda b, h: (b, h)),
            pl.BlockSpec((t, LANES), lambda b, h: (b, h // 2)),
            pl.BlockSpec((t, LANES), lambda b, h: (b, 2 * h)),
            pl.BlockSpec((t, LANES), lambda b, h: (b, 2 * h + 1)),
            pl.BlockSpec((t, LANES), lambda b, h: (b, 0)),
        ],
        out_specs=pl.BlockSpec((t, LANES), lambda b, h: (b, h)),
        out_shape=jax.ShapeDtypeStruct((n, MLA_WIDTH), BF16),
        scratch_shapes=[pltpu.VMEM((tq, tq), F32), pltpu.VMEM((t, 2 * LANES), BF16)],
        compiler_params=_params("parallel", "parallel"),
        name="mla_prompt",
    )(qn, qr, kv, kv, krd)


def _positions(tq, tk, q0, k0):
    qpos = lax.broadcasted_iota(jnp.int32, (tq, tk), 0) + q0
    kpos = lax.broadcasted_iota(jnp.int32, (tq, tk), 1) + k0
    return qpos, kpos


def _da_sample_kernel(q_ref, ck_ref, cv_ref, nk_ref, nv_ref, slope_ref, lq1_ref, lk1_ref, lq2_ref, lk2_ref,
                      subln_ref, o_ref, acc_ref, m_ref, l_ref, *, lam_init):
    c = pl.program_id(1)
    nc = pl.num_programs(1)
    tq = q_ref.shape[0]
    pc = ck_ref.shape[0] // DA_HEADS
    past = pc * nc

    @pl.when(c == 0)
    def _():
        m_ref[...] = jnp.full_like(m_ref, NEG_INF)
        l_ref[...] = jnp.zeros_like(l_ref)
        acc_ref[...] = jnp.zeros_like(acc_ref)

    def attend(keys, values, unit_bias):
        unit2 = jnp.concatenate([unit_bias, unit_bias], axis=0)
        scores = []
        for h in range(DA_HEADS):
            q2 = jnp.concatenate(_split_halves(q_ref[:, h * LANES:(h + 1) * LANES]), axis=0)
            scores.append(_dot_nt(q2, keys(h)) + unit2 * slope_ref[h][:, :1])
        probs = []
        for h, s in enumerate(scores):
            rows = slice(2 * h * tq, (2 * h + 2) * tq)
            m_old = m_ref[rows, :]
            m_new = jnp.maximum(m_old, jnp.max(s, axis=-1, keepdims=True))
            alpha = jnp.exp2(m_old - m_new)
            p = jnp.exp2(s - m_new)
            m_ref[rows, :] = m_new
            l_ref[rows, :] = alpha * l_ref[rows, :] + jnp.sum(p, axis=-1, keepdims=True)
            probs.append((alpha, p.astype(BF16)))
        for h, (alpha, p) in enumerate(probs):
            rows = slice(2 * h * tq, (2 * h + 2) * tq)
            acc_ref[rows, :] = alpha * acc_ref[rows, :] + _dot(p, values(h))

    def head_rows(h):
        return pl.ds(h, pc, stride=DA_HEADS)

    attend(lambda h: ck_ref[head_rows(h), :].astype(BF16), lambda h: cv_ref[head_rows(h), :].astype(BF16),
           _pos_bias(*_positions(tq, pc, past, c * pc), 1.0))

    @pl.when(c == nc - 1)
    def _():
        lam = _lambda(lq1_ref, lk1_ref, lq2_ref, lk2_ref, lam_init)
        attend(lambda h: nk_ref[:, h * LANES:(h + 1) * LANES], lambda h: nv_ref[:, h * LANES:(h + 1) * LANES],
               _pos_bias(*_positions(tq, tq, past, past), 1.0))
        for h in range(DA_HEADS):
            r1 = slice(2 * h * tq, (2 * h + 1) * tq)
            r2 = slice((2 * h + 1) * tq, (2 * h + 2) * tq)
            o = acc_ref[r1, :] / l_ref[r1, :] - acc_ref[r2, :] * (lam / l_ref[r2, :])
            o_ref[:, h * LANES:(h + 1) * LANES] = _da_finish(o, subln_ref[...], lam_init)


def _da_sample(q, cache_k, cache_v, nk, nv, slopes, lams, subln, *, lam_init, pc):
    bd, rows, _ = cache_k.shape
    past = rows // DA_HEADS
    pc = min(pc, past)
    assert past % pc == 0
    n = q.shape[0]
    tq = n // bd
    new = pl.BlockSpec((tq, DA_WIDTH), lambda b, c: (b, 0))
    cache = pl.BlockSpec((None, pc * DA_HEADS, LANES), lambda b, c: (b, c, 0))
    whole = lambda a: pl.BlockSpec(a.shape, lambda b, c: (0,) * a.ndim)
    return pl.pallas_call(
        functools.partial(_da_sample_kernel, lam_init=lam_init),
        grid=(bd, past // pc),
        in_specs=[new, cache, cache, new, new, whole(slopes)] + [whole(a) for a in lams] + [whole(subln)],
        out_specs=new,
        out_shape=jax.ShapeDtypeStruct((n, DA_WIDTH), BF16),
        scratch_shapes=[pltpu.VMEM((2 * DA_HEADS * tq, LANES), F32),
                        pltpu.VMEM((2 * DA_HEADS * tq, 1), F32),
                        pltpu.VMEM((2 * DA_HEADS * tq, 1), F32)],
        compiler_params=_params("parallel", "arbitrary"),
        name="da_sample",
    )(q, cache_k, cache_v, nk, nv, slopes, *lams, subln)


MLA_GROUP = 4


def _mla_sample_kernel(qn_ref, qr_ref, cc_ref, ckrd_ref, nc_ref, nkrd_ref, wukv_ref, o_ref):
    tq = qn_ref.shape[0]
    past = cc_ref.shape[0]
    cc = cc_ref[...].astype(BF16)
    nc = nc_ref[...].astype(BF16)
    rows = MLA_GROUP * tq
    qrow = lax.broadcasted_iota(jnp.int32, (rows, 1), 0) & (tq - 1)
    bias_c = _pos_bias(qrow + past, lax.broadcasted_iota(jnp.int32, (rows, past), 1), None)
    bias_n = _pos_bias(qrow + past, lax.broadcasted_iota(jnp.int32, (rows, tq), 1) + past, None)
    kv_cols = QK_NOPE + V_HEAD
    for g in range(MLA_HEADS // MLA_GROUP):
        heads = range(g * MLA_GROUP, (g + 1) * MLA_GROUP)
        qa = jnp.concatenate(
            [_dot_nt(qn_ref[:, h * QK_NOPE:(h + 1) * QK_NOPE],
                     wukv_ref[:, h * kv_cols:h * kv_cols + QK_NOPE]).astype(BF16) for h in heads], axis=0)
        qr = jnp.concatenate(
            [_select_rope(qr_ref[:, (h // 2) * LANES:(h // 2 + 1) * LANES], h) for h in heads], axis=0)
        sc = _dot_nt(qa, cc) + _dot_nt(qr, ckrd_ref[...]) + bias_c
        sn = _dot_nt(qa, nc) + _dot_nt(qr, nkrd_ref[...]) + bias_n
        m = jnp.maximum(jnp.max(sc, axis=-1, keepdims=True), jnp.max(sn, axis=-1, keepdims=True))
        pc = jnp.exp2(sc - m)
        pn = jnp.exp2(sn - m)
        w = 1.0 / (jnp.sum(pc, axis=-1, keepdims=True) + jnp.sum(pn, axis=-1, keepdims=True))
        lat = ((_dot(pc.astype(BF16), cc) + _dot(pn.astype(BF16), nc)) * w).astype(BF16)
        for j, h in enumerate(heads):
            w_uv = wukv_ref[:, h * kv_cols + QK_NOPE:(h + 1) * kv_cols]
            o_ref[:, h * V_HEAD:(h + 1) * V_HEAD] = _dot(lat[j * tq:(j + 1) * tq, :], w_uv).astype(BF16)


def _mla_sample(qn, qr, cache_ckv, krd_cache, ckv_new, krd_new, w_ukv):
    bd, past, _ = cache_ckv.shape
    n = qn.shape[0]
    tq = n // bd
    assert tq & (tq - 1) == 0
    row = lambda b: (b, 0)
    return pl.pallas_call(
        _mla_sample_kernel,
        grid=(bd,),
        in_specs=[
            pl.BlockSpec((tq, MLA_HEADS * QK_NOPE), row),
            pl.BlockSpec((tq, MLA_HEADS * QK_ROPE), row),
            pl.BlockSpec((None, past, KV_LORA), lambda b: (b, 0, 0)),
            pl.BlockSpec((None, past, LANES), lambda b: (b, 0, 0)),
            pl.BlockSpec((tq, KV_LORA), row),
            pl.BlockSpec((tq, LANES), row),
            _resident(w_ukv.shape),
        ],
        out_specs=pl.BlockSpec((tq, MLA_WIDTH), row),
        out_shape=jax.ShapeDtypeStruct((n, MLA_WIDTH), BF16),
        compiler_params=_params("parallel"),
        name="mla_sample",
    )(qn, qr, cache_ckv, krd_cache, ckv_new, krd_new, w_ukv)


def _proj_out_kernel(x_ref, a_ref, b_ref, w_ref, o_ref):
    o_ref[...] = (x_ref[...] + _dot(a_ref[...], w_ref[:DA_WIDTH, :]) + _dot(b_ref[...], w_ref[DA_WIDTH:, :]))


def _proj_out(x, a_da, a_mla, w_out, *, tm):
    n, d = x.shape
    tm = min(tm, n)
    assert n % tm == 0
    row = lambda i: (i, 0)
    return pl.pallas_call(
        _proj_out_kernel,
        grid=(n // tm,),
        in_specs=[pl.BlockSpec((tm, d), row), pl.BlockSpec((tm, DA_WIDTH), row),
                  pl.BlockSpec((tm, MLA_WIDTH), row), _resident(w_out.shape)],
        out_specs=pl.BlockSpec((tm, d), row),
        out_shape=jax.ShapeDtypeStruct((n, d), F32),
        compiler_params=_params("parallel"),
        name="proj_out",
    )(x, a_da, a_mla, w_out)


def _swap_halves_cols(w):
    half = w.shape[-1] // 2
    return jnp.concatenate([w[..., half:], w[..., :half]], axis=-1)


def _prep_weights(w_in, mix_norm, q_norm, w_uq, kv_norm, w_ukv, w_out):
    w_in = w_in.astype(BF16)
    w_kr = w_in[:, 3 * DA_WIDTH + Q_LORA + KV_LORA:]
    w_uq = w_uq.astype(BF16).reshape(Q_LORA, MLA_HEADS, QK_HEAD)
    w_uqr = w_uq[:, :, QK_NOPE:]
    return {
        "mix_norm": mix_norm[None, :],
        "w_in": w_in,
        "w_kr": jnp.concatenate([w_kr, w_kr], axis=1),
        "w_krs": jnp.concatenate([_swap_halves_cols(w_kr)] * 2, axis=1),
        "q_norm": q_norm[None, :], "kv_norm": kv_norm[None, :],
        "w_uqn": w_uq[:, :, :QK_NOPE].reshape(Q_LORA, MLA_HEADS * QK_NOPE),
        "w_uqr": w_uqr.reshape(Q_LORA, MLA_HEADS * QK_ROPE),
        "w_uqrs": _swap_halves_cols(w_uqr).reshape(Q_LORA, MLA_HEADS * QK_ROPE),
        "w_ukv": w_ukv.astype(BF16),
        "w_out": w_out.astype(BF16),
    }


def _rope_tables(pos):
    half = QK_ROPE // 2
    inv = ROPE_THETA ** (-jnp.arange(half, dtype=F32) / half)
    ang = pos.astype(F32)[:, None] * inv[None, :]
    c = jnp.concatenate([jnp.cos(ang)] * 2, axis=1)
    s = jnp.concatenate([-jnp.sin(ang), jnp.sin(ang)], axis=1)
    return {"c2": jnp.tile(c, (1, 2)), "s2": jnp.tile(s, (1, 2)),
            "c8": jnp.tile(c, (1, MLA_HEADS)), "s8": jnp.tile(s, (1, MLA_HEADS))}


def _layer(x, pos, past, ffn1, ffn2, wmix, slopes, lams, subln, final_norm, lam_init, *, batch):
    n, d = x.shape
    t = n // batch
    x = _ffn(x, *ffn1, final_norm, final_norm=False, tm=1024, tf=512)
    tm_in = 256
    tab_pos = pos if t % tm_in == 0 else jnp.tile(pos, batch)
    (q, k_f, k_b, v_f, v_b, ckv, kr, krd, qn, qr, kv) = _proj_in(x, wmix, _rope_tables(tab_pos), tm=tm_in)
    if past is None:
        a_da = _da_prompt(q, k_b, v_b, slopes, lams, subln, t=t, tq=256, lam_init=lam_init)
        a_mla = _mla_prompt(qn, qr, kv, krd, t=t, tq=256)
    else:
        cache_k, cache_v, cache_ckv, cache_kr = past
        p = cache_k.size // (batch * DA_WIDTH)
        a_da = _da_sample(q, cache_k.reshape(batch, p * DA_HEADS, DA_HEAD_DIM),
                          cache_v.reshape(batch, p * DA_HEADS, DA_HEAD_DIM),
                          k_b, v_b, slopes, lams, subln, lam_init=lam_init, pc=2048)
        cache_kr = cache_kr.reshape(batch, p, QK_ROPE)
        krd_cache = jnp.concatenate([cache_kr, cache_kr], axis=-1).astype(BF16)
        a_mla = _mla_sample(qn, qr, cache_ckv.reshape(batch, p, KV_LORA), krd_cache, ckv, krd, wmix["w_ukv"])
    x = _proj_out(x, a_da, a_mla, wmix["w_out"], tm=512)
    y = _ffn(x, *ffn2, final_norm, final_norm=True, tm=1024, tf=512)
    return y, (k_f, v_f, ckv, kr)


def kernel(x_prompt, x_sample, cache_da_k, cache_da_v, cache_mla_ckv, cache_mla_krope, ffn1_norm, ffn1_w_gate, ffn1_w_up, ffn1_w_down, mix_norm, w_in, da_lambda_q1, da_lambda_k1, da_lambda_q2, da_lambda_k2, da_subln, mla_q_norm, mla_w_uq, mla_kv_norm, mla_w_ukv, w_out, ffn2_norm, ffn2_w_gate, ffn2_w_up, ffn2_w_down, final_norm):
    depth = w_in.shape[0]
    assert depth == 1, "single-layer trunk"
    b, t, d = x_prompt.shape
    bd, td, _ = x_sample.shape
    past_len = cache_da_k.shape[2]
    l = 0
    lam_init = 0.8 - 0.6 * math.exp(-0.3 * l)

    def layer_bf16(w):
        return _cast_bf16(w.reshape(w.shape[1:]))

    ffn1 = (ffn1_norm[l][None, :], layer_bf16(ffn1_w_gate), layer_bf16(ffn1_w_up), layer_bf16(ffn1_w_down))
    ffn2 = (ffn2_norm[l][None, :], layer_bf16(ffn2_w_gate), layer_bf16(ffn2_w_up), layer_bf16(ffn2_w_down))
    wmix = _prep_weights(w_in[l], mix_norm[l], mla_q_norm[l], mla_w_uq[l], mla_kv_norm[l], mla_w_ukv[l],
                         w_out[l])
    slopes = jnp.asarray(np.broadcast_to(
        np.array([2.0 ** (-8.0 * (i + 1) / DA_HEADS) for i in range(DA_HEADS)], np.float32)[:, None, None],
        (DA_HEADS, 1, LANES)))
    lams = (da_lambda_q1[l][None, :], da_lambda_k1[l][None, :], da_lambda_q2[l][None, :],
            da_lambda_k2[l][None, :])
    subln = da_subln[l][None, :]
    fnorm = final_norm[None, :]

    past = (cache_da_k, cache_da_v, cache_mla_ckv, cache_mla_krope)
    y_s, st_s = _layer(x_sample.reshape(bd * td, d), past_len + jnp.arange(td, dtype=jnp.int32), past,
                       ffn1, ffn2, wmix, slopes, lams, subln, fnorm, lam_init, batch=bd)
    y_p, st_p = _layer(x_prompt.reshape(b * t, d), jnp.arange(t, dtype=jnp.int32), None,
                       ffn1, ffn2, wmix, slopes, lams, subln, fnorm, lam_init, batch=b)

    def state(st, nb, nt):
        k_f, v_f, ckv, kr = st
        return (k_f.reshape(1, nb, nt, DA_HEADS, DA_HEAD_DIM), v_f.reshape(1, nb, nt, DA_HEADS, DA_HEAD_DIM),
                ckv.reshape(1, nb, nt, KV_LORA), kr.reshape(1, nb, nt, QK_ROPE))

    return (y_p.reshape(b, t, d), y_s.reshape(bd, td, d)) + state(st_p, b, t) + state(st_s, bd, td)
```

```python
import functools
import math

import jax
import jax.numpy as jnp
import numpy as np
from jax import lax
from jax.experimental import pallas as pl
from jax.experimental.pallas import tpu as pltpu

F32 = jnp.float32
BF16 = jnp.bfloat16

CHUNK = 64
CHUNK_SHIFT = 6
assert 1 << CHUNK_SHIFT == CHUNK
EPS = 1e-6
NEG_INF = -1e30
DA_HEADS = 8
DA_HALF = 64
DA_HEAD_DIM = 2 * DA_HALF
DA_WIDTH = DA_HEADS * DA_HEAD_DIM
MLA_HEADS = 8
Q_LORA = 512
KV_LORA = 256
QK_NOPE = 128
QK_ROPE = 64
QK_HEAD = QK_NOPE + QK_ROPE
V_HEAD = 128
MLA_WIDTH = MLA_HEADS * V_HEAD
ROPE_THETA = 10000.0
LOG2E = math.log2(math.e)
LANES = 128
assert DA_HEAD_DIM == LANES and QK_NOPE == LANES and V_HEAD == LANES and 2 * QK_ROPE == LANES

VMEM_LIMIT_BYTES = 56 * 1024 * 1024


def _params(*semantics, vmem_limit_bytes=VMEM_LIMIT_BYTES):
    return pltpu.CompilerParams(dimension_semantics=semantics, vmem_limit_bytes=vmem_limit_bytes)


def _rms(x, g):
    return x * lax.rsqrt(jnp.mean(x * x, axis=-1, keepdims=True) + EPS) * g


def _dot(a, b):
    return jnp.dot(a, b, preferred_element_type=F32)


def _dot_nt(a, b):
    return lax.dot_general(a, b, (((1,), (1,)), ((), ())), preferred_element_type=F32)


def _resident(shape):
    return pl.BlockSpec(shape, lambda *_: (0,) * len(shape), pipeline_mode=pl.Buffered(1))


def _cast_kernel(w_ref, o_ref):
    o_ref[...] = w_ref[...].astype(BF16)


def _cast_bf16(w, *, block_bytes=4 * 1024 * 1024):
    r, c = w.shape
    tr = 1 << int(math.log2(max(16, min(r, block_bytes // (4 * c)))))
    while r % tr:
        tr //= 2
    assert tr % 16 == 0 or tr == r
    return pl.pallas_call(
        _cast_kernel,
        grid=(r // tr,),
        in_specs=[pl.BlockSpec((tr, c), lambda i: (i, 0))],
        out_specs=pl.BlockSpec((tr, c), lambda i: (i, 0)),
        out_shape=jax.ShapeDtypeStruct((r, c), BF16),
        compiler_params=_params("parallel"),
        name="cast_bf16",
    )(w)


def _ffn_kernel(x_ref, g_ref, wg_ref, wu_ref, wd_ref, fn_ref, o_ref, xn_ref, *, final_norm, nj):
    j = pl.program_id(1)

    @pl.when(j == 0)
    def _():
        xn_ref[...] = _rms(x_ref[...], g_ref[...]).astype(BF16)

    def half_down():
        xn = xn_ref[...]
        gate = _dot(xn, wg_ref[...])
        up = _dot(xn, wu_ref[...])
        h = (gate * jax.nn.sigmoid(gate) * (0.5 * up)).astype(BF16)
        return _dot(h, wd_ref[...])

    def finish(y):
        o_ref[...] = _rms(y, fn_ref[...]) if final_norm else y

    if nj == 1:
        finish(x_ref[...] + half_down())
        return

    @pl.when(j == 0)
    def _():
        o_ref[...] = x_ref[...] + half_down()

    @pl.when(jnp.logical_and(j > 0, j < nj - 1))
    def _():
        o_ref[...] += half_down()

    @pl.when(j == nj - 1)
    def _():
        finish(o_ref[...] + half_down())


def _ffn(x, norm, wg, wu, wd, fnorm, *, final_norm, tm, tf):
    n, d = x.shape
    f = wg.shape[1]
    tm = min(tm, n)
    tf = min(tf, f)
    assert n % tm == 0 and f % tf == 0
    return pl.pallas_call(
        functools.partial(_ffn_kernel, final_norm=final_norm, nj=f // tf),
        grid=(n // tm, f // tf),
        in_specs=[
            pl.BlockSpec((tm, d), lambda i, j: (i, 0)),
            pl.BlockSpec((1, d), lambda i, j: (0, 0)),
            pl.BlockSpec((d, tf), lambda i, j: (0, j)),
            pl.BlockSpec((d, tf), lambda i, j: (0, j)),
            pl.BlockSpec((tf, d), lambda i, j: (j, 0)),
            pl.BlockSpec((1, d), lambda i, j: (0, 0)),
        ],
        out_specs=pl.BlockSpec((tm, d), lambda i, j: (i, 0)),
        out_shape=jax.ShapeDtypeStruct((n, d), F32),
        scratch_shapes=[pltpu.VMEM((tm, d), BF16)],
        compiler_params=_params("parallel", "arbitrary", vmem_limit_bytes=62 * 1024 * 1024),
        name="ffn",
    )(x, norm, wg, wu, wd, fnorm)


def _store_by_head(ref, z):
    tokens = z.shape[0]
    for h in range(DA_HEADS):
        ref[pl.ds(h, tokens, stride=DA_HEADS), :] = z[:, h * DA_HEAD_DIM:(h + 1) * DA_HEAD_DIM]


def _proj_in_kernel(x_ref, g_ref, win_ref, wkr_ref, wkrs_ref,
                    qn_ref, kvn_ref, wuqn_ref, wuqr_ref, wuqrs_ref, wukv_ref,
                    c2_ref, s2_ref, c8_ref, s8_ref,
                    q_ref, kf_ref, kb_ref, vf_ref, vb_ref, ckv_ref, kr_ref, krd_ref,
                    qnope_ref, qrope_ref, kv_ref):
    h = _rms(x_ref[...], g_ref[...]).astype(BF16)
    o1, o2, o3 = DA_WIDTH, 2 * DA_WIDTH, 3 * DA_WIDTH
    o4, o5 = o3 + Q_LORA, o3 + Q_LORA + KV_LORA
    q_ref[...] = (_dot(h, win_ref[:, :o1]) * (DA_HALF ** -0.5 * LOG2E)).astype(BF16)
    k = _dot(h, win_ref[:, o1:o2])
    _store_by_head(kf_ref, k)
    kb_ref[...] = k.astype(BF16)
    v = _dot(h, win_ref[:, o2:o3])
    _store_by_head(vf_ref, v)
    vb_ref[...] = v.astype(BF16)
    cq = _rms(_dot(h, win_ref[:, o3:o4]), qn_ref[...]).astype(BF16)
    ckv = _rms(_dot(h, win_ref[:, o4:o5]), kvn_ref[...])
    ckv_ref[...] = ckv
    krd = _dot(h, wkr_ref[...]) * c2_ref[...] + _dot(h, wkrs_ref[...]) * s2_ref[...]
    kr_ref[...] = krd[:, :QK_ROPE]
    krd_ref[...] = krd.astype(BF16)
    mla_scale = QK_HEAD ** -0.5 * LOG2E
    qnope_ref[...] = (_dot(cq, wuqn_ref[...]) * mla_scale).astype(BF16)
    qr = _dot(cq, wuqr_ref[...]) * c8_ref[...] + _dot(cq, wuqrs_ref[...]) * s8_ref[...]
    qrope_ref[...] = (qr * mla_scale).astype(BF16)
    kv_ref[...] = _dot(ckv.astype(BF16), wukv_ref[...]).astype(BF16)


def _proj_in(x, w, tabs, *, tm):
    n, d = x.shape
    tm = min(tm, n)
    assert n % tm == 0
    t_rows = tabs["c2"].shape[0]
    assert t_rows % tm == 0 or tm % t_rows == 0
    if tm > t_rows:
        tm = t_rows
    nt = t_rows // tm

    def row(i):
        return (i, 0)

    def trow(i):
        return (i % nt, 0)

    weights = [w["w_in"], w["w_kr"], w["w_krs"],
               w["q_norm"], w["kv_norm"], w["w_uqn"], w["w_uqr"], w["w_uqrs"], w["w_ukv"]]
    tables = [tabs["c2"], tabs["s2"], tabs["c8"], tabs["s8"]]
    outs = [
        ((n, DA_WIDTH), BF16),
        ((n * DA_HEADS, DA_HEAD_DIM), F32),
        ((n, DA_WIDTH), BF16),
        ((n * DA_HEADS, DA_HEAD_DIM), F32),
        ((n, DA_WIDTH), BF16),
        ((n, KV_LORA), F32),
        ((n, QK_ROPE), F32),
        ((n, LANES), BF16),
        ((n, MLA_HEADS * QK_NOPE), BF16),
        ((n, MLA_HEADS * QK_ROPE), BF16),
        ((n, MLA_HEADS * (QK_NOPE + V_HEAD)), BF16),
    ]
    return pl.pallas_call(
        _proj_in_kernel,
        grid=(n // tm,),
        in_specs=([pl.BlockSpec((tm, d), row), _resident((1, d))]
                  + [_resident(a.shape) for a in weights]
                  + [pl.BlockSpec((tm, a.shape[1]), trow) for a in tables]),
        out_specs=[pl.BlockSpec((tm * s[0] // n, s[1]), row) for s, _ in outs],
        out_shape=[jax.ShapeDtypeStruct(s, dt) for s, dt in outs],
        compiler_params=_params("parallel"),
        name="proj_in",
    )(x, w["mix_norm"], *weights, *tables)


def _lambda(lq1_ref, lk1_ref, lq2_ref, lk2_ref, lam_init):
    a = jnp.sum(lq1_ref[...] * lk1_ref[...], axis=-1, keepdims=True)
    b = jnp.sum(lq2_ref[...] * lk2_ref[...], axis=-1, keepdims=True)
    return jnp.exp(a) - jnp.exp(b) + lam_init


def _pos_bias(qpos, kpos, slope):
    visible = lax.shift_right_arithmetic(kpos, CHUNK_SHIFT) <= lax.shift_right_arithmetic(qpos, CHUNK_SHIFT)
    if slope is None:
        return jnp.where(visible, 0.0, NEG_INF)
    dist = jnp.abs(qpos - kpos).astype(F32)
    return jnp.where(visible, (-LOG2E * slope) * dist, NEG_INF)


def _strip(tq, t, slope):
    qpos = lax.broadcasted_iota(jnp.int32, (tq, t), 0)
    kpos = lax.broadcasted_iota(jnp.int32, (tq, t), 1) - (t - tq)
    return _pos_bias(qpos, kpos, slope)


def _softmax_parts(s):
    m = jnp.max(s, axis=-1, keepdims=True)
    p = jnp.exp2(s - m)
    return p, jnp.sum(p, axis=-1, keepdims=True)


def _da_combine(p1, l1, p2, l2, lam):
    return (p1 - p2 * (lam * l1 / l2)).astype(BF16), 1.0 / l1


def _split_halves(q):
    lane = lax.broadcasted_iota(jnp.int32, q.shape, 1)
    zero = jnp.zeros_like(q)
    return jnp.where(lane < DA_HALF, q, zero), jnp.where(lane >= DA_HALF, q, zero)


def _da_finish(o, subln, lam_init):
    return (_rms(o, subln) * (1.0 - lam_init)).astype(BF16)


def _da_prompt_kernel(q_ref, k_ref, v_ref, slope_ref, lq1_ref, lk1_ref, lq2_ref, lk2_ref, subln_ref,
                      o_ref, strip_ref, *, tq, lam_init):
    t = q_ref.shape[0]
    lam = _lambda(lq1_ref, lk1_ref, lq2_ref, lk2_ref, lam_init)
    strip_ref[...] = _strip(tq, t, slope_ref[0][:, :1])

    def scores(qi):
        ext = (qi + 1) * tq
        q_lo, q_hi = _split_halves(q_ref[qi * tq:ext, :])
        k = k_ref[0:ext, :]
        bias = strip_ref[:, t - ext:t]
        return _dot_nt(q_lo, k) + bias, _dot_nt(q_hi, k) + bias

    nq = t // tq
    s_next = scores(0)
    for qi in range(nq):
        ext = (qi + 1) * tq
        s1, s2 = s_next
        if qi + 1 < nq:
            s_next = scores(qi + 1)
        p1, l1 = _softmax_parts(s1)
        p2, l2 = _softmax_parts(s2)
        a, w = _da_combine(p1, l1, p2, l2, lam)
        o = _dot(a, v_ref[0:ext, :]) * w
        o_ref[qi * tq:ext, :] = _da_finish(o, subln_ref[...], lam_init)


def _da_prompt(q, k, v, slopes, lams, subln, *, t, tq, lam_init):
    n = q.shape[0]
    tq = min(tq, t)
    blk = pl.BlockSpec((t, LANES), lambda b, h: (b, h))
    vec = lambda a: pl.BlockSpec(a.shape, lambda b, h: (0, 0))
    return pl.pallas_call(
        functools.partial(_da_prompt_kernel, tq=tq, lam_init=lam_init),
        grid=(n // t, DA_HEADS),
        in_specs=[blk, blk, blk, pl.BlockSpec((1, 1, LANES), lambda b, h: (h, 0, 0))]
                 + [vec(a) for a in lams] + [vec(subln)],
        out_specs=blk,
        out_shape=jax.ShapeDtypeStruct((n, DA_WIDTH), BF16),
        scratch_shapes=[pltpu.VMEM((tq, t), F32)],
        compiler_params=_params("parallel", "parallel"),
        name="da_prompt",
    )(q, k, v, slopes, *lams, subln)


def _select_rope(qr, h):
    lane = lax.broadcasted_iota(jnp.int32, qr.shape, 1)
    mine = lax.shift_right_logical(lane, CHUNK_SHIFT) == (h % 2)
    return jnp.where(mine, qr, jnp.zeros_like(qr))


def _mla_prompt_kernel(qn_ref, qr_ref, kn_ref, v_ref, krd_ref, o_ref, mask_ref, kfull_ref, *, tq):
    t = qn_ref.shape[0]
    h = pl.program_id(1)
    mask_ref[...] = _strip(tq, tq, None)
    kfull_ref[:, :LANES] = kn_ref[...]
    kfull_ref[:, LANES:] = krd_ref[...]
    def scores(qi):
        lo, ext = qi * tq, (qi + 1) * tq
        q = jnp.concatenate([qn_ref[lo:ext, :], _select_rope(qr_ref[lo:ext, :], h)], axis=1)
        sd = _dot_nt(q, kfull_ref[lo:ext, :]) + mask_ref[...]
        sp = _dot_nt(q, kfull_ref[0:lo, :]) if qi > 0 else None
        return sd, sp

    nq = t // tq
    s_next = scores(0)
    for qi in range(nq):
        lo, ext = qi * tq, (qi + 1) * tq
        sd, sp = s_next
        if qi + 1 < nq:
            s_next = scores(qi + 1)
        md = jnp.max(sd, axis=-1, keepdims=True)
        if qi == 0:
            pd = jnp.exp2(sd - md)
            l = jnp.sum(pd, axis=-1, keepdims=True)
            o = _dot(pd.astype(BF16), v_ref[lo:ext, :])
        else:
            m = jnp.maximum(md, jnp.max(sp, axis=-1, keepdims=True))
            pd = jnp.exp2(sd - m)
            pp = jnp.exp2(sp - m)
            l = jnp.sum(pd, axis=-1, keepdims=True) + jnp.sum(pp, axis=-1, keepdims=True)
            o = _dot(pp.astype(BF16), v_ref[0:lo, :]) + _dot(pd.astype(BF16), v_ref[lo:ext, :])
        o_ref[lo:ext, :] = (o * (1.0 / l)).astype(BF16)


def _mla_prompt(qn, qr, kv, krd, *, t, tq):
    n = qn.shape[0]
    tq = min(tq, t)
    return pl.pallas_call(
        functools.partial(_mla_prompt_kernel, tq=tq),
        grid=(n // t, MLA_HEADS),
        in_specs=[
            pl.BlockSpec((t, LANES), lambda b, h: (b, h)),
            pl.BlockSpec((t, LANES), lambda b, h: (b, h // 2)),
            pl.BlockSpec((t, LANES), lambda b, h: (b, 2 * h)),
            pl.BlockSpec((t, LANES), lambda b, h: (b, 2 * h + 1)),
            pl.BlockSpec((t, LANES), lambda b, h: (b, 0)),
        ],
        out_specs=pl.BlockSpec((t, LANES), lambda b, h: (b, h)),
        out_shape=jax.ShapeDtypeStruct((n, MLA_WIDTH), BF16),
        scratch_shapes=[pltpu.VMEM((tq, tq), F32), pltpu.VMEM((t, 2 * LANES), BF16)],
        compiler_params=_params("parallel", "parallel"),
        name="mla_prompt",
    )(qn, qr, kv, kv, krd)


def _positions(tq, tk, q0, k0):
    qpos = lax.broadcasted_iota(jnp.int32, (tq, tk), 0) + q0
    kpos = lax.broadcasted_iota(jnp.int32, (tq, tk), 1) + k0
    return qpos, kpos


def _da_sample_kernel(q_ref, ck_ref, cv_ref, nk_ref, nv_ref, slope_ref, lq1_ref, lk1_ref, lq2_ref, lk2_ref,
                      subln_ref, o_ref, acc_ref, m_ref, l_ref, *, lam_init):
    c = pl.program_id(1)
    nc = pl.num_programs(1)
    tq = q_ref.shape[0]
    pc = ck_ref.shape[0] // DA_HEADS
    past = pc * nc

    @pl.when(c == 0)
    def _():
        m_ref[...] = jnp.full_like(m_ref, NEG_INF)
        l_ref[...] = jnp.zeros_like(l_ref)
        acc_ref[...] = jnp.zeros_like(acc_ref)

    def attend(keys, values, unit_bias):
        unit2 = jnp.concatenate([unit_bias, unit_bias], axis=0)
        scores = []
        for h in range(DA_HEADS):
            q2 = jnp.concatenate(_split_halves(q_ref[:, h * LANES:(h + 1) * LANES]), axis=0)
            scores.append(_dot_nt(q2, keys(h)) + unit2 * slope_ref[h][:, :1])
        probs = []
        for h, s in enumerate(scores):
            rows = slice(2 * h * tq, (2 * h + 2) * tq)
            m_old = m_ref[rows, :]
            m_new = jnp.maximum(m_old, jnp.max(s, axis=-1, keepdims=True))
            alpha = jnp.exp2(m_old - m_new)
            p = jnp.exp2(s - m_new)
            m_ref[rows, :] = m_new
            l_ref[rows, :] = alpha * l_ref[rows, :] + jnp.sum(p, axis=-1, keepdims=True)
            probs.append((alpha, p.astype(BF16)))
        for h, (alpha, p) in enumerate(probs):
            rows = slice(2 * h * tq, (2 * h + 2) * tq)
            acc_ref[rows, :] = alpha * acc_ref[rows, :] + _dot(p, values(h))

    def head_rows(h):
        return pl.ds(h, pc, stride=DA_HEADS)

    attend(lambda h: ck_ref[head_rows(h), :].astype(BF16), lambda h: cv_ref[head_rows(h), :].astype(BF16),
           _pos_bias(*_positions(tq, pc, past, c * pc), 1.0))

    @pl.when(c == nc - 1)
    def _():
        lam = _lambda(lq1_ref, lk1_ref, lq2_ref, lk2_ref, lam_init)
        attend(lambda h: nk_ref[:, h * LANES:(h + 1) * LANES], lambda h: nv_ref[:, h * LANES:(h + 1) * LANES],
               _pos_bias(*_positions(tq, tq, past, past), 1.0))
        for h in range(DA_HEADS):
            r1 = slice(2 * h * tq, (2 * h + 1) * tq)
            r2 = slice((2 * h + 1) * tq, (2 * h + 2) * tq)
            o = acc_ref[r1, :] / l_ref[r1, :] - acc_ref[r2, :] * (lam / l_ref[r2, :])
            o_ref[:, h * LANES:(h + 1) * LANES] = _da_finish(o, subln_ref[...], lam_init)


def _da_sample(q, cache_k, cache_v, nk, nv, slopes, lams, subln, *, lam_init, pc):
    bd, rows, _ = cache_k.shape
    past = rows // DA_HEADS
    pc = min(pc, past)
    assert past % pc == 0
    n = q.shape[0]
    tq = n // bd
    new = pl.BlockSpec((tq, DA_WIDTH), lambda b, c: (b, 0))
    cache = pl.BlockSpec((None, pc * DA_HEADS, LANES), lambda b, c: (b, c, 0))
    whole = lambda a: pl.BlockSpec(a.shape, lambda b, c: (0,) * a.ndim)
    return pl.pallas_call(
        functools.partial(_da_sample_kernel, lam_init=lam_init),
        grid=(bd, past // pc),
        in_specs=[new, cache, cache, new, new, whole(slopes)] + [whole(a) for a in lams] + [whole(subln)],
        out_specs=new,
        out_shape=jax.ShapeDtypeStruct((n, DA_WIDTH), BF16),
        scratch_shapes=[pltpu.VMEM((2 * DA_HEADS * tq, LANES), F32),
                        pltpu.VMEM((2 * DA_HEADS * tq, 1), F32),
                        pltpu.VMEM((2 * DA_HEADS * tq, 1), F32)],
        compiler_params=_params("parallel", "arbitrary"),
        name="da_sample",
    )(q, cache_k, cache_v, nk, nv, slopes, *lams, subln)


MLA_GROUP = 4


def _mla_sample_kernel(qn_ref, qr_ref, cc_ref, ckrd_ref, nc_ref, nkrd_ref, wukv_ref, o_ref):
    tq = qn_ref.shape[0]
    past = cc_ref.shape[0]
    cc = cc_ref[...].astype(BF16)
    nc = nc_ref[...].astype(BF16)
    rows = MLA_GROUP * tq
    qrow = lax.broadcasted_iota(jnp.int32, (rows, 1), 0) & (tq - 1)
    bias_c = _pos_bias(qrow + past, lax.broadcasted_iota(jnp.int32, (rows, past), 1), None)
    bias_n = _pos_bias(qrow + past, lax.broadcasted_iota(jnp.int32, (rows, tq), 1) + past, None)
    kv_cols = QK_NOPE + V_HEAD
    for g in range(MLA_HEADS // MLA_GROUP):
        heads = range(g * MLA_GROUP, (g + 1) * MLA_GROUP)
        qa = jnp.concatenate(
            [_dot_nt(qn_ref[:, h * QK_NOPE:(h + 1) * QK_NOPE],
                     wukv_ref[:, h * kv_cols:h * kv_cols + QK_NOPE]).astype(BF16) for h in heads], axis=0)
        qr = jnp.concatenate(
            [_select_rope(qr_ref[:, (h // 2) * LANES:(h // 2 + 1) * LANES], h) for h in heads], axis=0)
        sc = _dot_nt(qa, cc) + _dot_nt(qr, ckrd_ref[...]) + bias_c
        sn = _dot_nt(qa, nc) + _dot_nt(qr, nkrd_ref[...]) + bias_n
        m = jnp.maximum(jnp.max(sc, axis=-1, keepdims=True), jnp.max(sn, axis=-1, keepdims=True))
        pc = jnp.exp2(sc - m)
        pn = jnp.exp2(sn - m)
        w = 1.0 / (jnp.sum(pc, axis=-1, keepdims=True) + jnp.sum(pn, axis=-1, keepdims=True))
        lat = ((_dot(pc.astype(BF16), cc) + _dot(pn.astype(BF16), nc)) * w).astype(BF16)
        for j, h in enumerate(heads):
            w_uv = wukv_ref[:, h * kv_cols + QK_NOPE:(h + 1) * kv_cols]
            o_ref[:, h * V_HEAD:(h + 1) * V_HEAD] = _dot(lat[j * tq:(j + 1) * tq, :], w_uv).astype(BF16)


def _mla_sample(qn, qr, cache_ckv, krd_cache, ckv_new, krd_new, w_ukv):
    bd, past, _ = cache_ckv.shape
    n = qn.shape[0]
    tq = n // bd
    assert tq & (tq - 1) == 0
    row = lambda b: (b, 0)
    return pl.pallas_call(
        _mla_sample_kernel,
        grid=(bd,),
        in_specs=[
            pl.BlockSpec((tq, MLA_HEADS * QK_NOPE), row),
            pl.BlockSpec((tq, MLA_HEADS * QK_ROPE), row),
            pl.BlockSpec((None, past, KV_LORA), lambda b: (b, 0, 0)),
            pl.BlockSpec((None, past, LANES), lambda b: (b, 0, 0)),
            pl.BlockSpec((tq, KV_LORA), row),
            pl.BlockSpec((tq, LANES), row),
            _resident(w_ukv.shape),
        ],
        out_specs=pl.BlockSpec((tq, MLA_WIDTH), row),
        out_shape=jax.ShapeDtypeStruct((n, MLA_WIDTH), BF16),
        compiler_params=_params("parallel"),
        name="mla_sample",
    )(qn, qr, cache_ckv, krd_cache, ckv_new, krd_new, w_ukv)


def _proj_out_kernel(x_ref, a_ref, b_ref, w_ref, o_ref):
    o_ref[...] = (x_ref[...] + _dot(a_ref[...], w_ref[:DA_WIDTH, :]) + _dot(b_ref[...], w_ref[DA_WIDTH:, :]))


def _proj_out(x, a_da, a_mla, w_out, *, tm):
    n, d = x.shape
    tm = min(tm, n)
    assert n % tm == 0
    row = lambda i: (i, 0)
    return pl.pallas_call(
        _proj_out_kernel,
        grid=(n // tm,),
        in_specs=[pl.BlockSpec((tm, d), row), pl.BlockSpec((tm, DA_WIDTH), row),
                  pl.BlockSpec((tm, MLA_WIDTH), row), _resident(w_out.shape)],
        out_specs=pl.BlockSpec((tm, d), row),
        out_shape=jax.ShapeDtypeStruct((n, d), F32),
        compiler_params=_params("parallel"),
        name="proj_out",
    )(x, a_da, a_mla, w_out)


def _swap_halves_cols(w):
    half = w.shape[-1] // 2
    return jnp.concatenate([w[..., half:], w[..., :half]], axis=-1)


def _prep_weights(w_in, mix_norm, q_norm, w_uq, kv_norm, w_ukv, w_out):
    w_in = w_in.astype(BF16)
    w_kr = w_in[:, 3 * DA_WIDTH + Q_LORA + KV_LORA:]
    w_uq = w_uq.astype(BF16).reshape(Q_LORA, MLA_HEADS, QK_HEAD)
    w_uqr = w_uq[:, :, QK_NOPE:]
    return {
        "mix_norm": mix_norm[None, :],
        "w_in": w_in,
        "w_kr": jnp.concatenate([w_kr, w_kr], axis=1),
        "w_krs": jnp.concatenate([_swap_halves_cols(w_kr)] * 2, axis=1),
        "q_norm": q_norm[None, :], "kv_norm": kv_norm[None, :],
        "w_uqn": w_uq[:, :, :QK_NOPE].reshape(Q_LORA, MLA_HEADS * QK_NOPE),
        "w_uqr": w_uqr.reshape(Q_LORA, MLA_HEADS * QK_ROPE),
        "w_uqrs": _swap_halves_cols(w_uqr).reshape(Q_LORA, MLA_HEADS * QK_ROPE),
        "w_ukv": w_ukv.astype(BF16),
        "w_out": w_out.astype(BF16),
    }


def _rope_tables(pos):
    half = QK_ROPE // 2
    inv = ROPE_THETA ** (-jnp.arange(half, dtype=F32) / half)
    ang = pos.astype(F32)[:, None] * inv[None, :]
    c = jnp.concatenate([jnp.cos(ang)] * 2, axis=1)
    s = jnp.concatenate([-jnp.sin(ang), jnp.sin(ang)], axis=1)
    return {"c2": jnp.tile(c, (1, 2)), "s2": jnp.tile(s, (1, 2)),
            "c8": jnp.tile(c, (1, MLA_HEADS)), "s8": jnp.tile(s, (1, MLA_HEADS))}


def _layer(x, pos, past, ffn1, ffn2, wmix, slopes, lams, subln, final_norm, lam_init, *, batch):
    n, d = x.shape
    t = n // batch
    x = _ffn(x, *ffn1, final_norm, final_norm=False, tm=1024, tf=512)
    tm_in = 256
    tab_pos = pos if t % tm_in == 0 else jnp.tile(pos, batch)
    (q, k_f, k_b, v_f, v_b, ckv, kr, krd, qn, qr, kv) = _proj_in(x, wmix, _rope_tables(tab_pos), tm=tm_in)
    if past is None:
        a_da = _da_prompt(q, k_b, v_b, slopes, lams, subln, t=t, tq=256, lam_init=lam_init)
        a_mla = _mla_prompt(qn, qr, kv, krd, t=t, tq=256)
    else:
        cache_k, cache_v, cache_ckv, cache_kr = past
        p = cache_k.size // (batch * DA_WIDTH)
        a_da = _da_sample(q, cache_k.reshape(batch, p * DA_HEADS, DA_HEAD_DIM),
                          cache_v.reshape(batch, p * DA_HEADS, DA_HEAD_DIM),
                          k_b, v_b, slopes, lams, subln, lam_init=lam_init, pc=2048)
        cache_kr = cache_kr.reshape(batch, p, QK_ROPE)
        krd_cache = jnp.concatenate([cache_kr, cache_kr], axis=-1).astype(BF16)
        a_mla = _mla_sample(qn, qr, cache_ckv.reshape(batch, p, KV_LORA), krd_cache, ckv, krd, wmix["w_ukv"])
    x = _proj_out(x, a_da, a_mla, wmix["w_out"], tm=512)
    y = _ffn(x, *ffn2, final_norm, final_norm=True, tm=1024, tf=512)
    return y, (k_f, v_f, ckv, kr)


def kernel(x_prompt, x_sample, cache_da_k, cache_da_v, cache_mla_ckv, cache_mla_krope, ffn1_norm, ffn1_w_gate, ffn1_w_up, ffn1_w_down, mix_norm, w_in, da_lambda_q1, da_lambda_k1, da_lambda_q2, da_lambda_k2, da_subln, mla_q_norm, mla_w_uq, mla_kv_norm, mla_w_ukv, w_out, ffn2_norm, ffn2_w_gate, ffn2_w_up, ffn2_w_down, final_norm):
    depth = w_in.shape[0]
    assert depth == 1, "single-layer trunk"
    b, t, d = x_prompt.shape
    bd, td, _ = x_sample.shape
    past_len = cache_da_k.shape[2]
    l = 0
    lam_init = 0.8 - 0.6 * math.exp(-0.3 * l)

    def layer_bf16(w):
        return _cast_bf16(w.reshape(w.shape[1:]))

    ffn1 = (ffn1_norm[l][None, :], layer_bf16(ffn1_w_gate), layer_bf16(ffn1_w_up), layer_bf16(ffn1_w_down))
    ffn2 = (ffn2_norm[l][None, :], layer_bf16(ffn2_w_gate), layer_bf16(ffn2_w_up), layer_bf16(ffn2_w_down))
    wmix = _prep_weights(w_in[l], mix_norm[l], mla_q_norm[l], mla_w_uq[l], mla_kv_norm[l], mla_w_ukv[l],
                         w_out[l])
    slopes = jnp.asarray(np.broadcast_to(
        np.array([2.0 ** (-8.0 * (i + 1) / DA_HEADS) for i in range(DA_HEADS)], np.float32)[:, None, None],
        (DA_HEADS, 1, LANES)))
    lams = (da_lambda_q1[l][None, :], da_lambda_k1[l][None, :], da_lambda_q2[l][None, :],
            da_lambda_k2[l][None, :])
    subln = da_subln[l][None, :]
    fnorm = final_norm[None, :]

    past = (cache_da_k, cache_da_v, cache_mla_ckv, cache_mla_krope)
    y_s, st_s = _layer(x_sample.reshape(bd * td, d), past_len + jnp.arange(td, dtype=jnp.int32), past,
                       ffn1, ffn2, wmix, slopes, lams, subln, fnorm, lam_init, batch=bd)
    y_p, st_p = _layer(x_prompt.reshape(b * t, d), jnp.arange(t, dtype=jnp.int32), None,
                       ffn1, ffn2, wmix, slopes, lams, subln, fnorm, lam_init, batch=b)

    def state(st, nb, nt):
        k_f, v_f, ckv, kr = st
        return (k_f.reshape(1, nb, nt, DA_HEADS, DA_HEAD_DIM), v_f.reshape(1, nb, nt, DA_HEADS, DA_HEAD_DIM),
                ckv.reshape(1, nb, nt, KV_LORA), kr.reshape(1, nb, nt, QK_ROPE))

    return (y_p.reshape(b, t, d), y_s.reshape(bd, td, d)) + state(st_p, b, t) + state(st_s, bd, td)
```

```python
import functools
import math

import jax
import jax.numpy as jnp
import numpy as np
from jax import lax
from jax.experimental import pallas as pl
from jax.experimental.pallas import tpu as pltpu

F32 = jnp.float32
BF16 = jnp.bfloat16

CHUNK = 64
CHUNK_SHIFT = 6
assert 1 << CHUNK_SHIFT == CHUNK
EPS = 1e-6
NEG_INF = -1e30
DA_HEADS = 8
DA_HALF = 64
DA_HEAD_DIM = 2 * DA_HALF
DA_WIDTH = DA_HEADS * DA_HEAD_DIM
MLA_HEADS = 8
Q_LORA = 512
KV_LORA = 256
QK_NOPE = 128
QK_ROPE = 64
QK_HEAD = QK_NOPE + QK_ROPE
V_HEAD = 128
MLA_WIDTH = MLA_HEADS * V_HEAD
ROPE_THETA = 10000.0
LOG2E = math.log2(math.e)
LANES = 128
assert DA_HEAD_DIM == LANES and QK_NOPE == LANES and V_HEAD == LANES and 2 * QK_ROPE == LANES

VMEM_LIMIT_BYTES = 56 * 1024 * 1024


def _params(*semantics, vmem_limit_bytes=VMEM_LIMIT_BYTES):
    return pltpu.CompilerParams(dimension_semantics=semantics, vmem_limit_bytes=vmem_limit_bytes)


def _rms(x, g):
    return x * lax.rsqrt(jnp.mean(x * x, axis=-1, keepdims=True) + EPS) * g


def _dot(a, b):
    return jnp.dot(a, b, preferred_element_type=F32)


def _dot_nt(a, b):
    return lax.dot_general(a, b, (((1,), (1,)), ((), ())), preferred_element_type=F32)


def _resident(shape):
    return pl.BlockSpec(shape, lambda *_: (0,) * len(shape), pipeline_mode=pl.Buffered(1))


def _ffn_kernel(x_ref, g_ref, wg_ref, wu_ref, wd_ref, fn_ref, o_ref, *rest, final_norm, nj, emit_bf16):
    j = pl.program_id(1)
    xn_ref = rest[-1]

    @pl.when(j == 0)
    def _():
        xn_ref[...] = _rms(x_ref[...], g_ref[...]).astype(BF16)

    def weights():
        if not emit_bf16:
            return wg_ref[...], wu_ref[...], wd_ref[...]
        blocks = []
        for w_ref, wb_ref in zip((wg_ref, wu_ref, wd_ref), rest[:3]):
            wb = w_ref[...].astype(BF16)
            wb_ref[...] = wb
            blocks.append(wb)
        return blocks

    def half_down():
        xn = xn_ref[...]
        wg, wu, wd = weights()
        gate = _dot(xn, wg)
        up = _dot(xn, wu)
        h = (gate * jax.nn.sigmoid(gate) * (0.5 * up)).astype(BF16)
        return _dot(h, wd)

    def finish(y):
        o_ref[...] = _rms(y, fn_ref[...]) if final_norm else y

    if nj == 1:
        finish(x_ref[...] + half_down())
        return

    @pl.when(j == 0)
    def _():
        o_ref[...] = x_ref[...] + half_down()

    @pl.when(jnp.logical_and(j > 0, j < nj - 1))
    def _():
        o_ref[...] += half_down()

    @pl.when(j == nj - 1)
    def _():
        finish(o_ref[...] + half_down())


def _ffn(x, norm, wg, wu, wd, fnorm, *, final_norm, tm, tf):
    n, d = x.shape
    f = wg.shape[1]
    tm = min(tm, n)
    tf = min(tf, f)
    assert n % tm == 0 and f % tf == 0
    emit_bf16 = wg.dtype == F32
    assert not emit_bf16 or n == tm
    col_block = pl.BlockSpec((d, tf), lambda i, j: (0, j))
    row_block = pl.BlockSpec((tf, d), lambda i, j: (j, 0))
    y_spec = pl.BlockSpec((tm, d), lambda i, j: (i, 0))
    y_shape = jax.ShapeDtypeStruct((n, d), F32)
    out = pl.pallas_call(
        functools.partial(_ffn_kernel, final_norm=final_norm, nj=f // tf, emit_bf16=emit_bf16),
        grid=(n // tm, f // tf),
        in_specs=[
            pl.BlockSpec((tm, d), lambda i, j: (i, 0)),
            pl.BlockSpec((1, d), lambda i, j: (0, 0)),
            col_block, col_block, row_block,
            pl.BlockSpec((1, d), lambda i, j: (0, 0)),
        ],
        out_specs=[y_spec, col_block, col_block, row_block] if emit_bf16 else y_spec,
        out_shape=([y_shape] + [jax.ShapeDtypeStruct(w.shape, BF16) for w in (wg, wu, wd)]
                   if emit_bf16 else y_shape),
        scratch_shapes=[pltpu.VMEM((tm, d), BF16)],
        compiler_params=_params("parallel", "arbitrary", vmem_limit_bytes=62 * 1024 * 1024),
        name="ffn",
    )(x, norm, wg, wu, wd, fnorm)
    return tuple(out) if emit_bf16 else out


def _store_by_head(ref, z):
    tokens = z.shape[0]
    for h in range(DA_HEADS):
        ref[pl.ds(h, tokens, stride=DA_HEADS), :] = z[:, h * DA_HEAD_DIM:(h + 1) * DA_HEAD_DIM]


def _proj_in_kernel(x_ref, g_ref, win_ref, wkr_ref, wkrs_ref,
                    qn_ref, kvn_ref, wuqn_ref, wuqr_ref, wuqrs_ref, wukv_ref,
                    c2_ref, s2_ref, c8_ref, s8_ref,
                    q_ref, kf_ref, kb_ref, vf_ref, vb_ref, ckv_ref, kr_ref, krd_ref,
                    qnope_ref, qrope_ref, kv_ref):
    h = _rms(x_ref[...], g_ref[...]).astype(BF16)
    o1, o2, o3 = DA_WIDTH, 2 * DA_WIDTH, 3 * DA_WIDTH
    o4, o5 = o3 + Q_LORA, o3 + Q_LORA + KV_LORA
    q_ref[...] = (_dot(h, win_ref[:, :o1]) * (DA_HALF ** -0.5 * LOG2E)).astype(BF16)
    k = _dot(h, win_ref[:, o1:o2])
    _store_by_head(kf_ref, k)
    kb_ref[...] = k.astype(BF16)
    v = _dot(h, win_ref[:, o2:o3])
    _store_by_head(vf_ref, v)
    vb_ref[...] = v.astype(BF16)
    cq = _rms(_dot(h, win_ref[:, o3:o4]), qn_ref[...]).astype(BF16)
    ckv = _rms(_dot(h, win_ref[:, o4:o5]), kvn_ref[...])
    ckv_ref[...] = ckv
    krd = _dot(h, wkr_ref[...]) * c2_ref[...] + _dot(h, wkrs_ref[...]) * s2_ref[...]
    kr_ref[...] = krd[:, :QK_ROPE]
    krd_ref[...] = krd.astype(BF16)
    mla_scale = QK_HEAD ** -0.5 * LOG2E
    qnope_ref[...] = (_dot(cq, wuqn_ref[...]) * mla_scale).astype(BF16)
    qr = _dot(cq, wuqr_ref[...]) * c8_ref[...] + _dot(cq, wuqrs_ref[...]) * s8_ref[...]
    qrope_ref[...] = (qr * mla_scale).astype(BF16)
    kv_ref[...] = _dot(ckv.astype(BF16), wukv_ref[...]).astype(BF16)


def _proj_in(x, w, tabs, *, tm):
    n, d = x.shape
    tm = min(tm, n)
    assert n % tm == 0
    t_rows = tabs["c2"].shape[0]
    assert t_rows % tm == 0 or tm % t_rows == 0
    if tm > t_rows:
        tm = t_rows
    nt = t_rows // tm

    def row(i):
        return (i, 0)

    def trow(i):
        return (i % nt, 0)

    weights = [w["w_in"], w["w_kr"], w["w_krs"],
               w["q_norm"], w["kv_norm"], w["w_uqn"], w["w_uqr"], w["w_uqrs"], w["w_ukv"]]
    tables = [tabs["c2"], tabs["s2"], tabs["c8"], tabs["s8"]]
    outs = [
        ((n, DA_WIDTH), BF16),
        ((n * DA_HEADS, DA_HEAD_DIM), F32),
        ((n, DA_WIDTH), BF16),
        ((n * DA_HEADS, DA_HEAD_DIM), F32),
        ((n, DA_WIDTH), BF16),
        ((n, KV_LORA), F32),
        ((n, QK_ROPE), F32),
        ((n, LANES), BF16),
        ((n, MLA_HEADS * QK_NOPE), BF16),
        ((n, MLA_HEADS * QK_ROPE), BF16),
        ((n, MLA_HEADS * (QK_NOPE + V_HEAD)), BF16),
    ]
    return pl.pallas_call(
        _proj_in_kernel,
        grid=(n // tm,),
        in_specs=([pl.BlockSpec((tm, d), row), _resident((1, d))]
                  + [_resident(a.shape) for a in weights]
                  + [pl.BlockSpec((tm, a.shape[1]), trow) for a in tables]),
        out_specs=[pl.BlockSpec((tm * s[0] // n, s[1]), row) for s, _ in outs],
        out_shape=[jax.ShapeDtypeStruct(s, dt) for s, dt in outs],
        compiler_params=_params("parallel"),
        name="proj_in",
    )(x, w["mix_norm"], *weights, *tables)


def _lambda(lq1_ref, lk1_ref, lq2_ref, lk2_ref, lam_init):
    a = jnp.sum(lq1_ref[...] * lk1_ref[...], axis=-1, keepdims=True)
    b = jnp.sum(lq2_ref[...] * lk2_ref[...], axis=-1, keepdims=True)
    return jnp.exp(a) - jnp.exp(b) + lam_init


def _pos_bias(qpos, kpos, slope):
    visible = lax.shift_right_arithmetic(kpos, CHUNK_SHIFT) <= lax.shift_right_arithmetic(qpos, CHUNK_SHIFT)
    if slope is None:
        return jnp.where(visible, 0.0, NEG_INF)
    dist = jnp.abs(qpos - kpos).astype(F32)
    return jnp.where(visible, (-LOG2E * slope) * dist, NEG_INF)


def _strip(tq, t, slope):
    qpos = lax.broadcasted_iota(jnp.int32, (tq, t), 0)
    kpos = lax.broadcasted_iota(jnp.int32, (tq, t), 1) - (t - tq)
    return _pos_bias(qpos, kpos, slope)


def _softmax_parts(s):
    m = jnp.max(s, axis=-1, keepdims=True)
    p = jnp.exp2(s - m)
    return p, jnp.sum(p, axis=-1, keepdims=True)


def _da_combine(p1, l1, p2, l2, lam):
    return (p1 - p2 * (lam * l1 / l2)).astype(BF16), 1.0 / l1


def _split_halves(q):
    lane = lax.broadcasted_iota(jnp.int32, q.shape, 1)
    zero = jnp.zeros_like(q)
    return jnp.where(lane < DA_HALF, q, zero), jnp.where(lane >= DA_HALF, q, zero)


def _da_finish(o, subln, lam_init):
    return (_rms(o, subln) * (1.0 - lam_init)).astype(BF16)


def _da_prompt_kernel(q_ref, k_ref, v_ref, slope_ref, lq1_ref, lk1_ref, lq2_ref, lk2_ref, subln_ref,
                      o_ref, strip_ref, *, tq, lam_init):
    t = q_ref.shape[0]
    lam = _lambda(lq1_ref, lk1_ref, lq2_ref, lk2_ref, lam_init)
    strip_ref[...] = _strip(tq, t, slope_ref[0][:, :1])

    def scores(qi):
        ext = (qi + 1) * tq
        q_lo, q_hi = _split_halves(q_ref[qi * tq:ext, :])
        k = k_ref[0:ext, :]
        bias = strip_ref[:, t - ext:t]
        return _dot_nt(q_lo, k) + bias, _dot_nt(q_hi, k) + bias

    nq = t // tq
    s_next = scores(0)
    for qi in range(nq):
        ext = (qi + 1) * tq
        s1, s2 = s_next
        if qi + 1 < nq:
            s_next = scores(qi + 1)
        p1, l1 = _softmax_parts(s1)
        p2, l2 = _softmax_parts(s2)
        a, w = _da_combine(p1, l1, p2, l2, lam)
        o = _dot(a, v_ref[0:ext, :]) * w
        o_ref[qi * tq:ext, :] = _da_finish(o, subln_ref[...], lam_init)


def _da_prompt(q, k, v, slopes, lams, subln, *, t, tq, lam_init):
    n = q.shape[0]
    tq = min(tq, t)
    blk = pl.BlockSpec((t, LANES), lambda b, h: (b, h))
    vec = lambda a: pl.BlockSpec(a.shape, lambda b, h: (0, 0))
    return pl.pallas_call(
        functools.partial(_da_prompt_kernel, tq=tq, lam_init=lam_init),
        grid=(n // t, DA_HEADS),
        in_specs=[blk, blk, blk, pl.BlockSpec((1, 1, LANES), lambda b, h: (h, 0, 0))]
                 + [vec(a) for a in lams] + [vec(subln)],
        out_specs=blk,
        out_shape=jax.ShapeDtypeStruct((n, DA_WIDTH), BF16),
        scratch_shapes=[pltpu.VMEM((tq, t), F32)],
        compiler_params=_params("parallel", "parallel"),
        name="da_prompt",
    )(q, k, v, slopes, *lams, subln)


def _select_rope(qr, h):
    lane = lax.broadcasted_iota(jnp.int32, qr.shape, 1)
    mine = lax.shift_right_logical(lane, CHUNK_SHIFT) == (h % 2)
    return jnp.where(mine, qr, jnp.zeros_like(qr))


def _mla_prompt_kernel(qn_ref, qr_ref, kn_ref, v_ref, krd_ref, o_ref, mask_ref, kfull_ref, *, tq):
    t = qn_ref.shape[0]
    h = pl.program_id(1)
    mask_ref[...] = _strip(tq, tq, None)
    kfull_ref[:, :LANES] = kn_ref[...]
    kfull_ref[:, LANES:] = krd_ref[...]
    def scores(qi):
        lo, ext = qi * tq, (qi + 1) * tq
        q = jnp.concatenate([qn_ref[lo:ext, :], _select_rope(qr_ref[lo:ext, :], h)], axis=1)
        sd = _dot_nt(q, kfull_ref[lo:ext, :]) + mask_ref[...]
        sp = _dot_nt(q, kfull_ref[0:lo, :]) if qi > 0 else None
        return sd, sp

    nq = t // tq
    s_next = scores(0)
    for qi in range(nq):
        lo, ext = qi * tq, (qi + 1) * tq
        sd, sp = s_next
        if qi + 1 < nq:
            s_next = scores(qi + 1)
        md = jnp.max(sd, axis=-1, keepdims=True)
        if qi == 0:
            pd = jnp.exp2(sd - md)
            l = jnp.sum(pd, axis=-1, keepdims=True)
            o = _dot(pd.astype(BF16), v_ref[lo:ext, :])
        else:
            m = jnp.maximum(md, jnp.max(sp, axis=-1, keepdims=True))
            pd = jnp.exp2(sd - m)
            pp = jnp.exp2(sp - m)
            l = jnp.sum(pd, axis=-1, keepdims=True) + jnp.sum(pp, axis=-1, keepdims=True)
            o = _dot(pp.astype(BF16), v_ref[0:lo, :]) + _dot(pd.astype(BF16), v_ref[lo:ext, :])
        o_ref[lo:ext, :] = (o * (1.0 / l)).astype(BF16)


def _mla_prompt(qn, qr, kv, krd, *, t, tq):
    n = qn.shape[0]
    tq = min(tq, t)
    return pl.pallas_call(
        functools.partial(_mla_prompt_kernel, tq=tq),
        grid=(n // t, MLA_HEADS),
        in_specs=[
            pl.BlockSpec((t, LANES), lambda b, h: (b, h)),
            pl.BlockSpec((t, LANES), lambda b, h: (b, h // 2)),
            pl.BlockSpec((t, LANES), lambda b, h: (b, 2 * h)),
            pl.BlockSpec((t, LANES), lambda b, h: (b, 2 * h + 1)),
            pl.BlockSpec((t, LANES), lambda b, h: (b, 0)),
        ],
        out_specs=pl.BlockSpec((t, LANES), lambda b, h: (b, h)),
        out_shape=jax.ShapeDtypeStruct((n, MLA_WIDTH), BF16),
        scratch_shapes=[pltpu.VMEM((tq, tq), F32), pltpu.VMEM((t, 2 * LANES), BF16)],
        compiler_params=_params("parallel", "parallel"),
        name="mla_prompt",
    )(qn, qr, kv, kv, krd)


def _positions(tq, tk, q0, k0):
    qpos = lax.broadcasted_iota(jnp.int32, (tq, tk), 0) + q0
    kpos = lax.broadcasted_iota(jnp.int32, (tq, tk), 1) + k0
    return qpos, kpos


def _da_sample_kernel(q_ref, ck_ref, cv_ref, nk_ref, nv_ref, slope_ref, lq1_ref, lk1_ref, lq2_ref, lk2_ref,
                      subln_ref, o_ref, acc_ref, m_ref, l_ref, *, lam_init):
    c = pl.program_id(1)
    nc = pl.num_programs(1)
    tq = q_ref.shape[0]
    pc = ck_ref.shape[0] // DA_HEADS
    past = pc * nc

    @pl.when(c == 0)
    def _():
        m_ref[...] = jnp.full_like(m_ref, NEG_INF)
        l_ref[...] = jnp.zeros_like(l_ref)
        acc_ref[...] = jnp.zeros_like(acc_ref)

    def attend(keys, values, unit_bias):
        unit2 = jnp.concatenate([unit_bias, unit_bias], axis=0)
        scores = []
        for h in range(DA_HEADS):
            q2 = jnp.concatenate(_split_halves(q_ref[:, h * LANES:(h + 1) * LANES]), axis=0)
            scores.append(_dot_nt(q2, keys(h)) + unit2 * slope_ref[h][:, :1])
        probs = []
        for h, s in enumerate(scores):
            rows = slice(2 * h * tq, (2 * h + 2) * tq)
            m_old = m_ref[rows, :]
            m_new = jnp.maximum(m_old, jnp.max(s, axis=-1, keepdims=True))
            alpha = jnp.exp2(m_old - m_new)
            p = jnp.exp2(s - m_new)
            m_ref[rows, :] = m_new
            l_ref[rows, :] = alpha * l_ref[rows, :] + jnp.sum(p, axis=-1, keepdims=True)
            probs.append((alpha, p.astype(BF16)))
        for h, (alpha, p) in enumerate(probs):
            rows = slice(2 * h * tq, (2 * h + 2) * tq)
            acc_ref[rows, :] = alpha * acc_ref[rows, :] + _dot(p, values(h))

    def head_rows(h):
        return pl.ds(h, pc, stride=DA_HEADS)

    attend(lambda h: ck_ref[head_rows(h), :].astype(BF16), lambda h: cv_ref[head_rows(h), :].astype(BF16),
           _pos_bias(*_positions(tq, pc, past, c * pc), 1.0))

    @pl.when(c == nc - 1)
    def _():
        lam = _lambda(lq1_ref, lk1_ref, lq2_ref, lk2_ref, lam_init)
        attend(lambda h: nk_ref[:, h * LANES:(h + 1) * LANES], lambda h: nv_ref[:, h * LANES:(h + 1) * LANES],
               _pos_bias(*_positions(tq, tq, past, past), 1.0))
        for h in range(DA_HEADS):
            r1 = slice(2 * h * tq, (2 * h + 1) * tq)
            r2 = slice((2 * h + 1) * tq, (2 * h + 2) * tq)
            o = acc_ref[r1, :] / l_ref[r1, :] - acc_ref[r2, :] * (lam / l_ref[r2, :])
            o_ref[:, h * LANES:(h + 1) * LANES] = _da_finish(o, subln_ref[...], lam_init)


def _da_sample(q, cache_k, cache_v, nk, nv, slopes, lams, subln, *, lam_init, pc):
    bd, rows, _ = cache_k.shape
    past = rows // DA_HEADS
    pc = min(pc, past)
    assert past % pc == 0
    n = q.shape[0]
    tq = n // bd
    new = pl.BlockSpec((tq, DA_WIDTH), lambda b, c: (b, 0))
    cache = pl.BlockSpec((None, pc * DA_HEADS, LANES), lambda b, c: (b, c, 0))
    whole = lambda a: pl.BlockSpec(a.shape, lambda b, c: (0,) * a.ndim)
    return pl.pallas_call(
        functools.partial(_da_sample_kernel, lam_init=lam_init),
        grid=(bd, past // pc),
        in_specs=[new, cache, cache, new, new, whole(slopes)] + [whole(a) for a in lams] + [whole(subln)],
        out_specs=new,
        out_shape=jax.ShapeDtypeStruct((n, DA_WIDTH), BF16),
        scratch_shapes=[pltpu.VMEM((2 * DA_HEADS * tq, LANES), F32),
                        pltpu.VMEM((2 * DA_HEADS * tq, 1), F32),
                        pltpu.VMEM((2 * DA_HEADS * tq, 1), F32)],
        compiler_params=_params("parallel", "arbitrary"),
        name="da_sample",
    )(q, cache_k, cache_v, nk, nv, slopes, *lams, subln)


MLA_GROUP = 4


def _mla_sample_kernel(qn_ref, qr_ref, cc_ref, ckrd_ref, nc_ref, nkrd_ref, wukv_ref, o_ref):
    tq = qn_ref.shape[0]
    past = cc_ref.shape[0]
    cc = cc_ref[...].astype(BF16)
    nc = nc_ref[...].astype(BF16)
    rows = MLA_GROUP * tq
    qrow = lax.broadcasted_iota(jnp.int32, (rows, 1), 0) & (tq - 1)
    bias_c = _pos_bias(qrow + past, lax.broadcasted_iota(jnp.int32, (rows, past), 1), None)
    bias_n = _pos_bias(qrow + past, lax.broadcasted_iota(jnp.int32, (rows, tq), 1) + past, None)
    kv_cols = QK_NOPE + V_HEAD
    for g in range(MLA_HEADS // MLA_GROUP):
        heads = range(g * MLA_GROUP, (g + 1) * MLA_GROUP)
        qa = jnp.concatenate(
            [_dot_nt(qn_ref[:, h * QK_NOPE:(h + 1) * QK_NOPE],
                     wukv_ref[:, h * kv_cols:h * kv_cols + QK_NOPE]).astype(BF16) for h in heads], axis=0)
        qr = jnp.concatenate(
            [_select_rope(qr_ref[:, (h // 2) * LANES:(h // 2 + 1) * LANES], h) for h in heads], axis=0)
        sc = _dot_nt(qa, cc) + _dot_nt(qr, ckrd_ref[...]) + bias_c
        sn = _dot_nt(qa, nc) + _dot_nt(qr, nkrd_ref[...]) + bias_n
        m = jnp.maximum(jnp.max(sc, axis=-1, keepdims=True), jnp.max(sn, axis=-1, keepdims=True))
        pc = jnp.exp2(sc - m)
        pn = jnp.exp2(sn - m)
        w = 1.0 / (jnp.sum(pc, axis=-1, keepdims=True) + jnp.sum(pn, axis=-1, keepdims=True))
        lat = ((_dot(pc.astype(BF16), cc) + _dot(pn.astype(BF16), nc)) * w).astype(BF16)
        for j, h in enumerate(heads):
            w_uv = wukv_ref[:, h * kv_cols + QK_NOPE:(h + 1) * kv_cols]
            o_ref[:, h * V_HEAD:(h + 1) * V_HEAD] = _dot(lat[j * tq:(j + 1) * tq, :], w_uv).astype(BF16)


def _mla_sample(qn, qr, cache_ckv, krd_cache, ckv_new, krd_new, w_ukv):
    bd, past, _ = cache_ckv.shape
    n = qn.shape[0]
    tq = n // bd
    assert tq & (tq - 1) == 0
    row = lambda b: (b, 0)
    return pl.pallas_call(
        _mla_sample_kernel,
        grid=(bd,),
        in_specs=[
            pl.BlockSpec((tq, MLA_HEADS * QK_NOPE), row),
            pl.BlockSpec((tq, MLA_HEADS * QK_ROPE), row),
            pl.BlockSpec((None, past, KV_LORA), lambda b: (b, 0, 0)),
            pl.BlockSpec((None, past, LANES), lambda b: (b, 0, 0)),
            pl.BlockSpec((tq, KV_LORA), row),
            pl.BlockSpec((tq, LANES), row),
            _resident(w_ukv.shape),
        ],
        out_specs=pl.BlockSpec((tq, MLA_WIDTH), row),
        out_shape=jax.ShapeDtypeStruct((n, MLA_WIDTH), BF16),
        compiler_params=_params("parallel"),
        name="mla_sample",
    )(qn, qr, cache_ckv, krd_cache, ckv_new, krd_new, w_ukv)


def _proj_out_kernel(x_ref, a_ref, b_ref, w_ref, o_ref):
    o_ref[...] = (x_ref[...] + _dot(a_ref[...], w_ref[:DA_WIDTH, :]) + _dot(b_ref[...], w_ref[DA_WIDTH:, :]))


def _proj_out(x, a_da, a_mla, w_out, *, tm):
    n, d = x.shape
    tm = min(tm, n)
    assert n % tm == 0
    row = lambda i: (i, 0)
    return pl.pallas_call(
        _proj_out_kernel,
        grid=(n // tm,),
        in_specs=[pl.BlockSpec((tm, d), row), pl.BlockSpec((tm, DA_WIDTH), row),
                  pl.BlockSpec((tm, MLA_WIDTH), row), _resident(w_out.shape)],
        out_specs=pl.BlockSpec((tm, d), row),
        out_shape=jax.ShapeDtypeStruct((n, d), F32),
        compiler_params=_params("parallel"),
        name="proj_out",
    )(x, a_da, a_mla, w_out)


def _swap_halves_cols(w):
    half = w.shape[-1] // 2
    return jnp.concatenate([w[..., half:], w[..., :half]], axis=-1)


def _prep_weights(w_in, mix_norm, q_norm, w_uq, kv_norm, w_ukv, w_out):
    w_in = w_in.astype(BF16)
    w_kr = w_in[:, 3 * DA_WIDTH + Q_LORA + KV_LORA:]
    w_uq = w_uq.astype(BF16).reshape(Q_LORA, MLA_HEADS, QK_HEAD)
    w_uqr = w_uq[:, :, QK_NOPE:]
    return {
        "mix_norm": mix_norm[None, :],
        "w_in": w_in,
        "w_kr": jnp.concatenate([w_kr, w_kr], axis=1),
        "w_krs": jnp.concatenate([_swap_halves_cols(w_kr)] * 2, axis=1),
        "q_norm": q_norm[None, :], "kv_norm": kv_norm[None, :],
        "w_uqn": w_uq[:, :, :QK_NOPE].reshape(Q_LORA, MLA_HEADS * QK_NOPE),
        "w_uqr": w_uqr.reshape(Q_LORA, MLA_HEADS * QK_ROPE),
        "w_uqrs": _swap_halves_cols(w_uqr).reshape(Q_LORA, MLA_HEADS * QK_ROPE),
        "w_ukv": w_ukv.astype(BF16),
        "w_out": w_out.astype(BF16),
    }


def _rope_tables(pos):
    half = QK_ROPE // 2
    inv = ROPE_THETA ** (-jnp.arange(half, dtype=F32) / half)
    ang = pos.astype(F32)[:, None] * inv[None, :]
    c = jnp.concatenate([jnp.cos(ang)] * 2, axis=1)
    s = jnp.concatenate([-jnp.sin(ang), jnp.sin(ang)], axis=1)
    return {"c2": jnp.tile(c, (1, 2)), "s2": jnp.tile(s, (1, 2)),
            "c8": jnp.tile(c, (1, MLA_HEADS)), "s8": jnp.tile(s, (1, MLA_HEADS))}


def _layer(x, pos, past, ffn1, ffn2, wmix, slopes, lams, subln, final_norm, lam_init, *, batch):
    n, d = x.shape
    t = n // batch

    def swiglu(x, ffn, *, last):
        casting = ffn[1].dtype == F32
        out = _ffn(x, *ffn, final_norm, final_norm=last, tm=1024, tf=256 if casting else 512)
        return (out[0], (ffn[0],) + out[1:]) if casting else (out, ffn)

    x, ffn1 = swiglu(x, ffn1, last=False)
    tm_in = 256
    tab_pos = pos if t % tm_in == 0 else jnp.tile(pos, batch)
    (q, k_f, k_b, v_f, v_b, ckv, kr, krd, qn, qr, kv) = _proj_in(x, wmix, _rope_tables(tab_pos), tm=tm_in)
    if past is None:
        a_da = _da_prompt(q, k_b, v_b, slopes, lams, subln, t=t, tq=256, lam_init=lam_init)
        a_mla = _mla_prompt(qn, qr, kv, krd, t=t, tq=256)
    else:
        cache_k, cache_v, cache_ckv, cache_kr = past
        p = cache_k.size // (batch * DA_WIDTH)
        a_da = _da_sample(q, cache_k.reshape(batch, p * DA_HEADS, DA_HEAD_DIM),
                          cache_v.reshape(batch, p * DA_HEADS, DA_HEAD_DIM),
                          k_b, v_b, slopes, lams, subln, lam_init=lam_init, pc=2048)
        cache_kr = cache_kr.reshape(batch, p, QK_ROPE)
        krd_cache = jnp.concatenate([cache_kr, cache_kr], axis=-1).astype(BF16)
        a_mla = _mla_sample(qn, qr, cache_ckv.reshape(batch, p, KV_LORA), krd_cache, ckv, krd, wmix["w_ukv"])
    x = _proj_out(x, a_da, a_mla, wmix["w_out"], tm=512)
    y, ffn2 = swiglu(x, ffn2, last=True)
    return y, (k_f, v_f, ckv, kr), ffn1, ffn2


def kernel(x_prompt, x_sample, cache_da_k, cache_da_v, cache_mla_ckv, cache_mla_krope, ffn1_norm, ffn1_w_gate, ffn1_w_up, ffn1_w_down, mix_norm, w_in, da_lambda_q1, da_lambda_k1, da_lambda_q2, da_lambda_k2, da_subln, mla_q_norm, mla_w_uq, mla_kv_norm, mla_w_ukv, w_out, ffn2_norm, ffn2_w_gate, ffn2_w_up, ffn2_w_down, final_norm):
    depth = w_in.shape[0]
    assert depth == 1, "single-layer trunk"
    b, t, d = x_prompt.shape
    bd, td, _ = x_sample.shape
    past_len = cache_da_k.shape[2]
    l = 0
    lam_init = 0.8 - 0.6 * math.exp(-0.3 * l)

    def one_layer(w):
        return w.reshape(w.shape[1:])

    ffn1 = (ffn1_norm[l][None, :], one_layer(ffn1_w_gate), one_layer(ffn1_w_up), one_layer(ffn1_w_down))
    ffn2 = (ffn2_norm[l][None, :], one_layer(ffn2_w_gate), one_layer(ffn2_w_up), one_layer(ffn2_w_down))
    wmix = _prep_weights(w_in[l], mix_norm[l], mla_q_norm[l], mla_w_uq[l], mla_kv_norm[l], mla_w_ukv[l],
                         w_out[l])
    slopes = jnp.asarray(np.broadcast_to(
        np.array([2.0 ** (-8.0 * (i + 1) / DA_HEADS) for i in range(DA_HEADS)], np.float32)[:, None, None],
        (DA_HEADS, 1, LANES)))
    lams = (da_lambda_q1[l][None, :], da_lambda_k1[l][None, :], da_lambda_q2[l][None, :],
            da_lambda_k2[l][None, :])
    subln = da_subln[l][None, :]
    fnorm = final_norm[None, :]

    past = (cache_da_k, cache_da_v, cache_mla_ckv, cache_mla_krope)
    assert bd * td <= 1024, "the running streams must fit one SwiGLU token tile"
    y_s, st_s, ffn1, ffn2 = _layer(x_sample.reshape(bd * td, d), past_len + jnp.arange(td, dtype=jnp.int32),
                                   past, ffn1, ffn2, wmix, slopes, lams, subln, fnorm, lam_init, batch=bd)
    y_p, st_p, _, _ = _layer(x_prompt.reshape(b * t, d), jnp.arange(t, dtype=jnp.int32), None,
                             ffn1, ffn2, wmix, slopes, lams, subln, fnorm, lam_init, batch=b)

    def state(st, nb, nt):
        k_f, v_f, ckv, kr = st
        return (k_f.reshape(1, nb, nt, DA_HEADS, DA_HEAD_DIM), v_f.reshape(1, nb, nt, DA_HEADS, DA_HEAD_DIM),
                ckv.reshape(1, nb, nt, KV_LORA), kr.reshape(1, nb, nt, QK_ROPE))

    return (y_p.reshape(b, t, d), y_s.reshape(bd, td, d)) + state(st_p, b, t) + state(st_s, bd, td)
```

```python
import functools
import math

import jax
import jax.numpy as jnp
import numpy as np
from jax import lax
from jax.experimental import pallas as pl
from jax.experimental.pallas import tpu as pltpu

F32 = jnp.float32
BF16 = jnp.bfloat16

CHUNK = 64
CHUNK_SHIFT = 6
assert 1 << CHUNK_SHIFT == CHUNK
EPS = 1e-6
NEG_INF = -1e30
DA_HEADS = 8
DA_HALF = 64
DA_HEAD_DIM = 2 * DA_HALF
DA_WIDTH = DA_HEADS * DA_HEAD_DIM
MLA_HEADS = 8
Q_LORA = 512
KV_LORA = 256
QK_NOPE = 128
QK_ROPE = 64
QK_HEAD = QK_NOPE + QK_ROPE
V_HEAD = 128
MLA_WIDTH = MLA_HEADS * V_HEAD
ROPE_THETA = 10000.0
LOG2E = math.log2(math.e)
ALIBI_SLOPES = np.array([2.0 ** (-8.0 * (i + 1) / DA_HEADS) for i in range(DA_HEADS)], np.float32)
LANES = 128
assert DA_HEAD_DIM == LANES and QK_NOPE == LANES and V_HEAD == LANES and 2 * QK_ROPE == LANES

VMEM_LIMIT_BYTES = 56 * 1024 * 1024


def _params(*semantics, vmem_limit_bytes=VMEM_LIMIT_BYTES):
    return pltpu.CompilerParams(dimension_semantics=semantics, vmem_limit_bytes=vmem_limit_bytes)


def _rms(x, g):
    return x * lax.rsqrt(jnp.mean(x * x, axis=-1, keepdims=True) + EPS) * g


def _dot(a, b):
    return jnp.dot(a, b, preferred_element_type=F32)


def _dot_nt(a, b):
    return lax.dot_general(a, b, (((1,), (1,)), ((), ())), preferred_element_type=F32)


def _resident(shape):
    return pl.BlockSpec(shape, lambda *_: (0,) * len(shape), pipeline_mode=pl.Buffered(1))


def _ffn_kernel(x_ref, g_ref, wg_ref, wu_ref, wd_ref, fn_ref, o_ref, *rest, final_norm, nj, emit_bf16):
    j = pl.program_id(1)
    xn_ref = rest[-1]

    @pl.when(j == 0)
    def _():
        xn_ref[...] = _rms(x_ref[...], g_ref[...]).astype(BF16)

    def weights():
        if not emit_bf16:
            return wg_ref[...], wu_ref[...], wd_ref[...]
        blocks = []
        for w_ref, wb_ref in zip((wg_ref, wu_ref, wd_ref), rest[:3]):
            wb = w_ref[...].astype(BF16)
            wb_ref[...] = wb
            blocks.append(wb)
        return blocks

    def half_down():
        xn = xn_ref[...]
        wg, wu, wd = weights()
        gate = _dot(xn, wg)
        up = _dot(xn, wu)
        h = (gate * jax.nn.sigmoid(gate) * (0.5 * up)).astype(BF16)
        return _dot(h, wd)

    def finish(y):
        o_ref[...] = _rms(y, fn_ref[...]) if final_norm else y

    if nj == 1:
        finish(x_ref[...] + half_down())
        return

    @pl.when(j == 0)
    def _():
        o_ref[...] = x_ref[...] + half_down()

    @pl.when(jnp.logical_and(j > 0, j < nj - 1))
    def _():
        o_ref[...] += half_down()

    @pl.when(j == nj - 1)
    def _():
        finish(o_ref[...] + half_down())


def _ffn(x, norm, wg, wu, wd, fnorm, *, final_norm, tm, tf):
    n, d = x.shape
    f = wg.shape[1]
    tm = min(tm, n)
    tf = min(tf, f)
    assert n % tm == 0 and f % tf == 0
    emit_bf16 = wg.dtype == F32
    assert not emit_bf16 or n == tm
    col_block = pl.BlockSpec((d, tf), lambda i, j: (0, j))
    row_block = pl.BlockSpec((tf, d), lambda i, j: (j, 0))
    y_spec = pl.BlockSpec((tm, d), lambda i, j: (i, 0))
    y_shape = jax.ShapeDtypeStruct((n, d), F32)
    out = pl.pallas_call(
        functools.partial(_ffn_kernel, final_norm=final_norm, nj=f // tf, emit_bf16=emit_bf16),
        grid=(n // tm, f // tf),
        in_specs=[
            pl.BlockSpec((tm, d), lambda i, j: (i, 0)),
            pl.BlockSpec((1, d), lambda i, j: (0, 0)),
            col_block, col_block, row_block,
            pl.BlockSpec((1, d), lambda i, j: (0, 0)),
        ],
        out_specs=[y_spec, col_block, col_block, row_block] if emit_bf16 else y_spec,
        out_shape=([y_shape] + [jax.ShapeDtypeStruct(w.shape, BF16) for w in (wg, wu, wd)]
                   if emit_bf16 else y_shape),
        scratch_shapes=[pltpu.VMEM((tm, d), BF16)],
        compiler_params=_params("parallel", "arbitrary", vmem_limit_bytes=62 * 1024 * 1024),
        name="ffn",
    )(x, norm, wg, wu, wd, fnorm)
    return tuple(out) if emit_bf16 else out


def _store_by_head(ref, z):
    tokens = z.shape[0]
    for h in range(DA_HEADS):
        ref[pl.ds(h, tokens, stride=DA_HEADS), :] = z[:, h * DA_HEAD_DIM:(h + 1) * DA_HEAD_DIM]


def _proj_in_kernel(x_ref, g_ref, win_ref, wkr_ref, wkrs_ref,
                    qn_ref, kvn_ref, wuqn_ref, wuqr_ref, wuqrs_ref, wukv_ref,
                    c2_ref, s2_ref, c8_ref, s8_ref,
                    q_ref, kf_ref, kb_ref, vf_ref, vb_ref, ckv_ref, kr_ref, krd_ref,
                    qnope_ref, qrope_ref, kv_ref):
    h = _rms(x_ref[...], g_ref[...]).astype(BF16)
    o1, o2, o3 = DA_WIDTH, 2 * DA_WIDTH, 3 * DA_WIDTH
    o4, o5 = o3 + Q_LORA, o3 + Q_LORA + KV_LORA
    q_ref[...] = (_dot(h, win_ref[:, :o1]) * (DA_HALF ** -0.5 * LOG2E)).astype(BF16)
    k = _dot(h, win_ref[:, o1:o2])
    _store_by_head(kf_ref, k)
    kb_ref[...] = k.astype(BF16)
    v = _dot(h, win_ref[:, o2:o3])
    _store_by_head(vf_ref, v)
    vb_ref[...] = v.astype(BF16)
    cq = _rms(_dot(h, win_ref[:, o3:o4]), qn_ref[...]).astype(BF16)
    ckv = _rms(_dot(h, win_ref[:, o4:o5]), kvn_ref[...])
    ckv_ref[...] = ckv
    krd = _dot(h, wkr_ref[...]) * c2_ref[...] + _dot(h, wkrs_ref[...]) * s2_ref[...]
    kr_ref[...] = krd[:, :QK_ROPE]
    krd_ref[...] = krd.astype(BF16)
    mla_scale = QK_HEAD ** -0.5 * LOG2E
    qnope_ref[...] = (_dot(cq, wuqn_ref[...]) * mla_scale).astype(BF16)
    qr = _dot(cq, wuqr_ref[...]) * c8_ref[...] + _dot(cq, wuqrs_ref[...]) * s8_ref[...]
    qrope_ref[...] = (qr * mla_scale).astype(BF16)
    kv_ref[...] = _dot(ckv.astype(BF16), wukv_ref[...]).astype(BF16)


def _proj_in(x, w, tabs, *, tm):
    n, d = x.shape
    tm = min(tm, n)
    assert n % tm == 0
    t_rows = tabs["c2"].shape[0]
    assert t_rows % tm == 0 or tm % t_rows == 0
    if tm > t_rows:
        tm = t_rows
    nt = t_rows // tm

    def row(i):
        return (i, 0)

    def trow(i):
        return (i % nt, 0)

    weights = [w["w_in"], w["w_kr"], w["w_krs"],
               w["q_norm"], w["kv_norm"], w["w_uqn"], w["w_uqr"], w["w_uqrs"], w["w_ukv"]]
    tables = [tabs["c2"], tabs["s2"], tabs["c8"], tabs["s8"]]
    outs = [
        ((n, DA_WIDTH), BF16),
        ((n * DA_HEADS, DA_HEAD_DIM), F32),
        ((n, DA_WIDTH), BF16),
        ((n * DA_HEADS, DA_HEAD_DIM), F32),
        ((n, DA_WIDTH), BF16),
        ((n, KV_LORA), F32),
        ((n, QK_ROPE), F32),
        ((n, LANES), BF16),
        ((n, MLA_HEADS * QK_NOPE), BF16),
        ((n, MLA_HEADS * QK_ROPE), BF16),
        ((n, MLA_HEADS * (QK_NOPE + V_HEAD)), BF16),
    ]
    return pl.pallas_call(
        _proj_in_kernel,
        grid=(n // tm,),
        in_specs=([pl.BlockSpec((tm, d), row), _resident((1, d))]
                  + [_resident(a.shape) for a in weights]
                  + [pl.BlockSpec((tm, a.shape[1]), trow) for a in tables]),
        out_specs=[pl.BlockSpec((tm * s[0] // n, s[1]), row) for s, _ in outs],
        out_shape=[jax.ShapeDtypeStruct(s, dt) for s, dt in outs],
        compiler_params=_params("parallel"),
        name="proj_in",
    )(x, w["mix_norm"], *weights, *tables)


def _lambda(lq1_ref, lk1_ref, lq2_ref, lk2_ref, lam_init):
    a = jnp.sum(lq1_ref[...] * lk1_ref[...], axis=-1, keepdims=True)
    b = jnp.sum(lq2_ref[...] * lk2_ref[...], axis=-1, keepdims=True)
    return jnp.exp(a) - jnp.exp(b) + lam_init


def _pos_bias(qpos, kpos, slope):
    visible = lax.shift_right_arithmetic(kpos, CHUNK_SHIFT) <= lax.shift_right_arithmetic(qpos, CHUNK_SHIFT)
    if slope is None:
        return jnp.where(visible, 0.0, NEG_INF)
    dist = jnp.abs(qpos - kpos).astype(F32)
    return jnp.where(visible, (-LOG2E * slope) * dist, NEG_INF)


def _strip(tq, t, slope):
    qpos = lax.broadcasted_iota(jnp.int32, (tq, t), 0)
    kpos = lax.broadcasted_iota(jnp.int32, (tq, t), 1) - (t - tq)
    return _pos_bias(qpos, kpos, slope)


def _softmax_parts(s):
    m = jnp.max(s, axis=-1, keepdims=True)
    p = jnp.exp2(s - m)
    return p, jnp.sum(p, axis=-1, keepdims=True)


def _da_combine(p1, l1, p2, l2, lam):
    return (p1 - p2 * (lam * l1 / l2)).astype(BF16), 1.0 / l1


def _split_halves(q):
    lane = lax.broadcasted_iota(jnp.int32, q.shape, 1)
    zero = jnp.zeros_like(q)
    return jnp.where(lane < DA_HALF, q, zero), jnp.where(lane >= DA_HALF, q, zero)


def _da_finish(o, subln, lam_init):
    return (_rms(o, subln) * (1.0 - lam_init)).astype(BF16)


BIAS_TERMS = 3


def _da_prompt_kernel(q_ref, k_ref, v_ref, tab_ref, slope_ref, lq1_ref, lk1_ref, lq2_ref, lk2_ref, subln_ref,
                      o_ref, diag_ref, k1_ref, k2_ref, *, tq, lam_init):
    t = q_ref.shape[0]
    lam = _lambda(lq1_ref, lk1_ref, lq2_ref, lk2_ref, lam_init)
    c = LOG2E * slope_ref[0][:, :1]

    lane = lax.broadcasted_iota(jnp.int32, (t, LANES), 1)
    k1_ref[...] = jnp.where(lane < DA_HALF, k_ref[...], tab_ref[0])
    k2_ref[...] = jnp.where(lane >= DA_HALF, k_ref[...], tab_ref[1])

    qpos = lax.broadcasted_iota(jnp.int32, (tq, tq), 0)
    kpos = lax.broadcasted_iota(jnp.int32, (tq, tq), 1)
    visible = lax.shift_right_arithmetic(kpos, CHUNK_SHIFT) <= lax.shift_right_arithmetic(qpos, CHUNK_SHIFT)
    ahead = jnp.maximum(kpos - qpos, 0).astype(F32)
    diag_ref[...] = jnp.where(visible, (-2.0 * c) * ahead, NEG_INF)

    qlane = lax.broadcasted_iota(jnp.int32, (tq, LANES), 1)
    ones_hi = ((qlane >= DA_HALF) & (qlane < DA_HALF + BIAS_TERMS)).astype(F32).astype(BF16)
    ones_lo = (qlane < BIAS_TERMS).astype(F32).astype(BF16)

    def scores(qi):
        lo, ext = qi * tq, (qi + 1) * tq
        q = q_ref[lo:ext, :]
        q1 = jnp.where(qlane < DA_HALF, q, ones_hi)
        q2 = jnp.where(qlane >= DA_HALF, q, ones_lo)
        diag = (_dot_nt(q1, k1_ref[lo:ext, :]) + diag_ref[...], _dot_nt(q2, k2_ref[lo:ext, :]) + diag_ref[...])
        past = (_dot_nt(q1, k1_ref[0:lo, :]), _dot_nt(q2, k2_ref[0:lo, :])) if qi > 0 else None
        return diag, past

    def softmax(sd, sp):
        m = jnp.max(sd, axis=-1, keepdims=True)
        if sp is None:
            pd = jnp.exp2(sd - m)
            return pd, None, jnp.sum(pd, axis=-1, keepdims=True)
        m = jnp.maximum(m, jnp.max(sp, axis=-1, keepdims=True))
        pd = jnp.exp2(sd - m)
        pp = jnp.exp2(sp - m)
        return pd, pp, jnp.sum(pd, axis=-1, keepdims=True) + jnp.sum(pp, axis=-1, keepdims=True)

    nq = t // tq
    s_next = scores(0)
    for qi in range(nq):
        lo, ext = qi * tq, (qi + 1) * tq
        diag, past = s_next
        if qi + 1 < nq:
            s_next = scores(qi + 1)
        pd1, pp1, l1 = softmax(diag[0], None if past is None else past[0])
        pd2, pp2, l2 = softmax(diag[1], None if past is None else past[1])
        r = lam * l1 / l2
        o = _dot((pd1 - pd2 * r).astype(BF16), v_ref[lo:ext, :])
        if past is not None:
            o = o + _dot((pp1 - pp2 * r).astype(BF16), v_ref[0:lo, :])
        o_ref[lo:ext, :] = _da_finish(o * (1.0 / l1), subln_ref[...], lam_init)


def _alibi_key_tables(t):
    x = (np.float32(LOG2E) * ALIBI_SLOPES)[:, None] * np.arange(t, dtype=np.float32)[None, :]
    pieces = []
    for _ in range(BIAS_TERMS):
        piece = x.astype(BF16)
        pieces.append(piece)
        x = x - piece.astype(np.float32)
    terms = jnp.asarray(np.stack(pieces, axis=-1))
    pad = lambda before: jnp.pad(terms, ((0, 0), (0, 0), (before, LANES - BIAS_TERMS - before)))
    return jnp.stack([pad(DA_HALF), pad(0)], axis=1)


def _da_prompt(q, k, v, slopes, lams, subln, *, t, tq, lam_init):
    n = q.shape[0]
    tq = min(tq, t)
    blk = pl.BlockSpec((t, LANES), lambda b, h: (b, h))
    vec = lambda a: pl.BlockSpec(a.shape, lambda b, h: (0, 0))
    return pl.pallas_call(
        functools.partial(_da_prompt_kernel, tq=tq, lam_init=lam_init),
        grid=(n // t, DA_HEADS),
        in_specs=[blk, blk, blk, pl.BlockSpec((None, 2, t, LANES), lambda b, h: (h, 0, 0, 0)),
                  pl.BlockSpec((1, 1, LANES), lambda b, h: (h, 0, 0))]
                 + [vec(a) for a in lams] + [vec(subln)],
        out_specs=blk,
        out_shape=jax.ShapeDtypeStruct((n, DA_WIDTH), BF16),
        scratch_shapes=[pltpu.VMEM((tq, tq), F32), pltpu.VMEM((t, LANES), BF16), pltpu.VMEM((t, LANES), BF16)],
        compiler_params=_params("parallel", "parallel"),
        name="da_prompt",
    )(q, k, v, _alibi_key_tables(t), slopes, *lams, subln)


def _select_rope(qr, h):
    lane = lax.broadcasted_iota(jnp.int32, qr.shape, 1)
    mine = lax.shift_right_logical(lane, CHUNK_SHIFT) == (h % 2)
    return jnp.where(mine, qr, jnp.zeros_like(qr))


def _mla_prompt_kernel(qn_ref, qr_ref, kn_ref, v_ref, krd_ref, o_ref, mask_ref, kfull_ref, *, tq):
    t = qn_ref.shape[0]
    h = pl.program_id(1)
    mask_ref[...] = _strip(tq, tq, None)
    kfull_ref[:, :LANES] = kn_ref[...]
    kfull_ref[:, LANES:] = krd_ref[...]
    def scores(qi):
        lo, ext = qi * tq, (qi + 1) * tq
        q = jnp.concatenate([qn_ref[lo:ext, :], _select_rope(qr_ref[lo:ext, :], h)], axis=1)
        sd = _dot_nt(q, kfull_ref[lo:ext, :]) + mask_ref[...]
        sp = _dot_nt(q, kfull_ref[0:lo, :]) if qi > 0 else None
        return sd, sp

    nq = t // tq
    s_next = scores(0)
    for qi in range(nq):
        lo, ext = qi * tq, (qi + 1) * tq
        sd, sp = s_next
        if qi + 1 < nq:
            s_next = scores(qi + 1)
        md = jnp.max(sd, axis=-1, keepdims=True)
        if qi == 0:
            pd = jnp.exp2(sd - md)
            l = jnp.sum(pd, axis=-1, keepdims=True)
            o = _dot(pd.astype(BF16), v_ref[lo:ext, :])
        else:
            m = jnp.maximum(md, jnp.max(sp, axis=-1, keepdims=True))
            pd = jnp.exp2(sd - m)
            pp = jnp.exp2(sp - m)
            l = jnp.sum(pd, axis=-1, keepdims=True) + jnp.sum(pp, axis=-1, keepdims=True)
            o = _dot(pp.astype(BF16), v_ref[0:lo, :]) + _dot(pd.astype(BF16), v_ref[lo:ext, :])
        o_ref[lo:ext, :] = (o * (1.0 / l)).astype(BF16)


def _mla_prompt(qn, qr, kv, krd, *, t, tq):
    n = qn.shape[0]
    tq = min(tq, t)
    return pl.pallas_call(
        functools.partial(_mla_prompt_kernel, tq=tq),
        grid=(n // t, MLA_HEADS),
        in_specs=[
            pl.BlockSpec((t, LANES), lambda b, h: (b, h)),
            pl.BlockSpec((t, LANES), lambda b, h: (b, h // 2)),
            pl.BlockSpec((t, LANES), lambda b, h: (b, 2 * h)),
            pl.BlockSpec((t, LANES), lambda b, h: (b, 2 * h + 1)),
            pl.BlockSpec((t, LANES), lambda b, h: (b, 0)),
        ],
        out_specs=pl.BlockSpec((t, LANES), lambda b, h: (b, h)),
        out_shape=jax.ShapeDtypeStruct((n, MLA_WIDTH), BF16),
        scratch_shapes=[pltpu.VMEM((tq, tq), F32), pltpu.VMEM((t, 2 * LANES), BF16)],
        compiler_params=_params("parallel", "parallel"),
        name="mla_prompt",
    )(qn, qr, kv, kv, krd)


def _positions(tq, tk, q0, k0):
    qpos = lax.broadcasted_iota(jnp.int32, (tq, tk), 0) + q0
    kpos = lax.broadcasted_iota(jnp.int32, (tq, tk), 1) + k0
    return qpos, kpos


def _da_sample_kernel(q_ref, ck_ref, cv_ref, nk_ref, nv_ref, slope_ref, lq1_ref, lk1_ref, lq2_ref, lk2_ref,
                      subln_ref, o_ref, acc_ref, m_ref, l_ref, *, lam_init):
    c = pl.program_id(1)
    nc = pl.num_programs(1)
    tq = q_ref.shape[0]
    pc = ck_ref.shape[0] // DA_HEADS
    past = pc * nc

    @pl.when(c == 0)
    def _():
        m_ref[...] = jnp.full_like(m_ref, NEG_INF)
        l_ref[...] = jnp.zeros_like(l_ref)
        acc_ref[...] = jnp.zeros_like(acc_ref)

    def attend(keys, values, unit_bias):
        unit2 = jnp.concatenate([unit_bias, unit_bias], axis=0)
        scores = []
        for h in range(DA_HEADS):
            q2 = jnp.concatenate(_split_halves(q_ref[:, h * LANES:(h + 1) * LANES]), axis=0)
            scores.append(_dot_nt(q2, keys(h)) + unit2 * slope_ref[h][:, :1])
        probs = []
        for h, s in enumerate(scores):
            rows = slice(2 * h * tq, (2 * h + 2) * tq)
            m_old = m_ref[rows, :]
            m_new = jnp.maximum(m_old, jnp.max(s, axis=-1, keepdims=True))
            alpha = jnp.exp2(m_old - m_new)
            p = jnp.exp2(s - m_new)
            m_ref[rows, :] = m_new
            l_ref[rows, :] = alpha * l_ref[rows, :] + jnp.sum(p, axis=-1, keepdims=True)
            probs.append((alpha, p.astype(BF16)))
        for h, (alpha, p) in enumerate(probs):
            rows = slice(2 * h * tq, (2 * h + 2) * tq)
            acc_ref[rows, :] = alpha * acc_ref[rows, :] + _dot(p, values(h))

    def head_rows(h):
        return pl.ds(h, pc, stride=DA_HEADS)

    attend(lambda h: ck_ref[head_rows(h), :].astype(BF16), lambda h: cv_ref[head_rows(h), :].astype(BF16),
           _pos_bias(*_positions(tq, pc, past, c * pc), 1.0))

    @pl.when(c == nc - 1)
    def _():
        lam = _lambda(lq1_ref, lk1_ref, lq2_ref, lk2_ref, lam_init)
        attend(lambda h: nk_ref[:, h * LANES:(h + 1) * LANES], lambda h: nv_ref[:, h * LANES:(h + 1) * LANES],
               _pos_bias(*_positions(tq, tq, past, past), 1.0))
        for h in range(DA_HEADS):
            r1 = slice(2 * h * tq, (2 * h + 1) * tq)
            r2 = slice((2 * h + 1) * tq, (2 * h + 2) * tq)
            o = acc_ref[r1, :] / l_ref[r1, :] - acc_ref[r2, :] * (lam / l_ref[r2, :])
            o_ref[:, h * LANES:(h + 1) * LANES] = _da_finish(o, subln_ref[...], lam_init)


def _da_sample(q, cache_k, cache_v, nk, nv, slopes, lams, subln, *, lam_init, pc):
    bd, rows, _ = cache_k.shape
    past = rows // DA_HEADS
    pc = min(pc, past)
    assert past % pc == 0
    n = q.shape[0]
    tq = n // bd
    new = pl.BlockSpec((tq, DA_WIDTH), lambda b, c: (b, 0))
    cache = pl.BlockSpec((None, pc * DA_HEADS, LANES), lambda b, c: (b, c, 0))
    whole = lambda a: pl.BlockSpec(a.shape, lambda b, c: (0,) * a.ndim)
    return pl.pallas_call(
        functools.partial(_da_sample_kernel, lam_init=lam_init),
        grid=(bd, past // pc),
        in_specs=[new, cache, cache, new, new, whole(slopes)] + [whole(a) for a in lams] + [whole(subln)],
        out_specs=new,
        out_shape=jax.ShapeDtypeStruct((n, DA_WIDTH), BF16),
        scratch_shapes=[pltpu.VMEM((2 * DA_HEADS * tq, LANES), F32),
                        pltpu.VMEM((2 * DA_HEADS * tq, 1), F32),
                        pltpu.VMEM((2 * DA_HEADS * tq, 1), F32)],
        compiler_params=_params("parallel", "arbitrary"),
        name="da_sample",
    )(q, cache_k, cache_v, nk, nv, slopes, *lams, subln)


MLA_GROUP = 4


def _mla_sample_kernel(qn_ref, qr_ref, cc_ref, ckrd_ref, nc_ref, nkrd_ref, wukv_ref, o_ref):
    tq = qn_ref.shape[0]
    past = cc_ref.shape[0]
    cc = cc_ref[...].astype(BF16)
    nc = nc_ref[...].astype(BF16)
    rows = MLA_GROUP * tq
    qrow = lax.broadcasted_iota(jnp.int32, (rows, 1), 0) & (tq - 1)
    bias_c = _pos_bias(qrow + past, lax.broadcasted_iota(jnp.int32, (rows, past), 1), None)
    bias_n = _pos_bias(qrow + past, lax.broadcasted_iota(jnp.int32, (rows, tq), 1) + past, None)
    kv_cols = QK_NOPE + V_HEAD
    for g in range(MLA_HEADS // MLA_GROUP):
        heads = range(g * MLA_GROUP, (g + 1) * MLA_GROUP)
        qa = jnp.concatenate(
            [_dot_nt(qn_ref[:, h * QK_NOPE:(h + 1) * QK_NOPE],
                     wukv_ref[:, h * kv_cols:h * kv_cols + QK_NOPE]).astype(BF16) for h in heads], axis=0)
        qr = jnp.concatenate(
            [_select_rope(qr_ref[:, (h // 2) * LANES:(h // 2 + 1) * LANES], h) for h in heads], axis=0)
        sc = _dot_nt(qa, cc) + _dot_nt(qr, ckrd_ref[...]) + bias_c
        sn = _dot_nt(qa, nc) + _dot_nt(qr, nkrd_ref[...]) + bias_n
        m = jnp.maximum(jnp.max(sc, axis=-1, keepdims=True), jnp.max(sn, axis=-1, keepdims=True))
        pc = jnp.exp2(sc - m)
        pn = jnp.exp2(sn - m)
        w = 1.0 / (jnp.sum(pc, axis=-1, keepdims=True) + jnp.sum(pn, axis=-1, keepdims=True))
        lat = ((_dot(pc.astype(BF16), cc) + _dot(pn.astype(BF16), nc)) * w).astype(BF16)
        for j, h in enumerate(heads):
            w_uv = wukv_ref[:, h * kv_cols + QK_NOPE:(h + 1) * kv_cols]
            o_ref[:, h * V_HEAD:(h + 1) * V_HEAD] = _dot(lat[j * tq:(j + 1) * tq, :], w_uv).astype(BF16)


def _mla_sample(qn, qr, cache_ckv, krd_cache, ckv_new, krd_new, w_ukv):
    bd, past, _ = cache_ckv.shape
    n = qn.shape[0]
    tq = n // bd
    assert tq & (tq - 1) == 0
    row = lambda b: (b, 0)
    return pl.pallas_call(
        _mla_sample_kernel,
        grid=(bd,),
        in_specs=[
            pl.BlockSpec((tq, MLA_HEADS * QK_NOPE), row),
            pl.BlockSpec((tq, MLA_HEADS * QK_ROPE), row),
            pl.BlockSpec((None, past, KV_LORA), lambda b: (b, 0, 0)),
            pl.BlockSpec((None, past, LANES), lambda b: (b, 0, 0)),
            pl.BlockSpec((tq, KV_LORA), row),
            pl.BlockSpec((tq, LANES), row),
            _resident(w_ukv.shape),
        ],
        out_specs=pl.BlockSpec((tq, MLA_WIDTH), row),
        out_shape=jax.ShapeDtypeStruct((n, MLA_WIDTH), BF16),
        compiler_params=_params("parallel"),
        name="mla_sample",
    )(qn, qr, cache_ckv, krd_cache, ckv_new, krd_new, w_ukv)


def _proj_out_kernel(x_ref, a_ref, b_ref, w_ref, o_ref):
    o_ref[...] = (x_ref[...] + _dot(a_ref[...], w_ref[:DA_WIDTH, :]) + _dot(b_ref[...], w_ref[DA_WIDTH:, :]))


def _proj_out(x, a_da, a_mla, w_out, *, tm):
    n, d = x.shape
    tm = min(tm, n)
    assert n % tm == 0
    row = lambda i: (i, 0)
    return pl.pallas_call(
        _proj_out_kernel,
        grid=(n // tm,),
        in_specs=[pl.BlockSpec((tm, d), row), pl.BlockSpec((tm, DA_WIDTH), row),
                  pl.BlockSpec((tm, MLA_WIDTH), row), _resident(w_out.shape)],
        out_specs=pl.BlockSpec((tm, d), row),
        out_shape=jax.ShapeDtypeStruct((n, d), F32),
        compiler_params=_params("parallel"),
        name="proj_out",
    )(x, a_da, a_mla, w_out)


def _swap_halves_cols(w):
    half = w.shape[-1] // 2
    return jnp.concatenate([w[..., half:], w[..., :half]], axis=-1)


def _prep_weights(w_in, mix_norm, q_norm, w_uq, kv_norm, w_ukv, w_out):
    w_in = w_in.astype(BF16)
    w_kr = w_in[:, 3 * DA_WIDTH + Q_LORA + KV_LORA:]
    w_uq = w_uq.astype(BF16).reshape(Q_LORA, MLA_HEADS, QK_HEAD)
    w_uqr = w_uq[:, :, QK_NOPE:]
    return {
        "mix_norm": mix_norm[None, :],
        "w_in": w_in,
        "w_kr": jnp.concatenate([w_kr, w_kr], axis=1),
        "w_krs": jnp.concatenate([_swap_halves_cols(w_kr)] * 2, axis=1),
        "q_norm": q_norm[None, :], "kv_norm": kv_norm[None, :],
        "w_uqn": w_uq[:, :, :QK_NOPE].reshape(Q_LORA, MLA_HEADS * QK_NOPE),
        "w_uqr": w_uqr.reshape(Q_LORA, MLA_HEADS * QK_ROPE),
        "w_uqrs": _swap_halves_cols(w_uqr).reshape(Q_LORA, MLA_HEADS * QK_ROPE),
        "w_ukv": w_ukv.astype(BF16),
        "w_out": w_out.astype(BF16),
    }


def _rope_tables(pos):
    half = QK_ROPE // 2
    inv = ROPE_THETA ** (-jnp.arange(half, dtype=F32) / half)
    ang = pos.astype(F32)[:, None] * inv[None, :]
    c = jnp.concatenate([jnp.cos(ang)] * 2, axis=1)
    s = jnp.concatenate([-jnp.sin(ang), jnp.sin(ang)], axis=1)
    return {"c2": jnp.tile(c, (1, 2)), "s2": jnp.tile(s, (1, 2)),
            "c8": jnp.tile(c, (1, MLA_HEADS)), "s8": jnp.tile(s, (1, MLA_HEADS))}


def _layer(x, pos, past, ffn1, ffn2, wmix, slopes, lams, subln, final_norm, lam_init, *, batch):
    n, d = x.shape
    t = n // batch

    def swiglu(x, ffn, *, last):
        casting = ffn[1].dtype == F32
        out = _ffn(x, *ffn, final_norm, final_norm=last, tm=1024, tf=256 if casting else 512)
        return (out[0], (ffn[0],) + out[1:]) if casting else (out, ffn)

    x, ffn1 = swiglu(x, ffn1, last=False)
    tm_in = 256
    tab_pos = pos if t % tm_in == 0 else jnp.tile(pos, batch)
    (q, k_f, k_b, v_f, v_b, ckv, kr, krd, qn, qr, kv) = _proj_in(x, wmix, _rope_tables(tab_pos), tm=tm_in)
    if past is None:
        a_da = _da_prompt(q, k_b, v_b, slopes, lams, subln, t=t, tq=256, lam_init=lam_init)
        a_mla = _mla_prompt(qn, qr, kv, krd, t=t, tq=256)
    else:
        cache_k, cache_v, cache_ckv, cache_kr = past
        p = cache_k.size // (batch * DA_WIDTH)
        a_da = _da_sample(q, cache_k.reshape(batch, p * DA_HEADS, DA_HEAD_DIM),
                          cache_v.reshape(batch, p * DA_HEADS, DA_HEAD_DIM),
                          k_b, v_b, slopes, lams, subln, lam_init=lam_init, pc=2048)
        cache_kr = cache_kr.reshape(batch, p, QK_ROPE)
        krd_cache = jnp.concatenate([cache_kr, cache_kr], axis=-1).astype(BF16)
        a_mla = _mla_sample(qn, qr, cache_ckv.reshape(batch, p, KV_LORA), krd_cache, ckv, krd, wmix["w_ukv"])
    x = _proj_out(x, a_da, a_mla, wmix["w_out"], tm=512)
    y, ffn2 = swiglu(x, ffn2, last=True)
    return y, (k_f, v_f, ckv, kr), ffn1, ffn2


def kernel(x_prompt, x_sample, cache_da_k, cache_da_v, cache_mla_ckv, cache_mla_krope, ffn1_norm, ffn1_w_gate, ffn1_w_up, ffn1_w_down, mix_norm, w_in, da_lambda_q1, da_lambda_k1, da_lambda_q2, da_lambda_k2, da_subln, mla_q_norm, mla_w_uq, mla_kv_norm, mla_w_ukv, w_out, ffn2_norm, ffn2_w_gate, ffn2_w_up, ffn2_w_down, final_norm):
    depth = w_in.shape[0]
    assert depth == 1, "single-layer trunk"
    b, t, d = x_prompt.shape
    bd, td, _ = x_sample.shape
    past_len = cache_da_k.shape[2]
    l = 0
    lam_init = 0.8 - 0.6 * math.exp(-0.3 * l)

    def one_layer(w):
        return w.reshape(w.shape[1:])

    ffn1 = (ffn1_norm[l][None, :], one_layer(ffn1_w_gate), one_layer(ffn1_w_up), one_layer(ffn1_w_down))
    ffn2 = (ffn2_norm[l][None, :], one_layer(ffn2_w_gate), one_layer(ffn2_w_up), one_layer(ffn2_w_down))
    wmix = _prep_weights(w_in[l], mix_norm[l], mla_q_norm[l], mla_w_uq[l], mla_kv_norm[l], mla_w_ukv[l],
                         w_out[l])
    slopes = jnp.asarray(np.broadcast_to(ALIBI_SLOPES[:, None, None], (DA_HEADS, 1, LANES)))
    lams = (da_lambda_q1[l][None, :], da_lambda_k1[l][None, :], da_lambda_q2[l][None, :],
            da_lambda_k2[l][None, :])
    subln = da_subln[l][None, :]
    fnorm = final_norm[None, :]

    past = (cache_da_k, cache_da_v, cache_mla_ckv, cache_mla_krope)
    assert bd * td <= 1024, "the running streams must fit one SwiGLU token tile"
    y_s, st_s, ffn1, ffn2 = _layer(x_sample.reshape(bd * td, d), past_len + jnp.arange(td, dtype=jnp.int32),
                                   past, ffn1, ffn2, wmix, slopes, lams, subln, fnorm, lam_init, batch=bd)
    y_p, st_p, _, _ = _layer(x_prompt.reshape(b * t, d), jnp.arange(t, dtype=jnp.int32), None,
                             ffn1, ffn2, wmix, slopes, lams, subln, fnorm, lam_init, batch=b)

    def state(st, nb, nt):
        k_f, v_f, ckv, kr = st
        return (k_f.reshape(1, nb, nt, DA_HEADS, DA_HEAD_DIM), v_f.reshape(1, nb, nt, DA_HEADS, DA_HEAD_DIM),
                ckv.reshape(1, nb, nt, KV_LORA), kr.reshape(1, nb, nt, QK_ROPE))

    return (y_p.reshape(b, t, d), y_s.reshape(bd, td, d)) + state(st_p, b, t) + state(st_s, bd, td)
```

```python
import functools
import math

import jax
import jax.numpy as jnp
import numpy as np
from jax import lax
from jax.experimental import pallas as pl
from jax.experimental.pallas import tpu as pltpu

F32 = jnp.float32
BF16 = jnp.bfloat16

CHUNK = 64
CHUNK_SHIFT = 6
assert 1 << CHUNK_SHIFT == CHUNK
EPS = 1e-6
NEG_INF = -1e30
DA_HEADS = 8
DA_HALF = 64
DA_HEAD_DIM = 2 * DA_HALF
DA_WIDTH = DA_HEADS * DA_HEAD_DIM
MLA_HEADS = 8
Q_LORA = 512
KV_LORA = 256
QK_NOPE = 128
QK_ROPE = 64
QK_HEAD = QK_NOPE + QK_ROPE
V_HEAD = 128
MLA_WIDTH = MLA_HEADS * V_HEAD
ROPE_THETA = 10000.0
LOG2E = math.log2(math.e)
ALIBI_SLOPES = np.array([2.0 ** (-8.0 * (i + 1) / DA_HEADS) for i in range(DA_HEADS)], np.float32)
LANES = 128
assert DA_HEAD_DIM == LANES and QK_NOPE == LANES and V_HEAD == LANES and 2 * QK_ROPE == LANES

VMEM_LIMIT_BYTES = 56 * 1024 * 1024


def _params(*semantics, vmem_limit_bytes=VMEM_LIMIT_BYTES):
    return pltpu.CompilerParams(dimension_semantics=semantics, vmem_limit_bytes=vmem_limit_bytes)


def _rms(x, g):
    return x * lax.rsqrt(jnp.mean(x * x, axis=-1, keepdims=True) + EPS) * g


def _dot(a, b):
    return jnp.dot(a, b, preferred_element_type=F32)


def _dot_nt(a, b):
    return lax.dot_general(a, b, (((1,), (1,)), ((), ())), preferred_element_type=F32)


def _resident(shape):
    return pl.BlockSpec(shape, lambda *_: (0,) * len(shape), pipeline_mode=pl.Buffered(1))


def _ffn_kernel(x_ref, g_ref, wg_ref, wu_ref, wd_ref, fn_ref, o_ref, *rest, final_norm, nj, emit_bf16):
    j = pl.program_id(1)
    xn_ref = rest[-1]

    @pl.when(j == 0)
    def _():
        xn_ref[...] = _rms(x_ref[...], g_ref[...]).astype(BF16)

    def weights():
        if not emit_bf16:
            return wg_ref[...], wu_ref[...], wd_ref[...]
        blocks = []
        for w_ref, wb_ref in zip((wg_ref, wu_ref, wd_ref), rest[:3]):
            wb = w_ref[...].astype(BF16)
            wb_ref[...] = wb
            blocks.append(wb)
        return blocks

    def half_down():
        xn = xn_ref[...]
        wg, wu, wd = weights()
        gate = _dot(xn, wg)
        up = _dot(xn, wu)
        h = (gate * jax.nn.sigmoid(gate) * (0.5 * up)).astype(BF16)
        return _dot(h, wd)

    def finish(y):
        o_ref[...] = _rms(y, fn_ref[...]) if final_norm else y

    if nj == 1:
        finish(x_ref[...] + half_down())
        return

    @pl.when(j == 0)
    def _():
        o_ref[...] = x_ref[...] + half_down()

    @pl.when(jnp.logical_and(j > 0, j < nj - 1))
    def _():
        o_ref[...] += half_down()

    @pl.when(j == nj - 1)
    def _():
        finish(o_ref[...] + half_down())


def _ffn(x, norm, wg, wu, wd, fnorm, *, final_norm, tm, tf):
    n, d = x.shape
    f = wg.shape[1]
    tm = min(tm, n)
    tf = min(tf, f)
    assert n % tm == 0 and f % tf == 0
    emit_bf16 = wg.dtype == F32
    assert not emit_bf16 or n == tm
    col_block = pl.BlockSpec((d, tf), lambda i, j: (0, j))
    row_block = pl.BlockSpec((tf, d), lambda i, j: (j, 0))
    y_spec = pl.BlockSpec((tm, d), lambda i, j: (i, 0))
    y_shape = jax.ShapeDtypeStruct((n, d), F32)
    out = pl.pallas_call(
        functools.partial(_ffn_kernel, final_norm=final_norm, nj=f // tf, emit_bf16=emit_bf16),
        grid=(n // tm, f // tf),
        in_specs=[
            pl.BlockSpec((tm, d), lambda i, j: (i, 0)),
            pl.BlockSpec((1, d), lambda i, j: (0, 0)),
            col_block, col_block, row_block,
            pl.BlockSpec((1, d), lambda i, j: (0, 0)),
        ],
        out_specs=[y_spec, col_block, col_block, row_block] if emit_bf16 else y_spec,
        out_shape=([y_shape] + [jax.ShapeDtypeStruct(w.shape, BF16) for w in (wg, wu, wd)]
                   if emit_bf16 else y_shape),
        scratch_shapes=[pltpu.VMEM((tm, d), BF16)],
        compiler_params=_params("parallel", "arbitrary", vmem_limit_bytes=62 * 1024 * 1024),
        name="ffn",
    )(x, norm, wg, wu, wd, fnorm)
    return tuple(out) if emit_bf16 else out


def _store_by_head(ref, z):
    tokens = z.shape[0]
    for h in range(DA_HEADS):
        ref[pl.ds(h, tokens, stride=DA_HEADS), :] = z[:, h * DA_HEAD_DIM:(h + 1) * DA_HEAD_DIM]


def _swap_rope_halves(x):
    n = x.shape[1]
    half = QK_ROPE // 2
    lane = lax.broadcasted_iota(jnp.int32, x.shape, 1)
    in_first_half = (lane & (QK_ROPE - 1)) < half
    return jnp.where(in_first_half, pltpu.roll(x, n - half, axis=1), pltpu.roll(x, half, axis=1))


def _proj_in_kernel(x_ref, g_ref, win_ref, wkr_ref,
                    qn_ref, kvn_ref, wuqn_ref, wuqr_ref, wukv_ref,
                    c2_ref, s2_ref, c8_ref, s8_ref,
                    q_ref, kf_ref, kb_ref, vf_ref, vb_ref, ckv_ref, kr_ref, krd_ref,
                    qnope_ref, qrope_ref, kv_ref):
    h = _rms(x_ref[...], g_ref[...]).astype(BF16)
    o1, o2, o3 = DA_WIDTH, 2 * DA_WIDTH, 3 * DA_WIDTH
    o4, o5 = o3 + Q_LORA, o3 + Q_LORA + KV_LORA
    q_ref[...] = (_dot(h, win_ref[:, :o1]) * (DA_HALF ** -0.5 * LOG2E)).astype(BF16)
    k = _dot(h, win_ref[:, o1:o2])
    _store_by_head(kf_ref, k)
    kb_ref[...] = k.astype(BF16)
    v = _dot(h, win_ref[:, o2:o3])
    _store_by_head(vf_ref, v)
    vb_ref[...] = v.astype(BF16)
    cq = _rms(_dot(h, win_ref[:, o3:o4]), qn_ref[...]).astype(BF16)
    ckv = _rms(_dot(h, win_ref[:, o4:o5]), kvn_ref[...])
    ckv_ref[...] = ckv
    kr = _dot(h, wkr_ref[...])
    krd = kr * c2_ref[...] + _swap_rope_halves(kr) * s2_ref[...]
    kr_ref[...] = krd[:, :QK_ROPE]
    krd_ref[...] = krd.astype(BF16)
    mla_scale = QK_HEAD ** -0.5 * LOG2E
    qnope_ref[...] = (_dot(cq, wuqn_ref[...]) * mla_scale).astype(BF16)
    qr = _dot(cq, wuqr_ref[...])
    qr = qr * c8_ref[...] + _swap_rope_halves(qr) * s8_ref[...]
    qrope_ref[...] = (qr * mla_scale).astype(BF16)
    kv_ref[...] = _dot(ckv.astype(BF16), wukv_ref[...]).astype(BF16)


def _proj_in(x, w, tabs, *, tm):
    n, d = x.shape
    tm = min(tm, n)
    assert n % tm == 0
    t_rows = tabs["c2"].shape[0]
    assert t_rows % tm == 0 or tm % t_rows == 0
    if tm > t_rows:
        tm = t_rows
    nt = t_rows // tm

    def row(i):
        return (i, 0)

    def trow(i):
        return (i % nt, 0)

    weights = [w["w_in"], w["w_kr"], w["q_norm"], w["kv_norm"], w["w_uqn"], w["w_uqr"], w["w_ukv"]]
    tables = [tabs["c2"], tabs["s2"], tabs["c8"], tabs["s8"]]
    outs = [
        ((n, DA_WIDTH), BF16),
        ((n * DA_HEADS, DA_HEAD_DIM), F32),
        ((n, DA_WIDTH), BF16),
        ((n * DA_HEADS, DA_HEAD_DIM), F32),
        ((n, DA_WIDTH), BF16),
        ((n, KV_LORA), F32),
        ((n, QK_ROPE), F32),
        ((n, LANES), BF16),
        ((n, MLA_HEADS * QK_NOPE), BF16),
        ((n, MLA_HEADS * QK_ROPE), BF16),
        ((n, MLA_HEADS * (QK_NOPE + V_HEAD)), BF16),
    ]
    return pl.pallas_call(
        _proj_in_kernel,
        grid=(n // tm,),
        in_specs=([pl.BlockSpec((tm, d), row), _resident((1, d))]
                  + [_resident(a.shape) for a in weights]
                  + [pl.BlockSpec((tm, a.shape[1]), trow) for a in tables]),
        out_specs=[pl.BlockSpec((tm * s[0] // n, s[1]), row) for s, _ in outs],
        out_shape=[jax.ShapeDtypeStruct(s, dt) for s, dt in outs],
        compiler_params=_params("parallel"),
        name="proj_in",
    )(x, w["mix_norm"], *weights, *tables)


def _lambda(lq1_ref, lk1_ref, lq2_ref, lk2_ref, lam_init):
    a = jnp.sum(lq1_ref[...] * lk1_ref[...], axis=-1, keepdims=True)
    b = jnp.sum(lq2_ref[...] * lk2_ref[...], axis=-1, keepdims=True)
    return jnp.exp(a) - jnp.exp(b) + lam_init


def _pos_bias(qpos, kpos, slope):
    visible = lax.shift_right_arithmetic(kpos, CHUNK_SHIFT) <= lax.shift_right_arithmetic(qpos, CHUNK_SHIFT)
    if slope is None:
        return jnp.where(visible, 0.0, NEG_INF)
    dist = jnp.abs(qpos - kpos).astype(F32)
    return jnp.where(visible, (-LOG2E * slope) * dist, NEG_INF)


def _strip(tq, t, slope):
    qpos = lax.broadcasted_iota(jnp.int32, (tq, t), 0)
    kpos = lax.broadcasted_iota(jnp.int32, (tq, t), 1) - (t - tq)
    return _pos_bias(qpos, kpos, slope)


def _softmax_parts(s):
    m = jnp.max(s, axis=-1, keepdims=True)
    p = jnp.exp2(s - m)
    return p, jnp.sum(p, axis=-1, keepdims=True)


def _da_combine(p1, l1, p2, l2, lam):
    return (p1 - p2 * (lam * l1 / l2)).astype(BF16), 1.0 / l1


def _split_halves(q):
    lane = lax.broadcasted_iota(jnp.int32, q.shape, 1)
    zero = jnp.zeros_like(q)
    return jnp.where(lane < DA_HALF, q, zero), jnp.where(lane >= DA_HALF, q, zero)


def _da_finish(o, subln, lam_init):
    return (_rms(o, subln) * (1.0 - lam_init)).astype(BF16)


BIAS_TERMS = 3


def _da_prompt_kernel(q_ref, k_ref, v_ref, tab_ref, slope_ref, lq1_ref, lk1_ref, lq2_ref, lk2_ref, subln_ref,
                      o_ref, diag_ref, k1_ref, k2_ref, *, tq, lam_init):
    t = q_ref.shape[0]
    lam = _lambda(lq1_ref, lk1_ref, lq2_ref, lk2_ref, lam_init)
    c = LOG2E * slope_ref[0][:, :1]

    lane = lax.broadcasted_iota(jnp.int32, (t, LANES), 1)
    k1_ref[...] = jnp.where(lane < DA_HALF, k_ref[...], tab_ref[0])
    k2_ref[...] = jnp.where(lane >= DA_HALF, k_ref[...], tab_ref[1])

    qpos = lax.broadcasted_iota(jnp.int32, (tq, tq), 0)
    kpos = lax.broadcasted_iota(jnp.int32, (tq, tq), 1)
    visible = lax.shift_right_arithmetic(kpos, CHUNK_SHIFT) <= lax.shift_right_arithmetic(qpos, CHUNK_SHIFT)
    ahead = jnp.maximum(kpos - qpos, 0).astype(F32)
    diag_ref[...] = jnp.where(visible, (-2.0 * c) * ahead, NEG_INF)

    qlane = lax.broadcasted_iota(jnp.int32, (tq, LANES), 1)
    ones_hi = ((qlane >= DA_HALF) & (qlane < DA_HALF + BIAS_TERMS)).astype(F32).astype(BF16)
    ones_lo = (qlane < BIAS_TERMS).astype(F32).astype(BF16)

    def scores(qi):
        lo, ext = qi * tq, (qi + 1) * tq
        q = q_ref[lo:ext, :]
        q1 = jnp.where(qlane < DA_HALF, q, ones_hi)
        q2 = jnp.where(qlane >= DA_HALF, q, ones_lo)
        diag = (_dot_nt(q1, k1_ref[lo:ext, :]) + diag_ref[...], _dot_nt(q2, k2_ref[lo:ext, :]) + diag_ref[...])
        past = (_dot_nt(q1, k1_ref[0:lo, :]), _dot_nt(q2, k2_ref[0:lo, :])) if qi > 0 else None
        return diag, past

    def softmax(sd, sp):
        m = jnp.max(sd, axis=-1, keepdims=True)
        if sp is None:
            pd = jnp.exp2(sd - m)
            return pd, None, jnp.sum(pd, axis=-1, keepdims=True)
        m = jnp.maximum(m, jnp.max(sp, axis=-1, keepdims=True))
        pd = jnp.exp2(sd - m)
        pp = jnp.exp2(sp - m)
        return pd, pp, jnp.sum(pd, axis=-1, keepdims=True) + jnp.sum(pp, axis=-1, keepdims=True)

    nq = t // tq
    s_next = scores(0)
    for qi in range(nq):
        lo, ext = qi * tq, (qi + 1) * tq
        diag, past = s_next
        if qi + 1 < nq:
            s_next = scores(qi + 1)
        pd1, pp1, l1 = softmax(diag[0], None if past is None else past[0])
        pd2, pp2, l2 = softmax(diag[1], None if past is None else past[1])
        r = lam * l1 / l2
        o = _dot((pd1 - pd2 * r).astype(BF16), v_ref[lo:ext, :])
        if past is not None:
            o = o + _dot((pp1 - pp2 * r).astype(BF16), v_ref[0:lo, :])
        o_ref[lo:ext, :] = _da_finish(o * (1.0 / l1), subln_ref[...], lam_init)


def _alibi_key_tables(t):
    x = (np.float32(LOG2E) * ALIBI_SLOPES)[:, None] * np.arange(t, dtype=np.float32)[None, :]
    pieces = []
    for _ in range(BIAS_TERMS):
        piece = x.astype(BF16)
        pieces.append(piece)
        x = x - piece.astype(np.float32)
    terms = jnp.asarray(np.stack(pieces, axis=-1))
    pad = lambda before: jnp.pad(terms, ((0, 0), (0, 0), (before, LANES - BIAS_TERMS - before)))
    return jnp.stack([pad(DA_HALF), pad(0)], axis=1)


def _da_prompt(q, k, v, slopes, lams, subln, *, t, tq, lam_init):
    n = q.shape[0]
    tq = min(tq, t)
    blk = pl.BlockSpec((t, LANES), lambda b, h: (b, h))
    vec = lambda a: pl.BlockSpec(a.shape, lambda b, h: (0, 0))
    return pl.pallas_call(
        functools.partial(_da_prompt_kernel, tq=tq, lam_init=lam_init),
        grid=(n // t, DA_HEADS),
        in_specs=[blk, blk, blk, pl.BlockSpec((None, 2, t, LANES), lambda b, h: (h, 0, 0, 0)),
                  pl.BlockSpec((1, 1, LANES), lambda b, h: (h, 0, 0))]
                 + [vec(a) for a in lams] + [vec(subln)],
        out_specs=blk,
        out_shape=jax.ShapeDtypeStruct((n, DA_WIDTH), BF16),
        scratch_shapes=[pltpu.VMEM((tq, tq), F32), pltpu.VMEM((t, LANES), BF16), pltpu.VMEM((t, LANES), BF16)],
        compiler_params=_params("parallel", "parallel"),
        name="da_prompt",
    )(q, k, v, _alibi_key_tables(t), slopes, *lams, subln)


def _select_rope(qr, h):
    lane = lax.broadcasted_iota(jnp.int32, qr.shape, 1)
    mine = lax.shift_right_logical(lane, CHUNK_SHIFT) == (h % 2)
    return jnp.where(mine, qr, jnp.zeros_like(qr))


def _mla_prompt_kernel(qn_ref, qr_ref, kn_ref, v_ref, krd_ref, o_ref, mask_ref, kfull_ref, *, tq):
    t = qn_ref.shape[0]
    h = pl.program_id(1)
    mask_ref[...] = _strip(tq, tq, None)
    kfull_ref[:, :LANES] = kn_ref[...]
    kfull_ref[:, LANES:] = krd_ref[...]
    def scores(qi):
        lo, ext = qi * tq, (qi + 1) * tq
        q = jnp.concatenate([qn_ref[lo:ext, :], _select_rope(qr_ref[lo:ext, :], h)], axis=1)
        sd = _dot_nt(q, kfull_ref[lo:ext, :]) + mask_ref[...]
        sp = _dot_nt(q, kfull_ref[0:lo, :]) if qi > 0 else None
        return sd, sp

    def probabilities(sd, sp):
        m = jnp.max(sd, axis=-1, keepdims=True)
        if sp is None:
            pd = jnp.exp2(sd - m)
            return pd.astype(BF16), None, jnp.sum(pd, axis=-1, keepdims=True)
        m = jnp.maximum(m, jnp.max(sp, axis=-1, keepdims=True))
        pd = jnp.exp2(sd - m)
        pp = jnp.exp2(sp - m)
        l = jnp.sum(pd, axis=-1, keepdims=True) + jnp.sum(pp, axis=-1, keepdims=True)
        return pd.astype(BF16), pp.astype(BF16), l

    def weighted_values(qi, pd, pp, l):
        lo, ext = qi * tq, (qi + 1) * tq
        o = _dot(pd, v_ref[lo:ext, :])
        if pp is not None:
            o = o + _dot(pp, v_ref[0:lo, :])
        o_ref[lo:ext, :] = (o * (1.0 / l)).astype(BF16)

    nq = t // tq
    s_next = scores(0)
    pending = None
    for qi in range(nq):
        sd, sp = s_next
        if qi + 1 < nq:
            s_next = scores(qi + 1)
        if pending is not None:
            weighted_values(qi - 1, *pending)
        pending = probabilities(sd, sp)
    weighted_values(nq - 1, *pending)


def _mla_prompt(qn, qr, kv, krd, *, t, tq):
    n = qn.shape[0]
    tq = min(tq, t)
    return pl.pallas_call(
        functools.partial(_mla_prompt_kernel, tq=tq),
        grid=(n // t, MLA_HEADS),
        in_specs=[
            pl.BlockSpec((t, LANES), lambda b, h: (b, h)),
            pl.BlockSpec((t, LANES), lambda b, h: (b, h // 2)),
            pl.BlockSpec((t, LANES), lambda b, h: (b, 2 * h)),
            pl.BlockSpec((t, LANES), lambda b, h: (b, 2 * h + 1)),
            pl.BlockSpec((t, LANES), lambda b, h: (b, 0)),
        ],
        out_specs=pl.BlockSpec((t, LANES), lambda b, h: (b, h)),
        out_shape=jax.ShapeDtypeStruct((n, MLA_WIDTH), BF16),
        scratch_shapes=[pltpu.VMEM((tq, tq), F32), pltpu.VMEM((t, 2 * LANES), BF16)],
        compiler_params=_params("parallel", "parallel"),
        name="mla_prompt",
    )(qn, qr, kv, kv, krd)


def _positions(tq, tk, q0, k0):
    qpos = lax.broadcasted_iota(jnp.int32, (tq, tk), 0) + q0
    kpos = lax.broadcasted_iota(jnp.int32, (tq, tk), 1) + k0
    return qpos, kpos


def _da_sample_kernel(q_ref, ck_ref, cv_ref, nk_ref, nv_ref, slope_ref, lq1_ref, lk1_ref, lq2_ref, lk2_ref,
                      subln_ref, o_ref, acc_ref, m_ref, l_ref, *, lam_init):
    c = pl.program_id(1)
    nc = pl.num_programs(1)
    tq = q_ref.shape[0]
    pc = ck_ref.shape[0] // DA_HEADS
    past = pc * nc

    @pl.when(c == 0)
    def _():
        m_ref[...] = jnp.full_like(m_ref, NEG_INF)
        l_ref[...] = jnp.zeros_like(l_ref)
        acc_ref[...] = jnp.zeros_like(acc_ref)

    def attend(keys, values, unit_bias):
        unit2 = jnp.concatenate([unit_bias, unit_bias], axis=0)
        scores = []
        for h in range(DA_HEADS):
            q2 = jnp.concatenate(_split_halves(q_ref[:, h * LANES:(h + 1) * LANES]), axis=0)
            scores.append(_dot_nt(q2, keys(h)) + unit2 * slope_ref[h][:, :1])
        probs = []
        for h, s in enumerate(scores):
            rows = slice(2 * h * tq, (2 * h + 2) * tq)
            m_old = m_ref[rows, :]
            m_new = jnp.maximum(m_old, jnp.max(s, axis=-1, keepdims=True))
            alpha = jnp.exp2(m_old - m_new)
            p = jnp.exp2(s - m_new)
            m_ref[rows, :] = m_new
            l_ref[rows, :] = alpha * l_ref[rows, :] + jnp.sum(p, axis=-1, keepdims=True)
            probs.append((alpha, p.astype(BF16)))
        for h, (alpha, p) in enumerate(probs):
            rows = slice(2 * h * tq, (2 * h + 2) * tq)
            acc_ref[rows, :] = alpha * acc_ref[rows, :] + _dot(p, values(h))

    def head_rows(h):
        return pl.ds(h, pc, stride=DA_HEADS)

    attend(lambda h: ck_ref[head_rows(h), :].astype(BF16), lambda h: cv_ref[head_rows(h), :].astype(BF16),
           _pos_bias(*_positions(tq, pc, past, c * pc), 1.0))

    @pl.when(c == nc - 1)
    def _():
        lam = _lambda(lq1_ref, lk1_ref, lq2_ref, lk2_ref, lam_init)
        attend(lambda h: nk_ref[:, h * LANES:(h + 1) * LANES], lambda h: nv_ref[:, h * LANES:(h + 1) * LANES],
               _pos_bias(*_positions(tq, tq, past, past), 1.0))
        for h in range(DA_HEADS):
            r1 = slice(2 * h * tq, (2 * h + 1) * tq)
            r2 = slice((2 * h + 1) * tq, (2 * h + 2) * tq)
            o = acc_ref[r1, :] / l_ref[r1, :] - acc_ref[r2, :] * (lam / l_ref[r2, :])
            o_ref[:, h * LANES:(h + 1) * LANES] = _da_finish(o, subln_ref[...], lam_init)


def _da_sample(q, cache_k, cache_v, nk, nv, slopes, lams, subln, *, lam_init, pc):
    bd, rows, _ = cache_k.shape
    past = rows // DA_HEADS
    pc = min(pc, past)
    assert past % pc == 0
    n = q.shape[0]
    tq = n // bd
    new = pl.BlockSpec((tq, DA_WIDTH), lambda b, c: (b, 0))
    cache = pl.BlockSpec((None, pc * DA_HEADS, LANES), lambda b, c: (b, c, 0))
    whole = lambda a: pl.BlockSpec(a.shape, lambda b, c: (0,) * a.ndim)
    return pl.pallas_call(
        functools.partial(_da_sample_kernel, lam_init=lam_init),
        grid=(bd, past // pc),
        in_specs=[new, cache, cache, new, new, whole(slopes)] + [whole(a) for a in lams] + [whole(subln)],
        out_specs=new,
        out_shape=jax.ShapeDtypeStruct((n, DA_WIDTH), BF16),
        scratch_shapes=[pltpu.VMEM((2 * DA_HEADS * tq, LANES), F32),
                        pltpu.VMEM((2 * DA_HEADS * tq, 1), F32),
                        pltpu.VMEM((2 * DA_HEADS * tq, 1), F32)],
        compiler_params=_params("parallel", "arbitrary"),
        name="da_sample",
    )(q, cache_k, cache_v, nk, nv, slopes, *lams, subln)


MLA_GROUP = 4


def _mla_sample_kernel(qn_ref, qr_ref, cc_ref, ckrd_ref, nc_ref, nkrd_ref, wukv_ref, o_ref):
    tq = qn_ref.shape[0]
    past = cc_ref.shape[0]
    cc = cc_ref[...].astype(BF16)
    nc = nc_ref[...].astype(BF16)
    rows = MLA_GROUP * tq
    qrow = lax.broadcasted_iota(jnp.int32, (rows, 1), 0) & (tq - 1)
    bias_c = _pos_bias(qrow + past, lax.broadcasted_iota(jnp.int32, (rows, past), 1), None)
    bias_n = _pos_bias(qrow + past, lax.broadcasted_iota(jnp.int32, (rows, tq), 1) + past, None)
    kv_cols = QK_NOPE + V_HEAD
    for g in range(MLA_HEADS // MLA_GROUP):
        heads = range(g * MLA_GROUP, (g + 1) * MLA_GROUP)
        qa = jnp.concatenate(
            [_dot_nt(qn_ref[:, h * QK_NOPE:(h + 1) * QK_NOPE],
                     wukv_ref[:, h * kv_cols:h * kv_cols + QK_NOPE]).astype(BF16) for h in heads], axis=0)
        qr = jnp.concatenate(
            [_select_rope(qr_ref[:, (h // 2) * LANES:(h // 2 + 1) * LANES], h) for h in heads], axis=0)
        sc = _dot_nt(qa, cc) + _dot_nt(qr, ckrd_ref[...]) + bias_c
        sn = _dot_nt(qa, nc) + _dot_nt(qr, nkrd_ref[...]) + bias_n
        m = jnp.maximum(jnp.max(sc, axis=-1, keepdims=True), jnp.max(sn, axis=-1, keepdims=True))
        pc = jnp.exp2(sc - m)
        pn = jnp.exp2(sn - m)
        w = 1.0 / (jnp.sum(pc, axis=-1, keepdims=True) + jnp.sum(pn, axis=-1, keepdims=True))
        lat = ((_dot(pc.astype(BF16), cc) + _dot(pn.astype(BF16), nc)) * w).astype(BF16)
        for j, h in enumerate(heads):
            w_uv = wukv_ref[:, h * kv_cols + QK_NOPE:(h + 1) * kv_cols]
            o_ref[:, h * V_HEAD:(h + 1) * V_HEAD] = _dot(lat[j * tq:(j + 1) * tq, :], w_uv).astype(BF16)


def _mla_sample(qn, qr, cache_ckv, krd_cache, ckv_new, krd_new, w_ukv):
    bd, past, _ = cache_ckv.shape
    n = qn.shape[0]
    tq = n // bd
    assert tq & (tq - 1) == 0
    row = lambda b: (b, 0)
    return pl.pallas_call(
        _mla_sample_kernel,
        grid=(bd,),
        in_specs=[
            pl.BlockSpec((tq, MLA_HEADS * QK_NOPE), row),
            pl.BlockSpec((tq, MLA_HEADS * QK_ROPE), row),
            pl.BlockSpec((None, past, KV_LORA), lambda b: (b, 0, 0)),
            pl.BlockSpec((None, past, LANES), lambda b: (b, 0, 0)),
            pl.BlockSpec((tq, KV_LORA), row),
            pl.BlockSpec((tq, LANES), row),
            _resident(w_ukv.shape),
        ],
        out_specs=pl.BlockSpec((tq, MLA_WIDTH), row),
        out_shape=jax.ShapeDtypeStruct((n, MLA_WIDTH), BF16),
        compiler_params=_params("parallel"),
        name="mla_sample",
    )(qn, qr, cache_ckv, krd_cache, ckv_new, krd_new, w_ukv)


def _proj_out_kernel(x_ref, a_ref, b_ref, w_ref, o_ref):
    o_ref[...] = (x_ref[...] + _dot(a_ref[...], w_ref[:DA_WIDTH, :]) + _dot(b_ref[...], w_ref[DA_WIDTH:, :]))


def _proj_out(x, a_da, a_mla, w_out, *, tm):
    n, d = x.shape
    tm = min(tm, n)
    assert n % tm == 0
    row = lambda i: (i, 0)
    return pl.pallas_call(
        _proj_out_kernel,
        grid=(n // tm,),
        in_specs=[pl.BlockSpec((tm, d), row), pl.BlockSpec((tm, DA_WIDTH), row),
                  pl.BlockSpec((tm, MLA_WIDTH), row), _resident(w_out.shape)],
        out_specs=pl.BlockSpec((tm, d), row),
        out_shape=jax.ShapeDtypeStruct((n, d), F32),
        compiler_params=_params("parallel"),
        name="proj_out",
    )(x, a_da, a_mla, w_out)


def _prep_weights(w_in, mix_norm, q_norm, w_uq, kv_norm, w_ukv, w_out):
    w_in = w_in.astype(BF16)
    w_kr = w_in[:, 3 * DA_WIDTH + Q_LORA + KV_LORA:]
    w_uq = w_uq.astype(BF16).reshape(Q_LORA, MLA_HEADS, QK_HEAD)
    w_uqr = w_uq[:, :, QK_NOPE:]
    return {
        "mix_norm": mix_norm[None, :],
        "w_in": w_in,
        "w_kr": jnp.concatenate([w_kr, w_kr], axis=1),
        "q_norm": q_norm[None, :], "kv_norm": kv_norm[None, :],
        "w_uqn": w_uq[:, :, :QK_NOPE].reshape(Q_LORA, MLA_HEADS * QK_NOPE),
        "w_uqr": w_uqr.reshape(Q_LORA, MLA_HEADS * QK_ROPE),
        "w_ukv": w_ukv.astype(BF16),
        "w_out": w_out.astype(BF16),
    }


def _rope_tables(pos):
    half = QK_ROPE // 2
    inv = ROPE_THETA ** (-jnp.arange(half, dtype=F32) / half)
    ang = pos.astype(F32)[:, None] * inv[None, :]
    c = jnp.concatenate([jnp.cos(ang)] * 2, axis=1)
    s = jnp.concatenate([-jnp.sin(ang), jnp.sin(ang)], axis=1)
    return {"c2": jnp.tile(c, (1, 2)), "s2": jnp.tile(s, (1, 2)),
            "c8": jnp.tile(c, (1, MLA_HEADS)), "s8": jnp.tile(s, (1, MLA_HEADS))}


def _layer(x, pos, past, ffn1, ffn2, wmix, slopes, lams, subln, final_norm, lam_init, *, batch):
    n, d = x.shape
    t = n // batch

    def swiglu(x, ffn, *, last):
        casting = ffn[1].dtype == F32
        out = _ffn(x, *ffn, final_norm, final_norm=last, tm=1024, tf=256 if casting else 512)
        return (out[0], (ffn[0],) + out[1:]) if casting else (out, ffn)

    x, ffn1 = swiglu(x, ffn1, last=False)
    tm_in = 256
    tab_pos = pos if t % tm_in == 0 else jnp.tile(pos, batch)
    (q, k_f, k_b, v_f, v_b, ckv, kr, krd, qn, qr, kv) = _proj_in(x, wmix, _rope_tables(tab_pos), tm=tm_in)
    if past is None:
        a_da = _da_prompt(q, k_b, v_b, slopes, lams, subln, t=t, tq=256, lam_init=lam_init)
        a_mla = _mla_prompt(qn, qr, kv, krd, t=t, tq=256)
    else:
        cache_k, cache_v, cache_ckv, cache_kr = past
        p = cache_k.size // (batch * DA_WIDTH)
        a_da = _da_sample(q, cache_k.reshape(batch, p * DA_HEADS, DA_HEAD_DIM),
                          cache_v.reshape(batch, p * DA_HEADS, DA_HEAD_DIM),
                          k_b, v_b, slopes, lams, subln, lam_init=lam_init, pc=2048)
        cache_kr = cache_kr.reshape(batch, p, QK_ROPE)
        krd_cache = jnp.concatenate([cache_kr, cache_kr], axis=-1).astype(BF16)
        a_mla = _mla_sample(qn, qr, cache_ckv.reshape(batch, p, KV_LORA), krd_cache, ckv, krd, wmix["w_ukv"])
    x = _proj_out(x, a_da, a_mla, wmix["w_out"], tm=512)
    y, ffn2 = swiglu(x, ffn2, last=True)
    return y, (k_f, v_f, ckv, kr), ffn1, ffn2


def kernel(x_prompt, x_sample, cache_da_k, cache_da_v, cache_mla_ckv, cache_mla_krope, ffn1_norm, ffn1_w_gate, ffn1_w_up, ffn1_w_down, mix_norm, w_in, da_lambda_q1, da_lambda_k1, da_lambda_q2, da_lambda_k2, da_subln, mla_q_norm, mla_w_uq, mla_kv_norm, mla_w_ukv, w_out, ffn2_norm, ffn2_w_gate, ffn2_w_up, ffn2_w_down, final_norm):
    depth = w_in.shape[0]
    assert depth == 1, "single-layer trunk"
    b, t, d = x_prompt.shape
    bd, td, _ = x_sample.shape
    past_len = cache_da_k.shape[2]
    l = 0
    lam_init = 0.8 - 0.6 * math.exp(-0.3 * l)

    def one_layer(w):
        return w.reshape(w.shape[1:])

    ffn1 = (ffn1_norm[l][None, :], one_layer(ffn1_w_gate), one_layer(ffn1_w_up), one_layer(ffn1_w_down))
    ffn2 = (ffn2_norm[l][None, :], one_layer(ffn2_w_gate), one_layer(ffn2_w_up), one_layer(ffn2_w_down))
    wmix = _prep_weights(w_in[l], mix_norm[l], mla_q_norm[l], mla_w_uq[l], mla_kv_norm[l], mla_w_ukv[l],
                         w_out[l])
    slopes = jnp.asarray(np.broadcast_to(ALIBI_SLOPES[:, None, None], (DA_HEADS, 1, LANES)))
    lams = (da_lambda_q1[l][None, :], da_lambda_k1[l][None, :], da_lambda_q2[l][None, :],
            da_lambda_k2[l][None, :])
    subln = da_subln[l][None, :]
    fnorm = final_norm[None, :]

    past = (cache_da_k, cache_da_v, cache_mla_ckv, cache_mla_krope)
    assert bd * td <= 1024, "the running streams must fit one SwiGLU token tile"
    y_s, st_s, ffn1, ffn2 = _layer(x_sample.reshape(bd * td, d), past_len + jnp.arange(td, dtype=jnp.int32),
                                   past, ffn1, ffn2, wmix, slopes, lams, subln, fnorm, lam_init, batch=bd)
    y_p, st_p, _, _ = _layer(x_prompt.reshape(b * t, d), jnp.arange(t, dtype=jnp.int32), None,
                             ffn1, ffn2, wmix, slopes, lams, subln, fnorm, lam_init, batch=b)

    def state(st, nb, nt):
        k_f, v_f, ckv, kr = st
        return (k_f.reshape(1, nb, nt, DA_HEADS, DA_HEAD_DIM), v_f.reshape(1, nb, nt, DA_HEADS, DA_HEAD_DIM),
                ckv.reshape(1, nb, nt, KV_LORA), kr.reshape(1, nb, nt, QK_ROPE))

    return (y_p.reshape(b, t, d), y_s.reshape(bd, td, d)) + state(st_p, b, t) + state(st_s, bd, td)
```

```python
import functools
import math

import jax
import jax.numpy as jnp
import numpy as np
from jax import lax
from jax.experimental import pallas as pl
from jax.experimental.pallas import tpu as pltpu

F32 = jnp.float32
BF16 = jnp.bfloat16

CHUNK = 64
CHUNK_SHIFT = 6
assert 1 << CHUNK_SHIFT == CHUNK
EPS = 1e-6
NEG_INF = -1e30
DA_HEADS = 8
DA_HALF = 64
DA_HEAD_DIM = 2 * DA_HALF
DA_WIDTH = DA_HEADS * DA_HEAD_DIM
MLA_HEADS = 8
Q_LORA = 512
KV_LORA = 256
QK_NOPE = 128
QK_ROPE = 64
QK_HEAD = QK_NOPE + QK_ROPE
V_HEAD = 128
MLA_WIDTH = MLA_HEADS * V_HEAD
ROPE_THETA = 10000.0
LOG2E = math.log2(math.e)
ALIBI_SLOPES = np.array([2.0 ** (-8.0 * (i + 1) / DA_HEADS) for i in range(DA_HEADS)], np.float32)
LANES = 128
assert DA_HEAD_DIM == LANES and QK_NOPE == LANES and V_HEAD == LANES and 2 * QK_ROPE == LANES

VMEM_LIMIT_BYTES = 56 * 1024 * 1024


def _params(*semantics, vmem_limit_bytes=VMEM_LIMIT_BYTES):
    return pltpu.CompilerParams(dimension_semantics=semantics, vmem_limit_bytes=vmem_limit_bytes)


def _rms(x, g):
    return x * lax.rsqrt(jnp.mean(x * x, axis=-1, keepdims=True) + EPS) * g


def _dot(a, b):
    return jnp.dot(a, b, preferred_element_type=F32)


def _dot_nt(a, b):
    return lax.dot_general(a, b, (((1,), (1,)), ((), ())), preferred_element_type=F32)


def _resident(shape):
    return pl.BlockSpec(shape, lambda *_: (0,) * len(shape), pipeline_mode=pl.Buffered(1))


def _ffn_kernel(x_ref, g_ref, wg_ref, wu_ref, wd_ref, fn_ref, o_ref, *rest, final_norm, nj, emit_bf16):
    j = pl.program_id(1)
    xn_ref = rest[-1]

    @pl.when(j == 0)
    def _():
        xn_ref[...] = _rms(x_ref[...], g_ref[...]).astype(BF16)

    def weights():
        if not emit_bf16:
            return wg_ref[...], wu_ref[...], wd_ref[...]
        blocks = []
        for w_ref, wb_ref in zip((wg_ref, wu_ref, wd_ref), rest[:3]):
            wb = w_ref[...].astype(BF16)
            wb_ref[...] = wb
            blocks.append(wb)
        return blocks

    def half_down():
        xn = xn_ref[...]
        wg, wu, wd = weights()
        gate = _dot(xn, wg)
        up = _dot(xn, wu)
        h = (gate * jax.nn.sigmoid(gate) * (0.5 * up)).astype(BF16)
        return _dot(h, wd)

    def finish(y):
        o_ref[...] = _rms(y, fn_ref[...]) if final_norm else y

    if nj == 1:
        finish(x_ref[...] + half_down())
        return

    @pl.when(j == 0)
    def _():
        o_ref[...] = x_ref[...] + half_down()

    @pl.when(jnp.logical_and(j > 0, j < nj - 1))
    def _():
        o_ref[...] += half_down()

    @pl.when(j == nj - 1)
    def _():
        finish(o_ref[...] + half_down())


def _ffn(x, norm, wg, wu, wd, fnorm, *, final_norm, tm, tf):
    n, d = x.shape
    f = wg.shape[1]
    tm = min(tm, n)
    tf = min(tf, f)
    assert n % tm == 0 and f % tf == 0
    emit_bf16 = wg.dtype == F32
    assert not emit_bf16 or n == tm
    col_block = pl.BlockSpec((d, tf), lambda i, j: (0, j))
    row_block = pl.BlockSpec((tf, d), lambda i, j: (j, 0))
    y_spec = pl.BlockSpec((tm, d), lambda i, j: (i, 0))
    y_shape = jax.ShapeDtypeStruct((n, d), F32)
    out = pl.pallas_call(
        functools.partial(_ffn_kernel, final_norm=final_norm, nj=f // tf, emit_bf16=emit_bf16),
        grid=(n // tm, f // tf),
        in_specs=[
            pl.BlockSpec((tm, d), lambda i, j: (i, 0)),
            pl.BlockSpec((1, d), lambda i, j: (0, 0)),
            col_block, col_block, row_block,
            pl.BlockSpec((1, d), lambda i, j: (0, 0)),
        ],
        out_specs=[y_spec, col_block, col_block, row_block] if emit_bf16 else y_spec,
        out_shape=([y_shape] + [jax.ShapeDtypeStruct(w.shape, BF16) for w in (wg, wu, wd)]
                   if emit_bf16 else y_shape),
        scratch_shapes=[pltpu.VMEM((tm, d), BF16)],
        compiler_params=_params("parallel", "arbitrary", vmem_limit_bytes=62 * 1024 * 1024),
        name="ffn",
    )(x, norm, wg, wu, wd, fnorm)
    return tuple(out) if emit_bf16 else out


def _store_by_head(ref, z):
    tokens = z.shape[0]
    for h in range(DA_HEADS):
        ref[pl.ds(h, tokens, stride=DA_HEADS), :] = z[:, h * DA_HEAD_DIM:(h + 1) * DA_HEAD_DIM]


def _swap_rope_halves(x):
    n = x.shape[1]
    half = QK_ROPE // 2
    lane = lax.broadcasted_iota(jnp.int32, x.shape, 1)
    in_first_half = (lane & (QK_ROPE - 1)) < half
    return jnp.where(in_first_half, pltpu.roll(x, n - half, axis=1), pltpu.roll(x, half, axis=1))


def _proj_in_kernel(x_ref, g_ref, win_ref, wkr_ref,
                    qn_ref, kvn_ref, wuqn_ref, wuqr_ref, wukv_ref,
                    c2_ref, s2_ref, c8_ref, s8_ref,
                    q_ref, kf_ref, kb_ref, vf_ref, vb_ref, ckv_ref, kr_ref, krd_ref,
                    qnope_ref, qrope_ref, kv_ref):
    h = _rms(x_ref[...], g_ref[...]).astype(BF16)
    o1, o2, o3 = DA_WIDTH, 2 * DA_WIDTH, 3 * DA_WIDTH
    o4, o5 = o3 + Q_LORA, o3 + Q_LORA + KV_LORA
    q_ref[...] = (_dot(h, win_ref[:, :o1]) * (DA_HALF ** -0.5 * LOG2E)).astype(BF16)
    k = _dot(h, win_ref[:, o1:o2])
    _store_by_head(kf_ref, k)
    kb_ref[...] = k.astype(BF16)
    v = _dot(h, win_ref[:, o2:o3])
    _store_by_head(vf_ref, v)
    vb_ref[...] = v.astype(BF16)
    cq = _rms(_dot(h, win_ref[:, o3:o4]), qn_ref[...]).astype(BF16)
    ckv = _rms(_dot(h, win_ref[:, o4:o5]), kvn_ref[...])
    ckv_ref[...] = ckv
    kr = _dot(h, wkr_ref[...])
    krd = kr * c2_ref[...] + _swap_rope_halves(kr) * s2_ref[...]
    kr_ref[...] = krd[:, :QK_ROPE]
    krd_ref[...] = krd.astype(BF16)
    mla_scale = QK_HEAD ** -0.5 * LOG2E
    qnope_ref[...] = (_dot(cq, wuqn_ref[...]) * mla_scale).astype(BF16)
    qr = _dot(cq, wuqr_ref[...])
    qr = qr * c8_ref[...] + _swap_rope_halves(qr) * s8_ref[...]
    qrope_ref[...] = (qr * mla_scale).astype(BF16)
    kv_ref[...] = _dot(ckv.astype(BF16), wukv_ref[...]).astype(BF16)


def _proj_in(x, w, tabs, *, tm):
    n, d = x.shape
    tm = min(tm, n)
    assert n % tm == 0
    t_rows = tabs["c2"].shape[0]
    assert t_rows % tm == 0 or tm % t_rows == 0
    if tm > t_rows:
        tm = t_rows
    nt = t_rows // tm

    def row(i):
        return (i, 0)

    def trow(i):
        return (i % nt, 0)

    weights = [w["w_in"], w["w_kr"], w["q_norm"], w["kv_norm"], w["w_uqn"], w["w_uqr"], w["w_ukv"]]
    tables = [tabs["c2"], tabs["s2"], tabs["c8"], tabs["s8"]]
    outs = [
        ((n, DA_WIDTH), BF16),
        ((n * DA_HEADS, DA_HEAD_DIM), F32),
        ((n, DA_WIDTH), BF16),
        ((n * DA_HEADS, DA_HEAD_DIM), F32),
        ((n, DA_WIDTH), BF16),
        ((n, KV_LORA), F32),
        ((n, QK_ROPE), F32),
        ((n, LANES), BF16),
        ((n, MLA_HEADS * QK_NOPE), BF16),
        ((n, MLA_HEADS * QK_ROPE), BF16),
        ((n, MLA_HEADS * (QK_NOPE + V_HEAD)), BF16),
    ]
    return pl.pallas_call(
        _proj_in_kernel,
        grid=(n // tm,),
        in_specs=([pl.BlockSpec((tm, d), row), _resident((1, d))]
                  + [_resident(a.shape) for a in weights]
                  + [pl.BlockSpec((tm, a.shape[1]), trow) for a in tables]),
        out_specs=[pl.BlockSpec((tm * s[0] // n, s[1]), row) for s, _ in outs],
        out_shape=[jax.ShapeDtypeStruct(s, dt) for s, dt in outs],
        compiler_params=_params("parallel"),
        name="proj_in",
    )(x, w["mix_norm"], *weights, *tables)


def _lambda(lq1_ref, lk1_ref, lq2_ref, lk2_ref, lam_init):
    a = jnp.sum(lq1_ref[...] * lk1_ref[...], axis=-1, keepdims=True)
    b = jnp.sum(lq2_ref[...] * lk2_ref[...], axis=-1, keepdims=True)
    return jnp.exp(a) - jnp.exp(b) + lam_init


def _pos_bias(qpos, kpos, slope):
    visible = lax.shift_right_arithmetic(kpos, CHUNK_SHIFT) <= lax.shift_right_arithmetic(qpos, CHUNK_SHIFT)
    if slope is None:
        return jnp.where(visible, 0.0, NEG_INF)
    dist = jnp.abs(qpos - kpos).astype(F32)
    return jnp.where(visible, (-LOG2E * slope) * dist, NEG_INF)


def _strip(tq, t, slope):
    qpos = lax.broadcasted_iota(jnp.int32, (tq, t), 0)
    kpos = lax.broadcasted_iota(jnp.int32, (tq, t), 1) - (t - tq)
    return _pos_bias(qpos, kpos, slope)


def _softmax_parts(s):
    m = jnp.max(s, axis=-1, keepdims=True)
    p = jnp.exp2(s - m)
    return p, jnp.sum(p, axis=-1, keepdims=True)


def _da_combine(p1, l1, p2, l2, lam):
    return (p1 - p2 * (lam * l1 / l2)).astype(BF16), 1.0 / l1


def _split_halves(q):
    lane = lax.broadcasted_iota(jnp.int32, q.shape, 1)
    zero = jnp.zeros_like(q)
    return jnp.where(lane < DA_HALF, q, zero), jnp.where(lane >= DA_HALF, q, zero)


def _da_finish(o, subln, lam_init):
    return (_rms(o, subln) * (1.0 - lam_init)).astype(BF16)


BIAS_TERMS = 3


def _da_prompt_kernel(q_ref, k_ref, v_ref, tab_ref, slope_ref, lq1_ref, lk1_ref, lq2_ref, lk2_ref, subln_ref,
                      o_ref, diag_ref, k1_ref, k2_ref, *, tq, lam_init):
    t = q_ref.shape[0]
    lam = _lambda(lq1_ref, lk1_ref, lq2_ref, lk2_ref, lam_init)
    c = LOG2E * slope_ref[0][:, :1]

    lane = lax.broadcasted_iota(jnp.int32, (t, LANES), 1)
    k1_ref[...] = jnp.where(lane < DA_HALF, k_ref[...], tab_ref[0])
    k2_ref[...] = jnp.where(lane >= DA_HALF, k_ref[...], tab_ref[1])

    qpos = lax.broadcasted_iota(jnp.int32, (tq, tq), 0)
    kpos = lax.broadcasted_iota(jnp.int32, (tq, tq), 1)
    visible = lax.shift_right_arithmetic(kpos, CHUNK_SHIFT) <= lax.shift_right_arithmetic(qpos, CHUNK_SHIFT)
    ahead = jnp.maximum(kpos - qpos, 0).astype(F32)
    diag_ref[...] = jnp.where(visible, (-2.0 * c) * ahead, NEG_INF)

    qlane = lax.broadcasted_iota(jnp.int32, (tq, LANES), 1)
    ones_hi = ((qlane >= DA_HALF) & (qlane < DA_HALF + BIAS_TERMS)).astype(F32).astype(BF16)
    ones_lo = (qlane < BIAS_TERMS).astype(F32).astype(BF16)

    def scores(qi):
        lo, ext = qi * tq, (qi + 1) * tq
        q = q_ref[lo:ext, :]
        q1 = jnp.where(qlane < DA_HALF, q, ones_hi)
        q2 = jnp.where(qlane >= DA_HALF, q, ones_lo)
        diag = (_dot_nt(q1, k1_ref[lo:ext, :]) + diag_ref[...], _dot_nt(q2, k2_ref[lo:ext, :]) + diag_ref[...])
        past = (_dot_nt(q1, k1_ref[0:lo, :]), _dot_nt(q2, k2_ref[0:lo, :])) if qi > 0 else None
        return diag, past

    def softmax(sd, sp):
        m = jnp.max(sd, axis=-1, keepdims=True)
        if sp is None:
            pd = jnp.exp2(sd - m)
            return pd, None, jnp.sum(pd, axis=-1, keepdims=True)
        m = jnp.maximum(m, jnp.max(sp, axis=-1, keepdims=True))
        pd = jnp.exp2(sd - m)
        pp = jnp.exp2(sp - m)
        return pd, pp, jnp.sum(pd, axis=-1, keepdims=True) + jnp.sum(pp, axis=-1, keepdims=True)

    nq = t // tq
    s_next = scores(0)
    for qi in range(nq):
        lo, ext = qi * tq, (qi + 1) * tq
        diag, past = s_next
        if qi + 1 < nq:
            s_next = scores(qi + 1)
        pd1, pp1, l1 = softmax(diag[0], None if past is None else past[0])
        pd2, pp2, l2 = softmax(diag[1], None if past is None else past[1])
        r = lam * l1 / l2
        o = _dot((pd1 - pd2 * r).astype(BF16), v_ref[lo:ext, :])
        if past is not None:
            o = o + _dot((pp1 - pp2 * r).astype(BF16), v_ref[0:lo, :])
        o_ref[lo:ext, :] = _da_finish(o * (1.0 / l1), subln_ref[...], lam_init)


def _alibi_key_tables(t):
    x = (np.float32(LOG2E) * ALIBI_SLOPES)[:, None] * np.arange(t, dtype=np.float32)[None, :]
    pieces = []
    for _ in range(BIAS_TERMS):
        piece = x.astype(BF16)
        pieces.append(piece)
        x = x - piece.astype(np.float32)
    terms = jnp.asarray(np.stack(pieces, axis=-1))
    pad = lambda before: jnp.pad(terms, ((0, 0), (0, 0), (before, LANES - BIAS_TERMS - before)))
    return jnp.stack([pad(DA_HALF), pad(0)], axis=1)


def _da_prompt(q, k, v, slopes, lams, subln, *, t, tq, lam_init):
    n = q.shape[0]
    tq = min(tq, t)
    blk = pl.BlockSpec((t, LANES), lambda b, h: (b, h))
    vec = lambda a: pl.BlockSpec(a.shape, lambda b, h: (0, 0))
    return pl.pallas_call(
        functools.partial(_da_prompt_kernel, tq=tq, lam_init=lam_init),
        grid=(n // t, DA_HEADS),
        in_specs=[blk, blk, blk, pl.BlockSpec((None, 2, t, LANES), lambda b, h: (h, 0, 0, 0)),
                  pl.BlockSpec((1, 1, LANES), lambda b, h: (h, 0, 0))]
                 + [vec(a) for a in lams] + [vec(subln)],
        out_specs=blk,
        out_shape=jax.ShapeDtypeStruct((n, DA_WIDTH), BF16),
        scratch_shapes=[pltpu.VMEM((tq, tq), F32), pltpu.VMEM((t, LANES), BF16), pltpu.VMEM((t, LANES), BF16)],
        compiler_params=_params("parallel", "parallel"),
        name="da_prompt",
    )(q, k, v, _alibi_key_tables(t), slopes, *lams, subln)


def _select_rope(qr, h):
    lane = lax.broadcasted_iota(jnp.int32, qr.shape, 1)
    mine = lax.shift_right_logical(lane, CHUNK_SHIFT) == (h % 2)
    return jnp.where(mine, qr, jnp.zeros_like(qr))


MLA_PAIR = 2


def _mla_prompt_kernel(qn_ref, qr_ref, kv_ref, krd_ref, o_ref, mask_ref, kfull_ref, *, tq):
    t = qn_ref.shape[0]
    kv_cols = QK_NOPE + V_HEAD
    mask_ref[...] = _strip(tq, tq, None)
    for hh in range(MLA_PAIR):
        kfull_ref[hh, :, :LANES] = kv_ref[:, hh * kv_cols:hh * kv_cols + QK_NOPE]
        kfull_ref[hh, :, LANES:] = krd_ref[...]

    def scores(hh, qi):
        lo, ext = qi * tq, (qi + 1) * tq
        rope_block = qr_ref[lo:ext, (hh // 2) * LANES:(hh // 2 + 1) * LANES]
        q = jnp.concatenate([qn_ref[lo:ext, hh * LANES:(hh + 1) * LANES], _select_rope(rope_block, hh)], axis=1)
        sd = _dot_nt(q, kfull_ref[hh, lo:ext, :]) + mask_ref[...]
        sp = _dot_nt(q, kfull_ref[hh, 0:lo, :]) if qi > 0 else None
        return sd, sp

    def probabilities(sd, sp):
        m = jnp.max(sd, axis=-1, keepdims=True)
        if sp is None:
            pd = jnp.exp2(sd - m)
            return pd.astype(BF16), None, jnp.sum(pd, axis=-1, keepdims=True)
        m = jnp.maximum(m, jnp.max(sp, axis=-1, keepdims=True))
        pd = jnp.exp2(sd - m)
        pp = jnp.exp2(sp - m)
        l = jnp.sum(pd, axis=-1, keepdims=True) + jnp.sum(pp, axis=-1, keepdims=True)
        return pd.astype(BF16), pp.astype(BF16), l

    def weighted_values(hh, qi, pd, pp, l):
        lo, ext = qi * tq, (qi + 1) * tq
        v_cols = slice(hh * kv_cols + QK_NOPE, (hh + 1) * kv_cols)
        o = _dot(pd, kv_ref[lo:ext, v_cols])
        if pp is not None:
            o = o + _dot(pp, kv_ref[0:lo, v_cols])
        o_ref[lo:ext, hh * LANES:(hh + 1) * LANES] = (o * (1.0 / l)).astype(BF16)

    tiles = [(hh, qi) for hh in range(MLA_PAIR) for qi in range(t // tq)]
    s_next = scores(*tiles[0])
    pending = None
    for i, tile in enumerate(tiles):
        sd, sp = s_next
        if i + 1 < len(tiles):
            s_next = scores(*tiles[i + 1])
        if pending is not None:
            weighted_values(*tiles[i - 1], *pending)
        pending = probabilities(sd, sp)
    weighted_values(*tiles[-1], *pending)


def _mla_prompt(qn, qr, kv, krd, *, t, tq):
    n = qn.shape[0]
    tq = min(tq, t)
    pair = lambda width: pl.BlockSpec((t, MLA_PAIR * width), lambda b, g: (b, g))
    return pl.pallas_call(
        functools.partial(_mla_prompt_kernel, tq=tq),
        grid=(n // t, MLA_HEADS // MLA_PAIR),
        in_specs=[pair(QK_NOPE), pair(QK_ROPE), pair(QK_NOPE + V_HEAD),
                  pl.BlockSpec((t, LANES), lambda b, g: (b, 0))],
        out_specs=pair(V_HEAD),
        out_shape=jax.ShapeDtypeStruct((n, MLA_WIDTH), BF16),
        scratch_shapes=[pltpu.VMEM((tq, tq), F32), pltpu.VMEM((MLA_PAIR, t, 2 * LANES), BF16)],
        compiler_params=_params("parallel", "parallel"),
        name="mla_prompt",
    )(qn, qr, kv, krd)


def _positions(tq, tk, q0, k0):
    qpos = lax.broadcasted_iota(jnp.int32, (tq, tk), 0) + q0
    kpos = lax.broadcasted_iota(jnp.int32, (tq, tk), 1) + k0
    return qpos, kpos


def _da_sample_kernel(q_ref, ck_ref, cv_ref, nk_ref, nv_ref, slope_ref, lq1_ref, lk1_ref, lq2_ref, lk2_ref,
                      subln_ref, o_ref, acc_ref, m_ref, l_ref, *, lam_init):
    c = pl.program_id(1)
    nc = pl.num_programs(1)
    tq = q_ref.shape[0]
    pc = ck_ref.shape[0] // DA_HEADS
    past = pc * nc

    @pl.when(c == 0)
    def _():
        m_ref[...] = jnp.full_like(m_ref, NEG_INF)
        l_ref[...] = jnp.zeros_like(l_ref)
        acc_ref[...] = jnp.zeros_like(acc_ref)

    def attend(keys, values, unit_bias):
        unit2 = jnp.concatenate([unit_bias, unit_bias], axis=0)
        scores = []
        for h in range(DA_HEADS):
            q2 = jnp.concatenate(_split_halves(q_ref[:, h * LANES:(h + 1) * LANES]), axis=0)
            scores.append(_dot_nt(q2, keys(h)) + unit2 * slope_ref[h][:, :1])
        probs = []
        for h, s in enumerate(scores):
            rows = slice(2 * h * tq, (2 * h + 2) * tq)
            m_old = m_ref[rows, :]
            m_new = jnp.maximum(m_old, jnp.max(s, axis=-1, keepdims=True))
            alpha = jnp.exp2(m_old - m_new)
            p = jnp.exp2(s - m_new)
            m_ref[rows, :] = m_new
            l_ref[rows, :] = alpha * l_ref[rows, :] + jnp.sum(p, axis=-1, keepdims=True)
            probs.append((alpha, p.astype(BF16)))
        for h, (alpha, p) in enumerate(probs):
            rows = slice(2 * h * tq, (2 * h + 2) * tq)
            acc_ref[rows, :] = alpha * acc_ref[rows, :] + _dot(p, values(h))

    def head_rows(h):
        return pl.ds(h, pc, stride=DA_HEADS)

    attend(lambda h: ck_ref[head_rows(h), :].astype(BF16), lambda h: cv_ref[head_rows(h), :].astype(BF16),
           _pos_bias(*_positions(tq, pc, past, c * pc), 1.0))

    @pl.when(c == nc - 1)
    def _():
        lam = _lambda(lq1_ref, lk1_ref, lq2_ref, lk2_ref, lam_init)
        attend(lambda h: nk_ref[:, h * LANES:(h + 1) * LANES], lambda h: nv_ref[:, h * LANES:(h + 1) * LANES],
               _pos_bias(*_positions(tq, tq, past, past), 1.0))
        for h in range(DA_HEADS):
            r1 = slice(2 * h * tq, (2 * h + 1) * tq)
            r2 = slice((2 * h + 1) * tq, (2 * h + 2) * tq)
            o = acc_ref[r1, :] / l_ref[r1, :] - acc_ref[r2, :] * (lam / l_ref[r2, :])
            o_ref[:, h * LANES:(h + 1) * LANES] = _da_finish(o, subln_ref[...], lam_init)


def _da_sample(q, cache_k, cache_v, nk, nv, slopes, lams, subln, *, lam_init, pc):
    bd, rows, _ = cache_k.shape
    past = rows // DA_HEADS
    pc = min(pc, past)
    assert past % pc == 0
    n = q.shape[0]
    tq = n // bd
    new = pl.BlockSpec((tq, DA_WIDTH), lambda b, c: (b, 0))
    cache = pl.BlockSpec((None, pc * DA_HEADS, LANES), lambda b, c: (b, c, 0))
    whole = lambda a: pl.BlockSpec(a.shape, lambda b, c: (0,) * a.ndim)
    return pl.pallas_call(
        functools.partial(_da_sample_kernel, lam_init=lam_init),
        grid=(bd, past // pc),
        in_specs=[new, cache, cache, new, new, whole(slopes)] + [whole(a) for a in lams] + [whole(subln)],
        out_specs=new,
        out_shape=jax.ShapeDtypeStruct((n, DA_WIDTH), BF16),
        scratch_shapes=[pltpu.VMEM((2 * DA_HEADS * tq, LANES), F32),
                        pltpu.VMEM((2 * DA_HEADS * tq, 1), F32),
                        pltpu.VMEM((2 * DA_HEADS * tq, 1), F32)],
        compiler_params=_params("parallel", "arbitrary"),
        name="da_sample",
    )(q, cache_k, cache_v, nk, nv, slopes, *lams, subln)


MLA_GROUP = 4


def _mla_sample_kernel(qn_ref, qr_ref, cc_ref, ckrd_ref, nc_ref, nkrd_ref, wukv_ref, o_ref):
    tq = qn_ref.shape[0]
    past = cc_ref.shape[0]
    cc = cc_ref[...].astype(BF16)
    nc = nc_ref[...].astype(BF16)
    rows = MLA_GROUP * tq
    qrow = lax.broadcasted_iota(jnp.int32, (rows, 1), 0) & (tq - 1)
    bias_c = _pos_bias(qrow + past, lax.broadcasted_iota(jnp.int32, (rows, past), 1), None)
    bias_n = _pos_bias(qrow + past, lax.broadcasted_iota(jnp.int32, (rows, tq), 1) + past, None)
    kv_cols = QK_NOPE + V_HEAD
    for g in range(MLA_HEADS // MLA_GROUP):
        heads = range(g * MLA_GROUP, (g + 1) * MLA_GROUP)
        qa = jnp.concatenate(
            [_dot_nt(qn_ref[:, h * QK_NOPE:(h + 1) * QK_NOPE],
                     wukv_ref[:, h * kv_cols:h * kv_cols + QK_NOPE]).astype(BF16) for h in heads], axis=0)
        qr = jnp.concatenate(
            [_select_rope(qr_ref[:, (h // 2) * LANES:(h // 2 + 1) * LANES], h) for h in heads], axis=0)
        sc = _dot_nt(qa, cc) + _dot_nt(qr, ckrd_ref[...]) + bias_c
        sn = _dot_nt(qa, nc) + _dot_nt(qr, nkrd_ref[...]) + bias_n
        m = jnp.maximum(jnp.max(sc, axis=-1, keepdims=True), jnp.max(sn, axis=-1, keepdims=True))
        pc = jnp.exp2(sc - m)
        pn = jnp.exp2(sn - m)
        w = 1.0 / (jnp.sum(pc, axis=-1, keepdims=True) + jnp.sum(pn, axis=-1, keepdims=True))
        lat = ((_dot(pc.astype(BF16), cc) + _dot(pn.astype(BF16), nc)) * w).astype(BF16)
        for j, h in enumerate(heads):
            w_uv = wukv_ref[:, h * kv_cols + QK_NOPE:(h + 1) * kv_cols]
            o_ref[:, h * V_HEAD:(h + 1) * V_HEAD] = _dot(lat[j * tq:(j + 1) * tq, :], w_uv).astype(BF16)


def _mla_sample(qn, qr, cache_ckv, krd_cache, ckv_new, krd_new, w_ukv):
    bd, past, _ = cache_ckv.shape
    n = qn.shape[0]
    tq = n // bd
    assert tq & (tq - 1) == 0
    row = lambda b: (b, 0)
    return pl.pallas_call(
        _mla_sample_kernel,
        grid=(bd,),
        in_specs=[
            pl.BlockSpec((tq, MLA_HEADS * QK_NOPE), row),
            pl.BlockSpec((tq, MLA_HEADS * QK_ROPE), row),
            pl.BlockSpec((None, past, KV_LORA), lambda b: (b, 0, 0)),
            pl.BlockSpec((None, past, LANES), lambda b: (b, 0, 0)),
            pl.BlockSpec((tq, KV_LORA), row),
            pl.BlockSpec((tq, LANES), row),
            _resident(w_ukv.shape),
        ],
        out_specs=pl.BlockSpec((tq, MLA_WIDTH), row),
        out_shape=jax.ShapeDtypeStruct((n, MLA_WIDTH), BF16),
        compiler_params=_params("parallel"),
        name="mla_sample",
    )(qn, qr, cache_ckv, krd_cache, ckv_new, krd_new, w_ukv)


def _proj_out_kernel(x_ref, a_ref, b_ref, w_ref, o_ref):
    o_ref[...] = (x_ref[...] + _dot(a_ref[...], w_ref[:DA_WIDTH, :]) + _dot(b_ref[...], w_ref[DA_WIDTH:, :]))


def _proj_out(x, a_da, a_mla, w_out, *, tm):
    n, d = x.shape
    tm = min(tm, n)
    assert n % tm == 0
    row = lambda i: (i, 0)
    return pl.pallas_call(
        _proj_out_kernel,
        grid=(n // tm,),
        in_specs=[pl.BlockSpec((tm, d), row), pl.BlockSpec((tm, DA_WIDTH), row),
                  pl.BlockSpec((tm, MLA_WIDTH), row), _resident(w_out.shape)],
        out_specs=pl.BlockSpec((tm, d), row),
        out_shape=jax.ShapeDtypeStruct((n, d), F32),
        compiler_params=_params("parallel"),
        name="proj_out",
    )(x, a_da, a_mla, w_out)


def _prep_weights(w_in, mix_norm, q_norm, w_uq, kv_norm, w_ukv, w_out):
    w_in = w_in.astype(BF16)
    w_kr = w_in[:, 3 * DA_WIDTH + Q_LORA + KV_LORA:]
    w_uq = w_uq.astype(BF16).reshape(Q_LORA, MLA_HEADS, QK_HEAD)
    w_uqr = w_uq[:, :, QK_NOPE:]
    return {
        "mix_norm": mix_norm[None, :],
        "w_in": w_in,
        "w_kr": jnp.concatenate([w_kr, w_kr], axis=1),
        "q_norm": q_norm[None, :], "kv_norm": kv_norm[None, :],
        "w_uqn": w_uq[:, :, :QK_NOPE].reshape(Q_LORA, MLA_HEADS * QK_NOPE),
        "w_uqr": w_uqr.reshape(Q_LORA, MLA_HEADS * QK_ROPE),
        "w_ukv": w_ukv.astype(BF16),
        "w_out": w_out.astype(BF16),
    }


def _rope_tables(pos):
    half = QK_ROPE // 2
    inv = ROPE_THETA ** (-jnp.arange(half, dtype=F32) / half)
    ang = pos.astype(F32)[:, None] * inv[None, :]
    c = jnp.concatenate([jnp.cos(ang)] * 2, axis=1)
    s = jnp.concatenate([-jnp.sin(ang), jnp.sin(ang)], axis=1)
    return {"c2": jnp.tile(c, (1, 2)), "s2": jnp.tile(s, (1, 2)),
            "c8": jnp.tile(c, (1, MLA_HEADS)), "s8": jnp.tile(s, (1, MLA_HEADS))}


def _layer(x, pos, past, ffn1, ffn2, wmix, slopes, lams, subln, final_norm, lam_init, *, batch):
    n, d = x.shape
    t = n // batch

    def swiglu(x, ffn, *, last):
        casting = ffn[1].dtype == F32
        out = _ffn(x, *ffn, final_norm, final_norm=last, tm=1024, tf=256 if casting else 512)
        return (out[0], (ffn[0],) + out[1:]) if casting else (out, ffn)

    x, ffn1 = swiglu(x, ffn1, last=False)
    tm_in = 256
    tab_pos = pos if t % tm_in == 0 else jnp.tile(pos, batch)
    (q, k_f, k_b, v_f, v_b, ckv, kr, krd, qn, qr, kv) = _proj_in(x, wmix, _rope_tables(tab_pos), tm=tm_in)
    if past is None:
        a_da = _da_prompt(q, k_b, v_b, slopes, lams, subln, t=t, tq=256, lam_init=lam_init)
        a_mla = _mla_prompt(qn, qr, kv, krd, t=t, tq=256)
    else:
        cache_k, cache_v, cache_ckv, cache_kr = past
        p = cache_k.size // (batch * DA_WIDTH)
        a_da = _da_sample(q, cache_k.reshape(batch, p * DA_HEADS, DA_HEAD_DIM),
                          cache_v.reshape(batch, p * DA_HEADS, DA_HEAD_DIM),
                          k_b, v_b, slopes, lams, subln, lam_init=lam_init, pc=2048)
        cache_kr = cache_kr.reshape(batch, p, QK_ROPE)
        krd_cache = jnp.concatenate([cache_kr, cache_kr], axis=-1).astype(BF16)
        a_mla = _mla_sample(qn, qr, cache_ckv.reshape(batch, p, KV_LORA), krd_cache, ckv, krd, wmix["w_ukv"])
    x = _proj_out(x, a_da, a_mla, wmix["w_out"], tm=512)
    y, ffn2 = swiglu(x, ffn2, last=True)
    return y, (k_f, v_f, ckv, kr), ffn1, ffn2


def kernel(x_prompt, x_sample, cache_da_k, cache_da_v, cache_mla_ckv, cache_mla_krope, ffn1_norm, ffn1_w_gate, ffn1_w_up, ffn1_w_down, mix_norm, w_in, da_lambda_q1, da_lambda_k1, da_lambda_q2, da_lambda_k2, da_subln, mla_q_norm, mla_w_uq, mla_kv_norm, mla_w_ukv, w_out, ffn2_norm, ffn2_w_gate, ffn2_w_up, ffn2_w_down, final_norm):
    depth = w_in.shape[0]
    assert depth == 1, "single-layer trunk"
    b, t, d = x_prompt.shape
    bd, td, _ = x_sample.shape
    past_len = cache_da_k.shape[2]
    l = 0
    lam_init = 0.8 - 0.6 * math.exp(-0.3 * l)

    def one_layer(w):
        return w.reshape(w.shape[1:])

    ffn1 = (ffn1_norm[l][None, :], one_layer(ffn1_w_gate), one_layer(ffn1_w_up), one_layer(ffn1_w_down))
    ffn2 = (ffn2_norm[l][None, :], one_layer(ffn2_w_gate), one_layer(ffn2_w_up), one_layer(ffn2_w_down))
    wmix = _prep_weights(w_in[l], mix_norm[l], mla_q_norm[l], mla_w_uq[l], mla_kv_norm[l], mla_w_ukv[l],
                         w_out[l])
    slopes = jnp.asarray(np.broadcast_to(ALIBI_SLOPES[:, None, None], (DA_HEADS, 1, LANES)))
    lams = (da_lambda_q1[l][None, :], da_lambda_k1[l][None, :], da_lambda_q2[l][None, :],
            da_lambda_k2[l][None, :])
    subln = da_subln[l][None, :]
    fnorm = final_norm[None, :]

    past = (cache_da_k, cache_da_v, cache_mla_ckv, cache_mla_krope)
    assert bd * td <= 1024, "the running streams must fit one SwiGLU token tile"
    y_s, st_s, ffn1, ffn2 = _layer(x_sample.reshape(bd * td, d), past_len + jnp.arange(td, dtype=jnp.int32),
                                   past, ffn1, ffn2, wmix, slopes, lams, subln, fnorm, lam_init, batch=bd)
    y_p, st_p, _, _ = _layer(x_prompt.reshape(b * t, d), jnp.arange(t, dtype=jnp.int32), None,
                             ffn1, ffn2, wmix, slopes, lams, subln, fnorm, lam_init, batch=b)

    def state(st, nb, nt):
        k_f, v_f, ckv, kr = st
        return (k_f.reshape(1, nb, nt, DA_HEADS, DA_HEAD_DIM), v_f.reshape(1, nb, nt, DA_HEADS, DA_HEAD_DIM),
                ckv.reshape(1, nb, nt, KV_LORA), kr.reshape(1, nb, nt, QK_ROPE))

    return (y_p.reshape(b, t, d), y_s.reshape(bd, td, d)) + state(st_p, b, t) + state(st_s, bd, td)
```

```python
import functools
import math

import jax
import jax.numpy as jnp
import numpy as np
from jax import lax
from jax.experimental import pallas as pl
from jax.experimental.pallas import tpu as pltpu

F32 = jnp.float32
BF16 = jnp.bfloat16

CHUNK = 64
CHUNK_SHIFT = 6
assert 1 << CHUNK_SHIFT == CHUNK
EPS = 1e-6
NEG_INF = -1e30
DA_HEADS = 8
DA_HALF = 64
DA_HEAD_DIM = 2 * DA_HALF
DA_WIDTH = DA_HEADS * DA_HEAD_DIM
MLA_HEADS = 8
Q_LORA = 512
KV_LORA = 256
QK_NOPE = 128
QK_ROPE = 64
QK_HEAD = QK_NOPE + QK_ROPE
V_HEAD = 128
MLA_WIDTH = MLA_HEADS * V_HEAD
ROPE_THETA = 10000.0
LOG2E = math.log2(math.e)
ALIBI_SLOPES = np.array([2.0 ** (-8.0 * (i + 1) / DA_HEADS) for i in range(DA_HEADS)], np.float32)
LANES = 128
assert DA_HEAD_DIM == LANES and QK_NOPE == LANES and V_HEAD == LANES and 2 * QK_ROPE == LANES

MIB = 1024 * 1024
VMEM_LIMIT_BYTES = 56 * MIB
FFN_VMEM_LIMIT_BYTES = 62 * MIB
FFN_ROWS = 1024
FFN_COLS = 512
FFN_COLS_CASTING = 256
PROJ_IN_ROWS = 256
PROJ_OUT_ROWS = 512
QUERY_TILE = 256
CACHE_CHUNK = 2048


def _params(*semantics, vmem_limit_bytes=VMEM_LIMIT_BYTES):
    return pltpu.CompilerParams(dimension_semantics=semantics, vmem_limit_bytes=vmem_limit_bytes)


def _rms(x, g):
    return x * lax.rsqrt(jnp.mean(x * x, axis=-1, keepdims=True) + EPS) * g


def _dot(a, b):
    return jnp.dot(a, b, preferred_element_type=F32)


def _dot_nt(a, b):
    return lax.dot_general(a, b, (((1,), (1,)), ((), ())), preferred_element_type=F32)


def _resident(shape):
    return pl.BlockSpec(shape, lambda *_: (0,) * len(shape), pipeline_mode=pl.Buffered(1))


def _ffn_kernel(x_ref, g_ref, wg_ref, wu_ref, wd_ref, fn_ref, o_ref, *rest, final_norm, nj, emit_bf16):
    j = pl.program_id(1)
    xn_ref = rest[-1]

    @pl.when(j == 0)
    def _():
        xn_ref[...] = _rms(x_ref[...], g_ref[...]).astype(BF16)

    def weights():
        if not emit_bf16:
            return wg_ref[...], wu_ref[...], wd_ref[...]
        blocks = []
        for w_ref, wb_ref in zip((wg_ref, wu_ref, wd_ref), rest[:3]):
            wb = w_ref[...].astype(BF16)
            wb_ref[...] = wb
            blocks.append(wb)
        return blocks

    def half_down():
        xn = xn_ref[...]
        wg, wu, wd = weights()
        gate = _dot(xn, wg)
        up = _dot(xn, wu)
        h = (gate * jax.nn.sigmoid(gate) * (0.5 * up)).astype(BF16)
        return _dot(h, wd)

    def finish(y):
        o_ref[...] = _rms(y, fn_ref[...]) if final_norm else y

    if nj == 1:
        finish(x_ref[...] + half_down())
        return

    @pl.when(j == 0)
    def _():
        o_ref[...] = x_ref[...] + half_down()

    @pl.when(jnp.logical_and(j > 0, j < nj - 1))
    def _():
        o_ref[...] += half_down()

    @pl.when(j == nj - 1)
    def _():
        finish(o_ref[...] + half_down())


def _ffn(x, norm, wg, wu, wd, fnorm, *, final_norm, tm, tf):
    n, d = x.shape
    f = wg.shape[1]
    tm = min(tm, n)
    tf = min(tf, f)
    assert n % tm == 0 and f % tf == 0
    emit_bf16 = wg.dtype == F32
    assert not emit_bf16 or n == tm
    col_block = pl.BlockSpec((d, tf), lambda i, j: (0, j))
    row_block = pl.BlockSpec((tf, d), lambda i, j: (j, 0))
    y_spec = pl.BlockSpec((tm, d), lambda i, j: (i, 0))
    y_shape = jax.ShapeDtypeStruct((n, d), F32)
    out = pl.pallas_call(
        functools.partial(_ffn_kernel, final_norm=final_norm, nj=f // tf, emit_bf16=emit_bf16),
        grid=(n // tm, f // tf),
        in_specs=[
            pl.BlockSpec((tm, d), lambda i, j: (i, 0)),
            pl.BlockSpec((1, d), lambda i, j: (0, 0)),
            col_block, col_block, row_block,
            pl.BlockSpec((1, d), lambda i, j: (0, 0)),
        ],
        out_specs=[y_spec, col_block, col_block, row_block] if emit_bf16 else y_spec,
        out_shape=([y_shape] + [jax.ShapeDtypeStruct(w.shape, BF16) for w in (wg, wu, wd)]
                   if emit_bf16 else y_shape),
        scratch_shapes=[pltpu.VMEM((tm, d), BF16)],
        compiler_params=_params("parallel", "arbitrary", vmem_limit_bytes=FFN_VMEM_LIMIT_BYTES),
        name="ffn",
    )(x, norm, wg, wu, wd, fnorm)
    return tuple(out) if emit_bf16 else out


def _store_by_head(ref, z):
    tokens = z.shape[0]
    for h in range(DA_HEADS):
        ref[pl.ds(h, tokens, stride=DA_HEADS), :] = z[:, h * DA_HEAD_DIM:(h + 1) * DA_HEAD_DIM]


def _swap_rope_halves(x):
    n = x.shape[1]
    half = QK_ROPE // 2
    lane = lax.broadcasted_iota(jnp.int32, x.shape, 1)
    in_first_half = (lane & (QK_ROPE - 1)) < half
    return jnp.where(in_first_half, pltpu.roll(x, n - half, axis=1), pltpu.roll(x, half, axis=1))


def _proj_in_kernel(x_ref, g_ref, win_ref, wkr_ref,
                    qn_ref, kvn_ref, wuqn_ref, wuqr_ref, wukv_ref,
                    c2_ref, s2_ref, c8_ref, s8_ref,
                    q_ref, kf_ref, kb_ref, vf_ref, vb_ref, ckv_ref, kr_ref, krd_ref,
                    qnope_ref, qrope_ref, kv_ref):
    h = _rms(x_ref[...], g_ref[...]).astype(BF16)
    o1, o2, o3 = DA_WIDTH, 2 * DA_WIDTH, 3 * DA_WIDTH
    o4, o5 = o3 + Q_LORA, o3 + Q_LORA + KV_LORA
    q_ref[...] = (_dot(h, win_ref[:, :o1]) * (DA_HALF ** -0.5 * LOG2E)).astype(BF16)
    k = _dot(h, win_ref[:, o1:o2])
    _store_by_head(kf_ref, k)
    kb_ref[...] = k.astype(BF16)
    v = _dot(h, win_ref[:, o2:o3])
    _store_by_head(vf_ref, v)
    vb_ref[...] = v.astype(BF16)
    cq = _rms(_dot(h, win_ref[:, o3:o4]), qn_ref[...]).astype(BF16)
    ckv = _rms(_dot(h, win_ref[:, o4:o5]), kvn_ref[...])
    ckv_ref[...] = ckv
    kr = _dot(h, wkr_ref[...])
    krd = kr * c2_ref[...] + _swap_rope_halves(kr) * s2_ref[...]
    kr_ref[...] = krd[:, :QK_ROPE]
    krd_ref[...] = krd.astype(BF16)
    mla_scale = QK_HEAD ** -0.5 * LOG2E
    qnope_ref[...] = (_dot(cq, wuqn_ref[...]) * mla_scale).astype(BF16)
    qr = _dot(cq, wuqr_ref[...])
    qr = qr * c8_ref[...] + _swap_rope_halves(qr) * s8_ref[...]
    qrope_ref[...] = (qr * mla_scale).astype(BF16)
    kv_ref[...] = _dot(ckv.astype(BF16), wukv_ref[...]).astype(BF16)


def _proj_in(x, w, tabs, *, tm):
    n, d = x.shape
    tm = min(tm, n)
    assert n % tm == 0
    t_rows = tabs["c2"].shape[0]
    assert t_rows % tm == 0 or tm % t_rows == 0
    if tm > t_rows:
        tm = t_rows
    nt = t_rows // tm

    def row(i):
        return (i, 0)

    def trow(i):
        return (i % nt, 0)

    weights = [w["w_in"], w["w_kr"], w["q_norm"], w["kv_norm"], w["w_uqn"], w["w_uqr"], w["w_ukv"]]
    tables = [tabs["c2"], tabs["s2"], tabs["c8"], tabs["s8"]]
    outs = [
        ((n, DA_WIDTH), BF16),
        ((n * DA_HEADS, DA_HEAD_DIM), F32),
        ((n, DA_WIDTH), BF16),
        ((n * DA_HEADS, DA_HEAD_DIM), F32),
        ((n, DA_WIDTH), BF16),
        ((n, KV_LORA), F32),
        ((n, QK_ROPE), F32),
        ((n, LANES), BF16),
        ((n, MLA_HEADS * QK_NOPE), BF16),
        ((n, MLA_HEADS * QK_ROPE), BF16),
        ((n, MLA_HEADS * (QK_NOPE + V_HEAD)), BF16),
    ]
    return pl.pallas_call(
        _proj_in_kernel,
        grid=(n // tm,),
        in_specs=([pl.BlockSpec((tm, d), row), _resident((1, d))]
                  + [_resident(a.shape) for a in weights]
                  + [pl.BlockSpec((tm, a.shape[1]), trow) for a in tables]),
        out_specs=[pl.BlockSpec((tm * s[0] // n, s[1]), row) for s, _ in outs],
        out_shape=[jax.ShapeDtypeStruct(s, dt) for s, dt in outs],
        compiler_params=_params("parallel"),
        name="proj_in",
    )(x, w["mix_norm"], *weights, *tables)


def _lambda(lq1_ref, lk1_ref, lq2_ref, lk2_ref, lam_init):
    a = jnp.sum(lq1_ref[...] * lk1_ref[...], axis=-1, keepdims=True)
    b = jnp.sum(lq2_ref[...] * lk2_ref[...], axis=-1, keepdims=True)
    return jnp.exp(a) - jnp.exp(b) + lam_init


def _pos_bias(qpos, kpos, slope):
    visible = lax.shift_right_arithmetic(kpos, CHUNK_SHIFT) <= lax.shift_right_arithmetic(qpos, CHUNK_SHIFT)
    if slope is None:
        return jnp.where(visible, 0.0, NEG_INF)
    dist = jnp.abs(qpos - kpos).astype(F32)
    return jnp.where(visible, (-LOG2E * slope) * dist, NEG_INF)


def _strip(tq, t, slope):
    qpos = lax.broadcasted_iota(jnp.int32, (tq, t), 0)
    kpos = lax.broadcasted_iota(jnp.int32, (tq, t), 1) - (t - tq)
    return _pos_bias(qpos, kpos, slope)


def _split_halves(q):
    lane = lax.broadcasted_iota(jnp.int32, q.shape, 1)
    zero = jnp.zeros_like(q)
    return jnp.where(lane < DA_HALF, q, zero), jnp.where(lane >= DA_HALF, q, zero)


def _da_finish(o, subln, lam_init):
    return (_rms(o, subln) * (1.0 - lam_init)).astype(BF16)


BIAS_TERMS = 3


def _da_prompt_kernel(q_ref, k_ref, v_ref, tab_ref, slope_ref, lq1_ref, lk1_ref, lq2_ref, lk2_ref, subln_ref,
                      o_ref, diag_ref, k1_ref, k2_ref, *, tq, lam_init):
    t = q_ref.shape[0]
    lam = _lambda(lq1_ref, lk1_ref, lq2_ref, lk2_ref, lam_init)
    c = LOG2E * slope_ref[0][:, :1]

    lane = lax.broadcasted_iota(jnp.int32, (t, LANES), 1)
    k1_ref[...] = jnp.where(lane < DA_HALF, k_ref[...], tab_ref[0])
    k2_ref[...] = jnp.where(lane >= DA_HALF, k_ref[...], tab_ref[1])

    qpos = lax.broadcasted_iota(jnp.int32, (tq, tq), 0)
    kpos = lax.broadcasted_iota(jnp.int32, (tq, tq), 1)
    visible = lax.shift_right_arithmetic(kpos, CHUNK_SHIFT) <= lax.shift_right_arithmetic(qpos, CHUNK_SHIFT)
    ahead = jnp.maximum(kpos - qpos, 0).astype(F32)
    diag_ref[...] = jnp.where(visible, (-2.0 * c) * ahead, NEG_INF)

    qlane = lax.broadcasted_iota(jnp.int32, (tq, LANES), 1)
    ones_hi = ((qlane >= DA_HALF) & (qlane < DA_HALF + BIAS_TERMS)).astype(F32).astype(BF16)
    ones_lo = (qlane < BIAS_TERMS).astype(F32).astype(BF16)

    def scores(qi):
        lo, ext = qi * tq, (qi + 1) * tq
        q = q_ref[lo:ext, :]
        q1 = jnp.where(qlane < DA_HALF, q, ones_hi)
        q2 = jnp.where(qlane >= DA_HALF, q, ones_lo)
        diag = (_dot_nt(q1, k1_ref[lo:ext, :]) + diag_ref[...], _dot_nt(q2, k2_ref[lo:ext, :]) + diag_ref[...])
        past = (_dot_nt(q1, k1_ref[0:lo, :]), _dot_nt(q2, k2_ref[0:lo, :])) if qi > 0 else None
        return diag, past

    def softmax(sd, sp):
        m = jnp.max(sd, axis=-1, keepdims=True)
        if sp is None:
            pd = jnp.exp2(sd - m)
            return pd, None, jnp.sum(pd, axis=-1, keepdims=True)
        m = jnp.maximum(m, jnp.max(sp, axis=-1, keepdims=True))
        pd = jnp.exp2(sd - m)
        pp = jnp.exp2(sp - m)
        return pd, pp, jnp.sum(pd, axis=-1, keepdims=True) + jnp.sum(pp, axis=-1, keepdims=True)

    nq = t // tq
    s_next = scores(0)
    for qi in range(nq):
        lo, ext = qi * tq, (qi + 1) * tq
        diag, past = s_next
        if qi + 1 < nq:
            s_next = scores(qi + 1)
        pd1, pp1, l1 = softmax(diag[0], None if past is None else past[0])
        pd2, pp2, l2 = softmax(diag[1], None if past is None else past[1])
        r = lam * l1 / l2
        o = _dot((pd1 - pd2 * r).astype(BF16), v_ref[lo:ext, :])
        if past is not None:
            o = o + _dot((pp1 - pp2 * r).astype(BF16), v_ref[0:lo, :])
        o_ref[lo:ext, :] = _da_finish(o * (1.0 / l1), subln_ref[...], lam_init)


def _alibi_key_tables(t):
    x = (np.float32(LOG2E) * ALIBI_SLOPES)[:, None] * np.arange(t, dtype=np.float32)[None, :]
    pieces = []
    for _ in range(BIAS_TERMS):
        piece = x.astype(BF16)
        pieces.append(piece)
        x = x - piece.astype(np.float32)
    terms = jnp.asarray(np.stack(pieces, axis=-1))
    pad = lambda before: jnp.pad(terms, ((0, 0), (0, 0), (before, LANES - BIAS_TERMS - before)))
    return jnp.stack([pad(DA_HALF), pad(0)], axis=1)


def _da_prompt(q, k, v, slopes, lams, subln, *, t, tq, lam_init):
    n = q.shape[0]
    tq = min(tq, t)
    blk = pl.BlockSpec((t, LANES), lambda b, h: (b, h))
    vec = lambda a: pl.BlockSpec(a.shape, lambda b, h: (0, 0))
    return pl.pallas_call(
        functools.partial(_da_prompt_kernel, tq=tq, lam_init=lam_init),
        grid=(n // t, DA_HEADS),
        in_specs=[blk, blk, blk, pl.BlockSpec((None, 2, t, LANES), lambda b, h: (h, 0, 0, 0)),
                  pl.BlockSpec((1, 1, LANES), lambda b, h: (h, 0, 0))]
                 + [vec(a) for a in lams] + [vec(subln)],
        out_specs=blk,
        out_shape=jax.ShapeDtypeStruct((n, DA_WIDTH), BF16),
        scratch_shapes=[pltpu.VMEM((tq, tq), F32), pltpu.VMEM((t, LANES), BF16), pltpu.VMEM((t, LANES), BF16)],
        compiler_params=_params("parallel", "parallel"),
        name="da_prompt",
    )(q, k, v, _alibi_key_tables(t), slopes, *lams, subln)


def _select_rope(qr, h):
    lane = lax.broadcasted_iota(jnp.int32, qr.shape, 1)
    mine = lax.shift_right_logical(lane, CHUNK_SHIFT) == (h % 2)
    return jnp.where(mine, qr, jnp.zeros_like(qr))


MLA_PAIR = 2


def _mla_prompt_kernel(qn_ref, qr_ref, kv_ref, krd_ref, o_ref, mask_ref, kfull_ref, *, tq):
    t = qn_ref.shape[0]
    kv_cols = QK_NOPE + V_HEAD
    mask_ref[...] = _strip(tq, tq, None)
    for hh in range(MLA_PAIR):
        kfull_ref[hh, :, :LANES] = kv_ref[:, hh * kv_cols:hh * kv_cols + QK_NOPE]
        kfull_ref[hh, :, LANES:] = krd_ref[...]

    def scores(hh, qi):
        lo, ext = qi * tq, (qi + 1) * tq
        rope_block = qr_ref[lo:ext, (hh // 2) * LANES:(hh // 2 + 1) * LANES]
        q = jnp.concatenate([qn_ref[lo:ext, hh * LANES:(hh + 1) * LANES], _select_rope(rope_block, hh)], axis=1)
        sd = _dot_nt(q, kfull_ref[hh, lo:ext, :]) + mask_ref[...]
        sp = _dot_nt(q, kfull_ref[hh, 0:lo, :]) if qi > 0 else None
        return sd, sp

    def probabilities(sd, sp):
        m = jnp.max(sd, axis=-1, keepdims=True)
        if sp is None:
            pd = jnp.exp2(sd - m)
            return pd.astype(BF16), None, jnp.sum(pd, axis=-1, keepdims=True)
        m = jnp.maximum(m, jnp.max(sp, axis=-1, keepdims=True))
        pd = jnp.exp2(sd - m)
        pp = jnp.exp2(sp - m)
        l = jnp.sum(pd, axis=-1, keepdims=True) + jnp.sum(pp, axis=-1, keepdims=True)
        return pd.astype(BF16), pp.astype(BF16), l

    def weighted_values(hh, qi, pd, pp, l):
        lo, ext = qi * tq, (qi + 1) * tq
        v_cols = slice(hh * kv_cols + QK_NOPE, (hh + 1) * kv_cols)
        o = _dot(pd, kv_ref[lo:ext, v_cols])
        if pp is not None:
            o = o + _dot(pp, kv_ref[0:lo, v_cols])
        o_ref[lo:ext, hh * LANES:(hh + 1) * LANES] = (o * (1.0 / l)).astype(BF16)

    tiles = [(hh, qi) for hh in range(MLA_PAIR) for qi in range(t // tq)]
    s_next = scores(*tiles[0])
    pending = None
    for i, tile in enumerate(tiles):
        sd, sp = s_next
        if i + 1 < len(tiles):
            s_next = scores(*tiles[i + 1])
        if pending is not None:
            weighted_values(*tiles[i - 1], *pending)
        pending = probabilities(sd, sp)
    weighted_values(*tiles[-1], *pending)


def _mla_prompt(qn, qr, kv, krd, *, t, tq):
    n = qn.shape[0]
    tq = min(tq, t)
    pair = lambda width: pl.BlockSpec((t, MLA_PAIR * width), lambda b, g: (b, g))
    return pl.pallas_call(
        functools.partial(_mla_prompt_kernel, tq=tq),
        grid=(n // t, MLA_HEADS // MLA_PAIR),
        in_specs=[pair(QK_NOPE), pair(QK_ROPE), pair(QK_NOPE + V_HEAD),
                  pl.BlockSpec((t, LANES), lambda b, g: (b, 0))],
        out_specs=pair(V_HEAD),
        out_shape=jax.ShapeDtypeStruct((n, MLA_WIDTH), BF16),
        scratch_shapes=[pltpu.VMEM((tq, tq), F32), pltpu.VMEM((MLA_PAIR, t, 2 * LANES), BF16)],
        compiler_params=_params("parallel", "parallel"),
        name="mla_prompt",
    )(qn, qr, kv, krd)


def _positions(tq, tk, q0, k0):
    qpos = lax.broadcasted_iota(jnp.int32, (tq, tk), 0) + q0
    kpos = lax.broadcasted_iota(jnp.int32, (tq, tk), 1) + k0
    return qpos, kpos


def _da_sample_kernel(q_ref, ck_ref, cv_ref, nk_ref, nv_ref, slope_ref, lq1_ref, lk1_ref, lq2_ref, lk2_ref,
                      subln_ref, o_ref, acc_ref, m_ref, l_ref, *, lam_init):
    c = pl.program_id(1)
    nc = pl.num_programs(1)
    tq = q_ref.shape[0]
    pc = ck_ref.shape[0] // DA_HEADS
    past = pc * nc

    @pl.when(c == 0)
    def _():
        m_ref[...] = jnp.full_like(m_ref, NEG_INF)
        l_ref[...] = jnp.zeros_like(l_ref)
        acc_ref[...] = jnp.zeros_like(acc_ref)

    def attend(keys, values, unit_bias):
        unit2 = jnp.concatenate([unit_bias, unit_bias], axis=0)
        scores = []
        for h in range(DA_HEADS):
            q2 = jnp.concatenate(_split_halves(q_ref[:, h * LANES:(h + 1) * LANES]), axis=0)
            scores.append(_dot_nt(q2, keys(h)) + unit2 * slope_ref[h][:, :1])
        probs = []
        for h, s in enumerate(scores):
            rows = slice(2 * h * tq, (2 * h + 2) * tq)
            m_old = m_ref[rows, :]
            m_new = jnp.maximum(m_old, jnp.max(s, axis=-1, keepdims=True))
            alpha = jnp.exp2(m_old - m_new)
            p = jnp.exp2(s - m_new)
            m_ref[rows, :] = m_new
            l_ref[rows, :] = alpha * l_ref[rows, :] + jnp.sum(p, axis=-1, keepdims=True)
            probs.append((alpha, p.astype(BF16)))
        for h, (alpha, p) in enumerate(probs):
            rows = slice(2 * h * tq, (2 * h + 2) * tq)
            acc_ref[rows, :] = alpha * acc_ref[rows, :] + _dot(p, values(h))

    def head_rows(h):
        return pl.ds(h, pc, stride=DA_HEADS)

    attend(lambda h: ck_ref[head_rows(h), :].astype(BF16), lambda h: cv_ref[head_rows(h), :].astype(BF16),
           _pos_bias(*_positions(tq, pc, past, c * pc), 1.0))

    @pl.when(c == nc - 1)
    def _():
        lam = _lambda(lq1_ref, lk1_ref, lq2_ref, lk2_ref, lam_init)
        attend(lambda h: nk_ref[:, h * LANES:(h + 1) * LANES], lambda h: nv_ref[:, h * LANES:(h + 1) * LANES],
               _pos_bias(*_positions(tq, tq, past, past), 1.0))
        for h in range(DA_HEADS):
            r1 = slice(2 * h * tq, (2 * h + 1) * tq)
            r2 = slice((2 * h + 1) * tq, (2 * h + 2) * tq)
            o = acc_ref[r1, :] / l_ref[r1, :] - acc_ref[r2, :] * (lam / l_ref[r2, :])
            o_ref[:, h * LANES:(h + 1) * LANES] = _da_finish(o, subln_ref[...], lam_init)


def _da_sample(q, cache_k, cache_v, nk, nv, slopes, lams, subln, *, lam_init, pc):
    bd, rows, _ = cache_k.shape
    past = rows // DA_HEADS
    pc = min(pc, past)
    assert past % pc == 0
    n = q.shape[0]
    tq = n // bd
    new = pl.BlockSpec((tq, DA_WIDTH), lambda b, c: (b, 0))
    cache = pl.BlockSpec((None, pc * DA_HEADS, LANES), lambda b, c: (b, c, 0))
    whole = lambda a: pl.BlockSpec(a.shape, lambda b, c: (0,) * a.ndim)
    return pl.pallas_call(
        functools.partial(_da_sample_kernel, lam_init=lam_init),
        grid=(bd, past // pc),
        in_specs=[new, cache, cache, new, new, whole(slopes)] + [whole(a) for a in lams] + [whole(subln)],
        out_specs=new,
        out_shape=jax.ShapeDtypeStruct((n, DA_WIDTH), BF16),
        scratch_shapes=[pltpu.VMEM((2 * DA_HEADS * tq, LANES), F32),
                        pltpu.VMEM((2 * DA_HEADS * tq, 1), F32),
                        pltpu.VMEM((2 * DA_HEADS * tq, 1), F32)],
        compiler_params=_params("parallel", "arbitrary"),
        name="da_sample",
    )(q, cache_k, cache_v, nk, nv, slopes, *lams, subln)


MLA_GROUP = 4


def _mla_sample_kernel(qn_ref, qr_ref, cc_ref, ckrd_ref, nc_ref, nkrd_ref, wukv_ref, o_ref):
    tq = qn_ref.shape[0]
    past = cc_ref.shape[0]
    cc = cc_ref[...].astype(BF16)
    nc = nc_ref[...].astype(BF16)
    rows = MLA_GROUP * tq
    masked = ((past + tq - 1) >> CHUNK_SHIFT) > (past >> CHUNK_SHIFT)
    if masked:
        qrow = lax.broadcasted_iota(jnp.int32, (rows, 1), 0) & (tq - 1)
        bias_c = _pos_bias(qrow + past, lax.broadcasted_iota(jnp.int32, (rows, past), 1), None)
        bias_n = _pos_bias(qrow + past, lax.broadcasted_iota(jnp.int32, (rows, tq), 1) + past, None)
    kv_cols = QK_NOPE + V_HEAD
    for g in range(MLA_HEADS // MLA_GROUP):
        heads = range(g * MLA_GROUP, (g + 1) * MLA_GROUP)
        qa = jnp.concatenate(
            [_dot_nt(qn_ref[:, h * QK_NOPE:(h + 1) * QK_NOPE],
                     wukv_ref[:, h * kv_cols:h * kv_cols + QK_NOPE]).astype(BF16) for h in heads], axis=0)
        qr = jnp.concatenate(
            [_select_rope(qr_ref[:, (h // 2) * LANES:(h // 2 + 1) * LANES], h) for h in heads], axis=0)
        sc = _dot_nt(qa, cc) + _dot_nt(qr, ckrd_ref[...])
        sn = _dot_nt(qa, nc) + _dot_nt(qr, nkrd_ref[...])
        if masked:
            sc, sn = sc + bias_c, sn + bias_n
        m = jnp.maximum(jnp.max(sc, axis=-1, keepdims=True), jnp.max(sn, axis=-1, keepdims=True))
        pc = jnp.exp2(sc - m)
        pn = jnp.exp2(sn - m)
        w = 1.0 / (jnp.sum(pc, axis=-1, keepdims=True) + jnp.sum(pn, axis=-1, keepdims=True))
        lat = ((_dot(pc.astype(BF16), cc) + _dot(pn.astype(BF16), nc)) * w).astype(BF16)
        for j, h in enumerate(heads):
            w_uv = wukv_ref[:, h * kv_cols + QK_NOPE:(h + 1) * kv_cols]
            o_ref[:, h * V_HEAD:(h + 1) * V_HEAD] = _dot(lat[j * tq:(j + 1) * tq, :], w_uv).astype(BF16)


def _mla_sample(qn, qr, cache_ckv, krd_cache, ckv_new, krd_new, w_ukv):
    bd, past, _ = cache_ckv.shape
    n = qn.shape[0]
    tq = n // bd
    assert tq & (tq - 1) == 0
    row = lambda b: (b, 0)
    return pl.pallas_call(
        _mla_sample_kernel,
        grid=(bd,),
        in_specs=[
            pl.BlockSpec((tq, MLA_HEADS * QK_NOPE), row),
            pl.BlockSpec((tq, MLA_HEADS * QK_ROPE), row),
            pl.BlockSpec((None, past, KV_LORA), lambda b: (b, 0, 0)),
            pl.BlockSpec((None, past, LANES), lambda b: (b, 0, 0)),
            pl.BlockSpec((tq, KV_LORA), row),
            pl.BlockSpec((tq, LANES), row),
            _resident(w_ukv.shape),
        ],
        out_specs=pl.BlockSpec((tq, MLA_WIDTH), row),
        out_shape=jax.ShapeDtypeStruct((n, MLA_WIDTH), BF16),
        compiler_params=_params("parallel"),
        name="mla_sample",
    )(qn, qr, cache_ckv, krd_cache, ckv_new, krd_new, w_ukv)


def _proj_out_kernel(x_ref, a_ref, b_ref, w_ref, o_ref):
    o_ref[...] = (x_ref[...] + _dot(a_ref[...], w_ref[:DA_WIDTH, :]) + _dot(b_ref[...], w_ref[DA_WIDTH:, :]))


def _proj_out(x, a_da, a_mla, w_out, *, tm):
    n, d = x.shape
    tm = min(tm, n)
    assert n % tm == 0
    row = lambda i: (i, 0)
    return pl.pallas_call(
        _proj_out_kernel,
        grid=(n // tm,),
        in_specs=[pl.BlockSpec((tm, d), row), pl.BlockSpec((tm, DA_WIDTH), row),
                  pl.BlockSpec((tm, MLA_WIDTH), row), _resident(w_out.shape)],
        out_specs=pl.BlockSpec((tm, d), row),
        out_shape=jax.ShapeDtypeStruct((n, d), F32),
        compiler_params=_params("parallel"),
        name="proj_out",
    )(x, a_da, a_mla, w_out)


def _prep_weights(w_in, mix_norm, q_norm, w_uq, kv_norm, w_ukv, w_out):
    w_in = w_in.astype(BF16)
    w_kr = w_in[:, 3 * DA_WIDTH + Q_LORA + KV_LORA:]
    w_uq = w_uq.astype(BF16).reshape(Q_LORA, MLA_HEADS, QK_HEAD)
    w_uqr = w_uq[:, :, QK_NOPE:]
    return {
        "mix_norm": mix_norm[None, :],
        "w_in": w_in,
        "w_kr": jnp.concatenate([w_kr, w_kr], axis=1),
        "q_norm": q_norm[None, :], "kv_norm": kv_norm[None, :],
        "w_uqn": w_uq[:, :, :QK_NOPE].reshape(Q_LORA, MLA_HEADS * QK_NOPE),
        "w_uqr": w_uqr.reshape(Q_LORA, MLA_HEADS * QK_ROPE),
        "w_ukv": w_ukv.astype(BF16),
        "w_out": w_out.astype(BF16),
    }


def _rope_tables(pos):
    half = QK_ROPE // 2
    inv = ROPE_THETA ** (-jnp.arange(half, dtype=F32) / half)
    ang = pos.astype(F32)[:, None] * inv[None, :]
    c = jnp.concatenate([jnp.cos(ang)] * 2, axis=1)
    s = jnp.concatenate([-jnp.sin(ang), jnp.sin(ang)], axis=1)
    return {"c2": jnp.tile(c, (1, 2)), "s2": jnp.tile(s, (1, 2)),
            "c8": jnp.tile(c, (1, MLA_HEADS)), "s8": jnp.tile(s, (1, MLA_HEADS))}


def _layer(x, pos, past, ffn1, ffn2, wmix, slopes, lams, subln, final_norm, lam_init, *, batch):
    n, d = x.shape
    t = n // batch

    def swiglu(x, ffn, *, last):
        casting = ffn[1].dtype == F32
        out = _ffn(x, *ffn, final_norm, final_norm=last, tm=FFN_ROWS,
                   tf=FFN_COLS_CASTING if casting else FFN_COLS)
        return (out[0], (ffn[0],) + out[1:]) if casting else (out, ffn)

    x, ffn1 = swiglu(x, ffn1, last=False)
    tab_pos = pos if t % PROJ_IN_ROWS == 0 else jnp.tile(pos, batch)
    (q, k_f, k_b, v_f, v_b, ckv, kr, krd, qn, qr, kv) = _proj_in(x, wmix, _rope_tables(tab_pos), tm=PROJ_IN_ROWS)
    if past is None:
        a_da = _da_prompt(q, k_b, v_b, slopes, lams, subln, t=t, tq=QUERY_TILE, lam_init=lam_init)
        a_mla = _mla_prompt(qn, qr, kv, krd, t=t, tq=QUERY_TILE)
    else:
        cache_k, cache_v, cache_ckv, cache_kr = past
        p = cache_k.size // (batch * DA_WIDTH)
        a_da = _da_sample(q, cache_k.reshape(batch, p * DA_HEADS, DA_HEAD_DIM),
                          cache_v.reshape(batch, p * DA_HEADS, DA_HEAD_DIM),
                          k_b, v_b, slopes, lams, subln, lam_init=lam_init, pc=CACHE_CHUNK)
        cache_kr = cache_kr.reshape(batch, p, QK_ROPE)
        krd_cache = jnp.concatenate([cache_kr, cache_kr], axis=-1).astype(BF16)
        a_mla = _mla_sample(qn, qr, cache_ckv.reshape(batch, p, KV_LORA), krd_cache, ckv, krd, wmix["w_ukv"])
    x = _proj_out(x, a_da, a_mla, wmix["w_out"], tm=PROJ_OUT_ROWS)
    y, ffn2 = swiglu(x, ffn2, last=True)
    return y, (k_f, v_f, ckv, kr), ffn1, ffn2


def kernel(x_prompt, x_sample, cache_da_k, cache_da_v, cache_mla_ckv, cache_mla_krope, ffn1_norm, ffn1_w_gate, ffn1_w_up, ffn1_w_down, mix_norm, w_in, da_lambda_q1, da_lambda_k1, da_lambda_q2, da_lambda_k2, da_subln, mla_q_norm, mla_w_uq, mla_kv_norm, mla_w_ukv, w_out, ffn2_norm, ffn2_w_gate, ffn2_w_up, ffn2_w_down, final_norm):
    depth = w_in.shape[0]
    assert depth == 1, "single-layer trunk"
    b, t, d = x_prompt.shape
    bd, td, _ = x_sample.shape
    past_len = cache_da_k.shape[2]
    l = 0
    lam_init = 0.8 - 0.6 * math.exp(-0.3 * l)

    def one_layer(w):
        return w.reshape(w.shape[1:])

    ffn1 = (ffn1_norm[l][None, :], one_layer(ffn1_w_gate), one_layer(ffn1_w_up), one_layer(ffn1_w_down))
    ffn2 = (ffn2_norm[l][None, :], one_layer(ffn2_w_gate), one_layer(ffn2_w_up), one_layer(ffn2_w_down))
    wmix = _prep_weights(w_in[l], mix_norm[l], mla_q_norm[l], mla_w_uq[l], mla_kv_norm[l], mla_w_ukv[l],
                         w_out[l])
    slopes = jnp.asarray(np.broadcast_to(ALIBI_SLOPES[:, None, None], (DA_HEADS, 1, LANES)))
    lams = (da_lambda_q1[l][None, :], da_lambda_k1[l][None, :], da_lambda_q2[l][None, :],
            da_lambda_k2[l][None, :])
    subln = da_subln[l][None, :]
    fnorm = final_norm[None, :]

    past = (cache_da_k, cache_da_v, cache_mla_ckv, cache_mla_krope)
    assert bd * td <= 1024, "the running streams must fit one SwiGLU token tile"
    y_s, st_s, ffn1, ffn2 = _layer(x_sample.reshape(bd * td, d), past_len + jnp.arange(td, dtype=jnp.int32),
                                   past, ffn1, ffn2, wmix, slopes, lams, subln, fnorm, lam_init, batch=bd)
    y_p, st_p, _, _ = _layer(x_prompt.reshape(b * t, d), jnp.arange(t, dtype=jnp.int32), None,
                             ffn1, ffn2, wmix, slopes, lams, subln, fnorm, lam_init, batch=b)

    def state(st, nb, nt):
        k_f, v_f, ckv, kr = st
        return (k_f.reshape(1, nb, nt, DA_HEADS, DA_HEAD_DIM), v_f.reshape(1, nb, nt, DA_HEADS, DA_HEAD_DIM),
                ckv.reshape(1, nb, nt, KV_LORA), kr.reshape(1, nb, nt, QK_ROPE))

    return (y_p.reshape(b, t, d), y_s.reshape(bd, td, d)) + state(st_p, b, t) + state(st_s, bd, td)
```

```python
import functools
import math

import jax
import jax.numpy as jnp
import numpy as np
from jax import lax
from jax.experimental import pallas as pl
from jax.experimental.pallas import tpu as pltpu

F32 = jnp.float32
BF16 = jnp.bfloat16

CHUNK = 64
CHUNK_SHIFT = 6
assert 1 << CHUNK_SHIFT == CHUNK
EPS = 1e-6
NEG_INF = -1e30
DA_HEADS = 8
DA_HALF = 64
DA_HEAD_DIM = 2 * DA_HALF
DA_WIDTH = DA_HEADS * DA_HEAD_DIM
MLA_HEADS = 8
Q_LORA = 512
KV_LORA = 256
QK_NOPE = 128
QK_ROPE = 64
QK_HEAD = QK_NOPE + QK_ROPE
V_HEAD = 128
MLA_WIDTH = MLA_HEADS * V_HEAD
ROPE_THETA = 10000.0
LOG2E = math.log2(math.e)
ALIBI_SLOPES = np.array([2.0 ** (-8.0 * (i + 1) / DA_HEADS) for i in range(DA_HEADS)], np.float32)
LANES = 128
assert DA_HEAD_DIM == LANES and QK_NOPE == LANES and V_HEAD == LANES and 2 * QK_ROPE == LANES

MIB = 1024 * 1024
VMEM_LIMIT_BYTES = 56 * MIB
FFN_VMEM_LIMIT_BYTES = 62 * MIB
FFN_ROWS = 1024
FFN_COLS = 512
FFN_COLS_CASTING = 256
PROJ_IN_ROWS = 256
PROJ_OUT_ROWS = 512
QUERY_TILE = 256
CACHE_CHUNK = 2048


def _params(*semantics, vmem_limit_bytes=VMEM_LIMIT_BYTES):
    return pltpu.CompilerParams(dimension_semantics=semantics, vmem_limit_bytes=vmem_limit_bytes)


def _rms(x, g):
    return x * lax.rsqrt(jnp.mean(x * x, axis=-1, keepdims=True) + EPS) * g


def _dot(a, b):
    return jnp.dot(a, b, preferred_element_type=F32)


def _dot_nt(a, b):
    return lax.dot_general(a, b, (((1,), (1,)), ((), ())), preferred_element_type=F32)


def _resident(shape):
    return pl.BlockSpec(shape, lambda *_: (0,) * len(shape), pipeline_mode=pl.Buffered(1))


def _ffn_kernel(x_ref, g_ref, wg_ref, wu_ref, wd_ref, fn_ref, o_ref, *rest, final_norm, nj, emit_bf16):
    j = pl.program_id(1)
    xn_ref = rest[-1]

    @pl.when(j == 0)
    def _():
        xn_ref[...] = _rms(x_ref[...], g_ref[...]).astype(BF16)

    def weights():
        if not emit_bf16:
            return wg_ref[...], wu_ref[...], wd_ref[...]
        blocks = []
        for w_ref, wb_ref in zip((wg_ref, wu_ref, wd_ref), rest[:3]):
            wb = w_ref[...].astype(BF16)
            wb_ref[...] = wb
            blocks.append(wb)
        return blocks

    def half_down():
        xn = xn_ref[...]
        wg, wu, wd = weights()
        gate = _dot(xn, wg)
        up = _dot(xn, wu)
        h = (gate * jax.nn.sigmoid(gate) * (0.5 * up)).astype(BF16)
        return _dot(h, wd)

    def finish(y):
        o_ref[...] = _rms(y, fn_ref[...]) if final_norm else y

    if nj == 1:
        finish(x_ref[...] + half_down())
        return

    @pl.when(j == 0)
    def _():
        o_ref[...] = x_ref[...] + half_down()

    @pl.when(jnp.logical_and(j > 0, j < nj - 1))
    def _():
        o_ref[...] += half_down()

    @pl.when(j == nj - 1)
    def _():
        finish(o_ref[...] + half_down())


def _ffn(x, norm, wg, wu, wd, fnorm, *, final_norm, tm, tf):
    n, d = x.shape
    f = wg.shape[1]
    tm = min(tm, n)
    tf = min(tf, f)
    assert n % tm == 0 and f % tf == 0
    emit_bf16 = wg.dtype == F32
    assert not emit_bf16 or n == tm
    col_block = pl.BlockSpec((d, tf), lambda i, j: (0, j))
    row_block = pl.BlockSpec((tf, d), lambda i, j: (j, 0))
    y_spec = pl.BlockSpec((tm, d), lambda i, j: (i, 0))
    y_shape = jax.ShapeDtypeStruct((n, d), F32)
    out = pl.pallas_call(
        functools.partial(_ffn_kernel, final_norm=final_norm, nj=f // tf, emit_bf16=emit_bf16),
        grid=(n // tm, f // tf),
        in_specs=[
            pl.BlockSpec((tm, d), lambda i, j: (i, 0)),
            pl.BlockSpec((1, d), lambda i, j: (0, 0)),
            col_block, col_block, row_block,
            pl.BlockSpec((1, d), lambda i, j: (0, 0)),
        ],
        out_specs=[y_spec, col_block, col_block, row_block] if emit_bf16 else y_spec,
        out_shape=([y_shape] + [jax.ShapeDtypeStruct(w.shape, BF16) for w in (wg, wu, wd)]
                   if emit_bf16 else y_shape),
        scratch_shapes=[pltpu.VMEM((tm, d), BF16)],
        compiler_params=_params("parallel", "arbitrary", vmem_limit_bytes=FFN_VMEM_LIMIT_BYTES),
        name="ffn",
    )(x, norm, wg, wu, wd, fnorm)
    return tuple(out) if emit_bf16 else out


def _store_by_head(ref, z):
    tokens = z.shape[0]
    for h in range(DA_HEADS):
        ref[pl.ds(h, tokens, stride=DA_HEADS), :] = z[:, h * DA_HEAD_DIM:(h + 1) * DA_HEAD_DIM]


def _swap_rope_halves(x):
    n = x.shape[1]
    half = QK_ROPE // 2
    lane = lax.broadcasted_iota(jnp.int32, x.shape, 1)
    in_first_half = (lane & (QK_ROPE - 1)) < half
    return jnp.where(in_first_half, pltpu.roll(x, n - half, axis=1), pltpu.roll(x, half, axis=1))


def _proj_in_kernel(x_ref, g_ref, win_ref, wkr_ref,
                    qn_ref, kvn_ref, wuqn_ref, wuqr_ref, wukv_ref,
                    c2_ref, s2_ref, c8_ref, s8_ref,
                    q_ref, kf_ref, kb_ref, vf_ref, vb_ref, ckv_ref, kr_ref, krd_ref,
                    qnope_ref, qrope_ref, kv_ref):
    h = _rms(x_ref[...], g_ref[...]).astype(BF16)
    o1, o2, o3 = DA_WIDTH, 2 * DA_WIDTH, 3 * DA_WIDTH
    o4, o5 = o3 + Q_LORA, o3 + Q_LORA + KV_LORA
    q_ref[...] = (_dot(h, win_ref[:, :o1]) * (DA_HALF ** -0.5 * LOG2E)).astype(BF16)
    k = _dot(h, win_ref[:, o1:o2])
    _store_by_head(kf_ref, k)
    kb_ref[...] = k.astype(BF16)
    v = _dot(h, win_ref[:, o2:o3])
    _store_by_head(vf_ref, v)
    vb_ref[...] = v.astype(BF16)
    cq = _rms(_dot(h, win_ref[:, o3:o4]), qn_ref[...]).astype(BF16)
    ckv = _rms(_dot(h, win_ref[:, o4:o5]), kvn_ref[...])
    ckv_ref[...] = ckv
    kr = _dot(h, wkr_ref[...])
    krd = kr * c2_ref[...] + _swap_rope_halves(kr) * s2_ref[...]
    kr_ref[...] = krd[:, :QK_ROPE]
    krd_ref[...] = krd.astype(BF16)
    mla_scale = QK_HEAD ** -0.5 * LOG2E
    qnope_ref[...] = (_dot(cq, wuqn_ref[...]) * mla_scale).astype(BF16)
    qr = _dot(cq, wuqr_ref[...])
    qr = qr * c8_ref[...] + _swap_rope_halves(qr) * s8_ref[...]
    qrope_ref[...] = (qr * mla_scale).astype(BF16)
    kv_ref[...] = _dot(ckv.astype(BF16), wukv_ref[...]).astype(BF16)


def _proj_in(x, w, tabs, *, tm):
    n, d = x.shape
    tm = min(tm, n)
    assert n % tm == 0
    t_rows = tabs["c2"].shape[0]
    assert t_rows % tm == 0 or tm % t_rows == 0
    if tm > t_rows:
        tm = t_rows
    nt = t_rows // tm

    def row(i):
        return (i, 0)

    def trow(i):
        return (i % nt, 0)

    weights = [w["w_in"], w["w_kr"], w["q_norm"], w["kv_norm"], w["w_uqn"], w["w_uqr"], w["w_ukv"]]
    tables = [tabs["c2"], tabs["s2"], tabs["c8"], tabs["s8"]]
    outs = [
        ((n, DA_WIDTH), BF16),
        ((n * DA_HEADS, DA_HEAD_DIM), F32),
        ((n, DA_WIDTH), BF16),
        ((n * DA_HEADS, DA_HEAD_DIM), F32),
        ((n, DA_WIDTH), BF16),
        ((n, KV_LORA), F32),
        ((n, QK_ROPE), F32),
        ((n, LANES), BF16),
        ((n, MLA_HEADS * QK_NOPE), BF16),
        ((n, MLA_HEADS * QK_ROPE), BF16),
        ((n, MLA_HEADS * (QK_NOPE + V_HEAD)), BF16),
    ]
    return pl.pallas_call(
        _proj_in_kernel,
        grid=(n // tm,),
        in_specs=([pl.BlockSpec((tm, d), row), _resident((1, d))]
                  + [_resident(a.shape) for a in weights]
                  + [pl.BlockSpec((tm, a.shape[1]), trow) for a in tables]),
        out_specs=[pl.BlockSpec((tm * s[0] // n, s[1]), row) for s, _ in outs],
        out_shape=[jax.ShapeDtypeStruct(s, dt) for s, dt in outs],
        compiler_params=_params("parallel"),
        name="proj_in",
    )(x, w["mix_norm"], *weights, *tables)


def _lambda(lq1_ref, lk1_ref, lq2_ref, lk2_ref, lam_init):
    a = jnp.sum(lq1_ref[...] * lk1_ref[...], axis=-1, keepdims=True)
    b = jnp.sum(lq2_ref[...] * lk2_ref[...], axis=-1, keepdims=True)
    return jnp.exp(a) - jnp.exp(b) + lam_init


def _pos_bias(qpos, kpos, slope):
    visible = lax.shift_right_arithmetic(kpos, CHUNK_SHIFT) <= lax.shift_right_arithmetic(qpos, CHUNK_SHIFT)
    if slope is None:
        return jnp.where(visible, 0.0, NEG_INF)
    dist = jnp.abs(qpos - kpos).astype(F32)
    return jnp.where(visible, (-LOG2E * slope) * dist, NEG_INF)


def _strip(tq, t, slope):
    qpos = lax.broadcasted_iota(jnp.int32, (tq, t), 0)
    kpos = lax.broadcasted_iota(jnp.int32, (tq, t), 1) - (t - tq)
    return _pos_bias(qpos, kpos, slope)


def _split_halves(q):
    lane = lax.broadcasted_iota(jnp.int32, q.shape, 1)
    zero = jnp.zeros_like(q)
    return jnp.where(lane < DA_HALF, q, zero), jnp.where(lane >= DA_HALF, q, zero)


def _da_finish(o, subln, lam_init):
    return (_rms(o, subln) * (1.0 - lam_init)).astype(BF16)


BIAS_TERMS = 3


def _da_prompt_kernel(q_ref, k_ref, v_ref, tab_ref, slope_ref, lq1_ref, lk1_ref, lq2_ref, lk2_ref, subln_ref,
                      o_ref, diag_ref, k1_ref, k2_ref, vt_ref, *, tq, lam_init):
    t = q_ref.shape[0]
    lam = _lambda(lq1_ref, lk1_ref, lq2_ref, lk2_ref, lam_init)
    c = LOG2E * slope_ref[0][:, :1]

    lane = lax.broadcasted_iota(jnp.int32, (t, LANES), 1)
    k1_ref[...] = jnp.where(lane < DA_HALF, k_ref[...], tab_ref[0])
    k2_ref[...] = jnp.where(lane >= DA_HALF, k_ref[...], tab_ref[1])
    vt_ref[...] = v_ref[...].T

    kpos = lax.broadcasted_iota(jnp.int32, (tq, tq), 0)
    qpos = lax.broadcasted_iota(jnp.int32, (tq, tq), 1)
    visible = lax.shift_right_arithmetic(kpos, CHUNK_SHIFT) <= lax.shift_right_arithmetic(qpos, CHUNK_SHIFT)
    ahead = jnp.maximum(kpos - qpos, 0).astype(F32)
    diag_ref[...] = jnp.where(visible, (-2.0 * c) * ahead, NEG_INF)

    qlane = lax.broadcasted_iota(jnp.int32, (tq, LANES), 1)
    ones_hi = ((qlane >= DA_HALF) & (qlane < DA_HALF + BIAS_TERMS)).astype(F32).astype(BF16)
    ones_lo = (qlane < BIAS_TERMS).astype(F32).astype(BF16)

    def scores(qi):
        lo, ext = qi * tq, (qi + 1) * tq
        q = q_ref[lo:ext, :]
        q1 = jnp.where(qlane < DA_HALF, q, ones_hi)
        q2 = jnp.where(qlane >= DA_HALF, q, ones_lo)
        diag = (_dot_nt(k1_ref[lo:ext, :], q1) + diag_ref[...], _dot_nt(k2_ref[lo:ext, :], q2) + diag_ref[...])
        past = (_dot_nt(k1_ref[0:lo, :], q1), _dot_nt(k2_ref[0:lo, :], q2)) if qi > 0 else None
        return diag, past

    def softmax(sd, sp):
        m = jnp.max(sd, axis=0, keepdims=True)
        if sp is None:
            pd = jnp.exp2(sd - m)
            return pd, None, jnp.sum(pd, axis=0, keepdims=True)
        m = jnp.maximum(m, jnp.max(sp, axis=0, keepdims=True))
        pd = jnp.exp2(sd - m)
        pp = jnp.exp2(sp - m)
        return pd, pp, jnp.sum(pd, axis=0, keepdims=True) + jnp.sum(pp, axis=0, keepdims=True)

    nq = t // tq
    s_next = scores(0)
    for qi in range(nq):
        lo, ext = qi * tq, (qi + 1) * tq
        diag, past = s_next
        if qi + 1 < nq:
            s_next = scores(qi + 1)
        pd1, pp1, l1 = softmax(diag[0], None if past is None else past[0])
        pd2, pp2, l2 = softmax(diag[1], None if past is None else past[1])
        r = lam * l1 / l2
        ot = _dot(vt_ref[:, lo:ext], (pd1 - pd2 * r).astype(BF16))
        if past is not None:
            ot = ot + _dot(vt_ref[:, 0:lo], (pp1 - pp2 * r).astype(BF16))
        ot = ot * (1.0 / l1)
        ot = ot * lax.rsqrt(jnp.mean(ot * ot, axis=0, keepdims=True) + EPS) * (subln_ref[...] * (1.0 - lam_init))
        o_ref[lo:ext, :] = ot.T.astype(BF16)


def _alibi_key_tables(t):
    x = (np.float32(LOG2E) * ALIBI_SLOPES)[:, None] * np.arange(t, dtype=np.float32)[None, :]
    pieces = []
    for _ in range(BIAS_TERMS):
        piece = x.astype(BF16)
        pieces.append(piece)
        x = x - piece.astype(np.float32)
    terms = jnp.asarray(np.stack(pieces, axis=-1))
    pad = lambda before: jnp.pad(terms, ((0, 0), (0, 0), (before, LANES - BIAS_TERMS - before)))
    return jnp.stack([pad(DA_HALF), pad(0)], axis=1)


def _da_prompt(q, k, v, slopes, lams, subln, *, t, tq, lam_init):
    n = q.shape[0]
    subln = subln.reshape(DA_HEAD_DIM, 1)
    tq = min(tq, t)
    blk = pl.BlockSpec((t, LANES), lambda b, h: (b, h))
    vec = lambda a: pl.BlockSpec(a.shape, lambda b, h: (0, 0))
    return pl.pallas_call(
        functools.partial(_da_prompt_kernel, tq=tq, lam_init=lam_init),
        grid=(n // t, DA_HEADS),
        in_specs=[blk, blk, blk, pl.BlockSpec((None, 2, t, LANES), lambda b, h: (h, 0, 0, 0)),
                  pl.BlockSpec((1, 1, LANES), lambda b, h: (h, 0, 0))]
                 + [vec(a) for a in lams] + [vec(subln)],
        out_specs=blk,
        out_shape=jax.ShapeDtypeStruct((n, DA_WIDTH), BF16),
        scratch_shapes=[pltpu.VMEM((tq, tq), F32), pltpu.VMEM((t, LANES), BF16), pltpu.VMEM((t, LANES), BF16),
                        pltpu.VMEM((LANES, t), BF16)],
        compiler_params=_params("parallel", "parallel"),
        name="da_prompt",
    )(q, k, v, _alibi_key_tables(t), slopes, *lams, subln)


def _select_rope(qr, h):
    lane = lax.broadcasted_iota(jnp.int32, qr.shape, 1)
    mine = lax.shift_right_logical(lane, CHUNK_SHIFT) == (h % 2)
    return jnp.where(mine, qr, jnp.zeros_like(qr))


MLA_PAIR = 2


def _mla_prompt_kernel(qn_ref, qr_ref, kv_ref, krd_ref, o_ref, mask_ref, kfull_ref, *, tq):
    t = qn_ref.shape[0]
    kv_cols = QK_NOPE + V_HEAD
    mask_ref[...] = _strip(tq, tq, None)
    for hh in range(MLA_PAIR):
        kfull_ref[hh, :, :LANES] = kv_ref[:, hh * kv_cols:hh * kv_cols + QK_NOPE]
        kfull_ref[hh, :, LANES:] = krd_ref[...]

    def scores(hh, qi):
        lo, ext = qi * tq, (qi + 1) * tq
        rope_block = qr_ref[lo:ext, (hh // 2) * LANES:(hh // 2 + 1) * LANES]
        q = jnp.concatenate([qn_ref[lo:ext, hh * LANES:(hh + 1) * LANES], _select_rope(rope_block, hh)], axis=1)
        sd = _dot_nt(q, kfull_ref[hh, lo:ext, :]) + mask_ref[...]
        sp = _dot_nt(q, kfull_ref[hh, 0:lo, :]) if qi > 0 else None
        return sd, sp

    def probabilities(sd, sp):
        m = jnp.max(sd, axis=-1, keepdims=True)
        if sp is None:
            pd = jnp.exp2(sd - m)
            return pd.astype(BF16), None, jnp.sum(pd, axis=-1, keepdims=True)
        m = jnp.maximum(m, jnp.max(sp, axis=-1, keepdims=True))
        pd = jnp.exp2(sd - m)
        pp = jnp.exp2(sp - m)
        l = jnp.sum(pd, axis=-1, keepdims=True) + jnp.sum(pp, axis=-1, keepdims=True)
        return pd.astype(BF16), pp.astype(BF16), l

    def weighted_values(hh, qi, pd, pp, l):
        lo, ext = qi * tq, (qi + 1) * tq
        v_cols = slice(hh * kv_cols + QK_NOPE, (hh + 1) * kv_cols)
        o = _dot(pd, kv_ref[lo:ext, v_cols])
        if pp is not None:
            o = o + _dot(pp, kv_ref[0:lo, v_cols])
        o_ref[lo:ext, hh * LANES:(hh + 1) * LANES] = (o * (1.0 / l)).astype(BF16)

    tiles = [(hh, qi) for hh in range(MLA_PAIR) for qi in range(t // tq)]
    s_next = scores(*tiles[0])
    pending = None
    for i, tile in enumerate(tiles):
        sd, sp = s_next
        if i + 1 < len(tiles):
            s_next = scores(*tiles[i + 1])
        if pending is not None:
            weighted_values(*tiles[i - 1], *pending)
        pending = probabilities(sd, sp)
    weighted_values(*tiles[-1], *pending)


def _mla_prompt(qn, qr, kv, krd, *, t, tq):
    n = qn.shape[0]
    tq = min(tq, t)
    pair = lambda width: pl.BlockSpec((t, MLA_PAIR * width), lambda b, g: (b, g))
    return pl.pallas_call(
        functools.partial(_mla_prompt_kernel, tq=tq),
        grid=(n // t, MLA_HEADS // MLA_PAIR),
        in_specs=[pair(QK_NOPE), pair(QK_ROPE), pair(QK_NOPE + V_HEAD),
                  pl.BlockSpec((t, LANES), lambda b, g: (b, 0))],
        out_specs=pair(V_HEAD),
        out_shape=jax.ShapeDtypeStruct((n, MLA_WIDTH), BF16),
        scratch_shapes=[pltpu.VMEM((tq, tq), F32), pltpu.VMEM((MLA_PAIR, t, 2 * LANES), BF16)],
        compiler_params=_params("parallel", "parallel"),
        name="mla_prompt",
    )(qn, qr, kv, krd)


def _positions(tq, tk, q0, k0):
    qpos = lax.broadcasted_iota(jnp.int32, (tq, tk), 0) + q0
    kpos = lax.broadcasted_iota(jnp.int32, (tq, tk), 1) + k0
    return qpos, kpos


def _da_sample_kernel(q_ref, ck_ref, cv_ref, nk_ref, nv_ref, slope_ref, lq1_ref, lk1_ref, lq2_ref, lk2_ref,
                      subln_ref, o_ref, acc_ref, m_ref, l_ref, *, lam_init):
    c = pl.program_id(1)
    nc = pl.num_programs(1)
    tq = q_ref.shape[0]
    pc = ck_ref.shape[0] // DA_HEADS
    past = pc * nc

    @pl.when(c == 0)
    def _():
        m_ref[...] = jnp.full_like(m_ref, NEG_INF)
        l_ref[...] = jnp.zeros_like(l_ref)
        acc_ref[...] = jnp.zeros_like(acc_ref)

    def attend(keys, values, unit_bias):
        unit2 = jnp.concatenate([unit_bias, unit_bias], axis=0)
        scores = []
        for h in range(DA_HEADS):
            q2 = jnp.concatenate(_split_halves(q_ref[:, h * LANES:(h + 1) * LANES]), axis=0)
            scores.append(_dot_nt(q2, keys(h)) + unit2 * slope_ref[h][:, :1])
        probs = []
        for h, s in enumerate(scores):
            rows = slice(2 * h * tq, (2 * h + 2) * tq)
            m_old = m_ref[rows, :]
            m_new = jnp.maximum(m_old, jnp.max(s, axis=-1, keepdims=True))
            alpha = jnp.exp2(m_old - m_new)
            p = jnp.exp2(s - m_new)
            m_ref[rows, :] = m_new
            l_ref[rows, :] = alpha * l_ref[rows, :] + jnp.sum(p, axis=-1, keepdims=True)
            probs.append((alpha, p.astype(BF16)))
        for h, (alpha, p) in enumerate(probs):
            rows = slice(2 * h * tq, (2 * h + 2) * tq)
            acc_ref[rows, :] = alpha * acc_ref[rows, :] + _dot(p, values(h))

    def head_rows(h):
        return pl.ds(h, pc, stride=DA_HEADS)

    attend(lambda h: ck_ref[head_rows(h), :].astype(BF16), lambda h: cv_ref[head_rows(h), :].astype(BF16),
           _pos_bias(*_positions(tq, pc, past, c * pc), 1.0))

    @pl.when(c == nc - 1)
    def _():
        lam = _lambda(lq1_ref, lk1_ref, lq2_ref, lk2_ref, lam_init)
        attend(lambda h: nk_ref[:, h * LANES:(h + 1) * LANES], lambda h: nv_ref[:, h * LANES:(h + 1) * LANES],
               _pos_bias(*_positions(tq, tq, past, past), 1.0))
        for h in range(DA_HEADS):
            r1 = slice(2 * h * tq, (2 * h + 1) * tq)
            r2 = slice((2 * h + 1) * tq, (2 * h + 2) * tq)
            o = acc_ref[r1, :] / l_ref[r1, :] - acc_ref[r2, :] * (lam / l_ref[r2, :])
            o_ref[:, h * LANES:(h + 1) * LANES] = _da_finish(o, subln_ref[...], lam_init)


def _da_sample(q, cache_k, cache_v, nk, nv, slopes, lams, subln, *, lam_init, pc):
    bd, rows, _ = cache_k.shape
    past = rows // DA_HEADS
    pc = min(pc, past)
    assert past % pc == 0
    n = q.shape[0]
    tq = n // bd
    new = pl.BlockSpec((tq, DA_WIDTH), lambda b, c: (b, 0))
    cache = pl.BlockSpec((None, pc * DA_HEADS, LANES), lambda b, c: (b, c, 0))
    whole = lambda a: pl.BlockSpec(a.shape, lambda b, c: (0,) * a.ndim)
    return pl.pallas_call(
        functools.partial(_da_sample_kernel, lam_init=lam_init),
        grid=(bd, past // pc),
        in_specs=[new, cache, cache, new, new, whole(slopes)] + [whole(a) for a in lams] + [whole(subln)],
        out_specs=new,
        out_shape=jax.ShapeDtypeStruct((n, DA_WIDTH), BF16),
        scratch_shapes=[pltpu.VMEM((2 * DA_HEADS * tq, LANES), F32),
                        pltpu.VMEM((2 * DA_HEADS * tq, 1), F32),
                        pltpu.VMEM((2 * DA_HEADS * tq, 1), F32)],
        compiler_params=_params("parallel", "arbitrary"),
        name="da_sample",
    )(q, cache_k, cache_v, nk, nv, slopes, *lams, subln)


MLA_GROUP = 4


def _mla_sample_kernel(qn_ref, qr_ref, cc_ref, ckrd_ref, nc_ref, nkrd_ref, wukv_ref, o_ref):
    tq = qn_ref.shape[0]
    past = cc_ref.shape[0]
    cc = cc_ref[...].astype(BF16)
    nc = nc_ref[...].astype(BF16)
    rows = MLA_GROUP * tq
    masked = ((past + tq - 1) >> CHUNK_SHIFT) > (past >> CHUNK_SHIFT)
    if masked:
        qrow = lax.broadcasted_iota(jnp.int32, (rows, 1), 0) & (tq - 1)
        bias_c = _pos_bias(qrow + past, lax.broadcasted_iota(jnp.int32, (rows, past), 1), None)
        bias_n = _pos_bias(qrow + past, lax.broadcasted_iota(jnp.int32, (rows, tq), 1) + past, None)
    kv_cols = QK_NOPE + V_HEAD
    for g in range(MLA_HEADS // MLA_GROUP):
        heads = range(g * MLA_GROUP, (g + 1) * MLA_GROUP)
        qa = jnp.concatenate(
            [_dot_nt(qn_ref[:, h * QK_NOPE:(h + 1) * QK_NOPE],
                     wukv_ref[:, h * kv_cols:h * kv_cols + QK_NOPE]).astype(BF16) for h in heads], axis=0)
        qr = jnp.concatenate(
            [_select_rope(qr_ref[:, (h // 2) * LANES:(h // 2 + 1) * LANES], h) for h in heads], axis=0)
        sc = _dot_nt(qa, cc) + _dot_nt(qr, ckrd_ref[...])
        sn = _dot_nt(qa, nc) + _dot_nt(qr, nkrd_ref[...])
        if masked:
            sc, sn = sc + bias_c, sn + bias_n
        m = jnp.maximum(jnp.max(sc, axis=-1, keepdims=True), jnp.max(sn, axis=-1, keepdims=True))
        pc = jnp.exp2(sc - m)
        pn = jnp.exp2(sn - m)
        w = 1.0 / (jnp.sum(pc, axis=-1, keepdims=True) + jnp.sum(pn, axis=-1, keepdims=True))
        lat = ((_dot(pc.astype(BF16), cc) + _dot(pn.astype(BF16), nc)) * w).astype(BF16)
        for j, h in enumerate(heads):
            w_uv = wukv_ref[:, h * kv_cols + QK_NOPE:(h + 1) * kv_cols]
            o_ref[:, h * V_HEAD:(h + 1) * V_HEAD] = _dot(lat[j * tq:(j + 1) * tq, :], w_uv).astype(BF16)


def _mla_sample(qn, qr, cache_ckv, krd_cache, ckv_new, krd_new, w_ukv):
    bd, past, _ = cache_ckv.shape
    n = qn.shape[0]
    tq = n // bd
    assert tq & (tq - 1) == 0
    row = lambda b: (b, 0)
    return pl.pallas_call(
        _mla_sample_kernel,
        grid=(bd,),
        in_specs=[
            pl.BlockSpec((tq, MLA_HEADS * QK_NOPE), row),
            pl.BlockSpec((tq, MLA_HEADS * QK_ROPE), row),
            pl.BlockSpec((None, past, KV_LORA), lambda b: (b, 0, 0)),
            pl.BlockSpec((None, past, LANES), lambda b: (b, 0, 0)),
            pl.BlockSpec((tq, KV_LORA), row),
            pl.BlockSpec((tq, LANES), row),
            _resident(w_ukv.shape),
        ],
        out_specs=pl.BlockSpec((tq, MLA_WIDTH), row),
        out_shape=jax.ShapeDtypeStruct((n, MLA_WIDTH), BF16),
        compiler_params=_params("parallel"),
        name="mla_sample",
    )(qn, qr, cache_ckv, krd_cache, ckv_new, krd_new, w_ukv)


def _proj_out_kernel(x_ref, a_ref, b_ref, w_ref, o_ref):
    o_ref[...] = (x_ref[...] + _dot(a_ref[...], w_ref[:DA_WIDTH, :]) + _dot(b_ref[...], w_ref[DA_WIDTH:, :]))


def _proj_out(x, a_da, a_mla, w_out, *, tm):
    n, d = x.shape
    tm = min(tm, n)
    assert n % tm == 0
    row = lambda i: (i, 0)
    return pl.pallas_call(
        _proj_out_kernel,
        grid=(n // tm,),
        in_specs=[pl.BlockSpec((tm, d), row), pl.BlockSpec((tm, DA_WIDTH), row),
                  pl.BlockSpec((tm, MLA_WIDTH), row), _resident(w_out.shape)],
        out_specs=pl.BlockSpec((tm, d), row),
        out_shape=jax.ShapeDtypeStruct((n, d), F32),
        compiler_params=_params("parallel"),
        name="proj_out",
    )(x, a_da, a_mla, w_out)


def _prep_weights(w_in, mix_norm, q_norm, w_uq, kv_norm, w_ukv, w_out):
    w_in = w_in.astype(BF16)
    w_kr = w_in[:, 3 * DA_WIDTH + Q_LORA + KV_LORA:]
    w_uq = w_uq.astype(BF16).reshape(Q_LORA, MLA_HEADS, QK_HEAD)
    w_uqr = w_uq[:, :, QK_NOPE:]
    return {
        "mix_norm": mix_norm[None, :],
        "w_in": w_in,
        "w_kr": jnp.concatenate([w_kr, w_kr], axis=1),
        "q_norm": q_norm[None, :], "kv_norm": kv_norm[None, :],
        "w_uqn": w_uq[:, :, :QK_NOPE].reshape(Q_LORA, MLA_HEADS * QK_NOPE),
        "w_uqr": w_uqr.reshape(Q_LORA, MLA_HEADS * QK_ROPE),
        "w_ukv": w_ukv.astype(BF16),
        "w_out": w_out.astype(BF16),
    }


def _rope_tables(pos):
    half = QK_ROPE // 2
    inv = ROPE_THETA ** (-jnp.arange(half, dtype=F32) / half)
    ang = pos.astype(F32)[:, None] * inv[None, :]
    c = jnp.concatenate([jnp.cos(ang)] * 2, axis=1)
    s = jnp.concatenate([-jnp.sin(ang), jnp.sin(ang)], axis=1)
    return {"c2": jnp.tile(c, (1, 2)), "s2": jnp.tile(s, (1, 2)),
            "c8": jnp.tile(c, (1, MLA_HEADS)), "s8": jnp.tile(s, (1, MLA_HEADS))}


def _layer(x, pos, past, ffn1, ffn2, wmix, slopes, lams, subln, final_norm, lam_init, *, batch):
    n, d = x.shape
    t = n // batch

    def swiglu(x, ffn, *, last):
        casting = ffn[1].dtype == F32
        out = _ffn(x, *ffn, final_norm, final_norm=last, tm=FFN_ROWS,
                   tf=FFN_COLS_CASTING if casting else FFN_COLS)
        return (out[0], (ffn[0],) + out[1:]) if casting else (out, ffn)

    x, ffn1 = swiglu(x, ffn1, last=False)
    tab_pos = pos if t % PROJ_IN_ROWS == 0 else jnp.tile(pos, batch)
    (q, k_f, k_b, v_f, v_b, ckv, kr, krd, qn, qr, kv) = _proj_in(x, wmix, _rope_tables(tab_pos), tm=PROJ_IN_ROWS)
    if past is None:
        a_da = _da_prompt(q, k_b, v_b, slopes, lams, subln, t=t, tq=QUERY_TILE, lam_init=lam_init)
        a_mla = _mla_prompt(qn, qr, kv, krd, t=t, tq=QUERY_TILE)
    else:
        cache_k, cache_v, cache_ckv, cache_kr = past
        p = cache_k.size // (batch * DA_WIDTH)
        a_da = _da_sample(q, cache_k.reshape(batch, p * DA_HEADS, DA_HEAD_DIM),
                          cache_v.reshape(batch, p * DA_HEADS, DA_HEAD_DIM),
                          k_b, v_b, slopes, lams, subln, lam_init=lam_init, pc=CACHE_CHUNK)
        cache_kr = cache_kr.reshape(batch, p, QK_ROPE)
        krd_cache = jnp.concatenate([cache_kr, cache_kr], axis=-1).astype(BF16)
        a_mla = _mla_sample(qn, qr, cache_ckv.reshape(batch, p, KV_LORA), krd_cache, ckv, krd, wmix["w_ukv"])
    x = _proj_out(x, a_da, a_mla, wmix["w_out"], tm=PROJ_OUT_ROWS)
    y, ffn2 = swiglu(x, ffn2, last=True)
    return y, (k_f, v_f, ckv, kr), ffn1, ffn2


def kernel(x_prompt, x_sample, cache_da_k, cache_da_v, cache_mla_ckv, cache_mla_krope, ffn1_norm, ffn1_w_gate, ffn1_w_up, ffn1_w_down, mix_norm, w_in, da_lambda_q1, da_lambda_k1, da_lambda_q2, da_lambda_k2, da_subln, mla_q_norm, mla_w_uq, mla_kv_norm, mla_w_ukv, w_out, ffn2_norm, ffn2_w_gate, ffn2_w_up, ffn2_w_down, final_norm):
    depth = w_in.shape[0]
    assert depth == 1, "single-layer trunk"
    b, t, d = x_prompt.shape
    bd, td, _ = x_sample.shape
    past_len = cache_da_k.shape[2]
    l = 0
    lam_init = 0.8 - 0.6 * math.exp(-0.3 * l)

    def one_layer(w):
        return w.reshape(w.shape[1:])

    ffn1 = (ffn1_norm[l][None, :], one_layer(ffn1_w_gate), one_layer(ffn1_w_up), one_layer(ffn1_w_down))
    ffn2 = (ffn2_norm[l][None, :], one_layer(ffn2_w_gate), one_layer(ffn2_w_up), one_layer(ffn2_w_down))
    wmix = _prep_weights(w_in[l], mix_norm[l], mla_q_norm[l], mla_w_uq[l], mla_kv_norm[l], mla_w_ukv[l],
                         w_out[l])
    slopes = jnp.asarray(np.broadcast_to(ALIBI_SLOPES[:, None, None], (DA_HEADS, 1, LANES)))
    lams = (da_lambda_q1[l][None, :], da_lambda_k1[l][None, :], da_lambda_q2[l][None, :],
            da_lambda_k2[l][None, :])
    subln = da_subln[l][None, :]
    fnorm = final_norm[None, :]

    past = (cache_da_k, cache_da_v, cache_mla_ckv, cache_mla_krope)
    assert bd * td <= 1024, "the running streams must fit one SwiGLU token tile"
    y_s, st_s, ffn1, ffn2 = _layer(x_sample.reshape(bd * td, d), past_len + jnp.arange(td, dtype=jnp.int32),
                                   past, ffn1, ffn2, wmix, slopes, lams, subln, fnorm, lam_init, batch=bd)
    y_p, st_p, _, _ = _layer(x_prompt.reshape(b * t, d), jnp.arange(t, dtype=jnp.int32), None,
                             ffn1, ffn2, wmix, slopes, lams, subln, fnorm, lam_init, batch=b)

    def state(st, nb, nt):
        k_f, v_f, ckv, kr = st
        return (k_f.reshape(1, nb, nt, DA_HEADS, DA_HEAD_DIM), v_f.reshape(1, nb, nt, DA_HEADS, DA_HEAD_DIM),
                ckv.reshape(1, nb, nt, KV_LORA), kr.reshape(1, nb, nt, QK_ROPE))

    return (y_p.reshape(b, t, d), y_s.reshape(bd, td, d)) + state(st_p, b, t) + state(st_s, bd, td)
```

```python
import functools
import math

import jax
import jax.numpy as jnp
import numpy as np
from jax import lax
from jax.experimental import pallas as pl
from jax.experimental.pallas import tpu as pltpu

F32 = jnp.float32
BF16 = jnp.bfloat16

CHUNK = 64
CHUNK_SHIFT = 6
assert 1 << CHUNK_SHIFT == CHUNK
EPS = 1e-6
NEG_INF = -1e30
DA_HEADS = 8
DA_HALF = 64
DA_HEAD_DIM = 2 * DA_HALF
DA_WIDTH = DA_HEADS * DA_HEAD_DIM
MLA_HEADS = 8
Q_LORA = 512
KV_LORA = 256
QK_NOPE = 128
QK_ROPE = 64
QK_HEAD = QK_NOPE + QK_ROPE
V_HEAD = 128
MLA_WIDTH = MLA_HEADS * V_HEAD
ROPE_THETA = 10000.0
LOG2E = math.log2(math.e)
ALIBI_SLOPES = np.array([2.0 ** (-8.0 * (i + 1) / DA_HEADS) for i in range(DA_HEADS)], np.float32)
LANES = 128
assert DA_HEAD_DIM == LANES and QK_NOPE == LANES and V_HEAD == LANES and 2 * QK_ROPE == LANES

MIB = 1024 * 1024
VMEM_LIMIT_BYTES = 56 * MIB
FFN_VMEM_LIMIT_BYTES = 62 * MIB
FFN_ROWS = 1024
FFN_COLS = 512
FFN_COLS_CASTING = 256
PROJ_IN_ROWS = 256
PROJ_OUT_ROWS = 512
QUERY_TILE = 256
CACHE_CHUNK = 2048


def _params(*semantics, vmem_limit_bytes=VMEM_LIMIT_BYTES):
    return pltpu.CompilerParams(dimension_semantics=semantics, vmem_limit_bytes=vmem_limit_bytes)


def _rms(x, g):
    return x * lax.rsqrt(jnp.mean(x * x, axis=-1, keepdims=True) + EPS) * g


def _dot(a, b):
    return jnp.dot(a, b, preferred_element_type=F32)


def _dot_nt(a, b):
    return lax.dot_general(a, b, (((1,), (1,)), ((), ())), preferred_element_type=F32)


def _resident(shape):
    return pl.BlockSpec(shape, lambda *_: (0,) * len(shape), pipeline_mode=pl.Buffered(1))


def _ffn_kernel(x_ref, g_ref, wg_ref, wu_ref, wd_ref, fn_ref, o_ref, *rest, final_norm, nj, emit_bf16):
    j = pl.program_id(1)
    xn_ref = rest[-1]

    @pl.when(j == 0)
    def _():
        xn_ref[...] = _rms(x_ref[...], g_ref[...]).astype(BF16)

    def weights():
        if not emit_bf16:
            return wg_ref[...], wu_ref[...], wd_ref[...]
        blocks = []
        for w_ref, wb_ref in zip((wg_ref, wu_ref, wd_ref), rest[:3]):
            wb = w_ref[...].astype(BF16)
            wb_ref[...] = wb
            blocks.append(wb)
        return blocks

    def half_down():
        xn = xn_ref[...]
        wg, wu, wd = weights()
        gate = _dot(xn, wg)
        up = _dot(xn, wu)
        h = (gate * jax.nn.sigmoid(gate) * (0.5 * up)).astype(BF16)
        return _dot(h, wd)

    def finish(y):
        o_ref[...] = _rms(y, fn_ref[...]) if final_norm else y

    if nj == 1:
        finish(x_ref[...] + half_down())
        return

    @pl.when(j == 0)
    def _():
        o_ref[...] = x_ref[...] + half_down()

    @pl.when(jnp.logical_and(j > 0, j < nj - 1))
    def _():
        o_ref[...] += half_down()

    @pl.when(j == nj - 1)
    def _():
        finish(o_ref[...] + half_down())


def _ffn(x, norm, wg, wu, wd, fnorm, *, final_norm, tm, tf):
    n, d = x.shape
    f = wg.shape[1]
    tm = min(tm, n)
    tf = min(tf, f)
    assert n % tm == 0 and f % tf == 0
    emit_bf16 = wg.dtype == F32
    assert not emit_bf16 or n == tm
    col_block = pl.BlockSpec((d, tf), lambda i, j: (0, j))
    row_block = pl.BlockSpec((tf, d), lambda i, j: (j, 0))
    y_spec = pl.BlockSpec((tm, d), lambda i, j: (i, 0))
    y_shape = jax.ShapeDtypeStruct((n, d), F32)
    out = pl.pallas_call(
        functools.partial(_ffn_kernel, final_norm=final_norm, nj=f // tf, emit_bf16=emit_bf16),
        grid=(n // tm, f // tf),
        in_specs=[
            pl.BlockSpec((tm, d), lambda i, j: (i, 0)),
            pl.BlockSpec((1, d), lambda i, j: (0, 0)),
            col_block, col_block, row_block,
            pl.BlockSpec((1, d), lambda i, j: (0, 0)),
        ],
        out_specs=[y_spec, col_block, col_block, row_block] if emit_bf16 else y_spec,
        out_shape=([y_shape] + [jax.ShapeDtypeStruct(w.shape, BF16) for w in (wg, wu, wd)]
                   if emit_bf16 else y_shape),
        scratch_shapes=[pltpu.VMEM((tm, d), BF16)],
        compiler_params=_params("parallel", "arbitrary", vmem_limit_bytes=FFN_VMEM_LIMIT_BYTES),
        name="ffn",
    )(x, norm, wg, wu, wd, fnorm)
    return tuple(out) if emit_bf16 else out


def _store_by_head(ref, z):
    tokens = z.shape[0]
    for h in range(DA_HEADS):
        ref[pl.ds(h, tokens, stride=DA_HEADS), :] = z[:, h * DA_HEAD_DIM:(h + 1) * DA_HEAD_DIM]


def _swap_rope_halves(x):
    n = x.shape[1]
    half = QK_ROPE // 2
    lane = lax.broadcasted_iota(jnp.int32, x.shape, 1)
    in_first_half = (lane & (QK_ROPE - 1)) < half
    return jnp.where(in_first_half, pltpu.roll(x, n - half, axis=1), pltpu.roll(x, half, axis=1))


def _proj_in_kernel(x_ref, g_ref, win_ref, wkr_ref,
                    qn_ref, kvn_ref, wuqn_ref, wuqr_ref, wukv_ref,
                    c2_ref, s2_ref, c8_ref, s8_ref,
                    q_ref, kf_ref, kb_ref, vf_ref, vb_ref, ckv_ref, kr_ref, krd_ref,
                    qnope_ref, qrope_ref, kv_ref):
    h = _rms(x_ref[...], g_ref[...]).astype(BF16)
    o1, o2, o3 = DA_WIDTH, 2 * DA_WIDTH, 3 * DA_WIDTH
    o4, o5 = o3 + Q_LORA, o3 + Q_LORA + KV_LORA
    q_ref[...] = (_dot(h, win_ref[:, :o1]) * (DA_HALF ** -0.5 * LOG2E)).astype(BF16)
    k = _dot(h, win_ref[:, o1:o2])
    _store_by_head(kf_ref, k)
    kb_ref[...] = k.astype(BF16)
    v = _dot(h, win_ref[:, o2:o3])
    _store_by_head(vf_ref, v)
    vb_ref[...] = v.astype(BF16)
    cq = _rms(_dot(h, win_ref[:, o3:o4]), qn_ref[...]).astype(BF16)
    ckv = _rms(_dot(h, win_ref[:, o4:o5]), kvn_ref[...])
    ckv_ref[...] = ckv
    kr = _dot(h, wkr_ref[...])
    krd = kr * c2_ref[...] + _swap_rope_halves(kr) * s2_ref[...]
    kr_ref[...] = krd[:, :QK_ROPE]
    krd_ref[...] = krd.astype(BF16)
    mla_scale = QK_HEAD ** -0.5 * LOG2E
    qnope_ref[...] = (_dot(cq, wuqn_ref[...]) * mla_scale).astype(BF16)
    qr = _dot(cq, wuqr_ref[...])
    qr = qr * c8_ref[...] + _swap_rope_halves(qr) * s8_ref[...]
    qrope_ref[...] = (qr * mla_scale).astype(BF16)
    kv_ref[...] = _dot(ckv.astype(BF16), wukv_ref[...]).astype(BF16)


def _proj_in(x, w, tabs, *, tm):
    n, d = x.shape
    tm = min(tm, n)
    assert n % tm == 0
    t_rows = tabs["c2"].shape[0]
    assert t_rows % tm == 0 or tm % t_rows == 0
    if tm > t_rows:
        tm = t_rows
    nt = t_rows // tm

    def row(i):
        return (i, 0)

    def trow(i):
        return (i % nt, 0)

    weights = [w["w_in"], w["w_kr"], w["q_norm"], w["kv_norm"], w["w_uqn"], w["w_uqr"], w["w_ukv"]]
    tables = [tabs["c2"], tabs["s2"], tabs["c8"], tabs["s8"]]
    outs = [
        ((n, DA_WIDTH), BF16),
        ((n * DA_HEADS, DA_HEAD_DIM), F32),
        ((n, DA_WIDTH), BF16),
        ((n * DA_HEADS, DA_HEAD_DIM), F32),
        ((n, DA_WIDTH), BF16),
        ((n, KV_LORA), F32),
        ((n, QK_ROPE), F32),
        ((n, LANES), BF16),
        ((n, MLA_HEADS * QK_NOPE), BF16),
        ((n, MLA_HEADS * QK_ROPE), BF16),
        ((n, MLA_HEADS * (QK_NOPE + V_HEAD)), BF16),
    ]
    return pl.pallas_call(
        _proj_in_kernel,
        grid=(n // tm,),
        in_specs=([pl.BlockSpec((tm, d), row), _resident((1, d))]
                  + [_resident(a.shape) for a in weights]
                  + [pl.BlockSpec((tm, a.shape[1]), trow) for a in tables]),
        out_specs=[pl.BlockSpec((tm * s[0] // n, s[1]), row) for s, _ in outs],
        out_shape=[jax.ShapeDtypeStruct(s, dt) for s, dt in outs],
        compiler_params=_params("parallel"),
        name="proj_in",
    )(x, w["mix_norm"], *weights, *tables)


def _lambda(lq1_ref, lk1_ref, lq2_ref, lk2_ref, lam_init):
    a = jnp.sum(lq1_ref[...] * lk1_ref[...], axis=-1, keepdims=True)
    b = jnp.sum(lq2_ref[...] * lk2_ref[...], axis=-1, keepdims=True)
    return jnp.exp(a) - jnp.exp(b) + lam_init


def _pos_bias(qpos, kpos, slope):
    visible = lax.shift_right_arithmetic(kpos, CHUNK_SHIFT) <= lax.shift_right_arithmetic(qpos, CHUNK_SHIFT)
    if slope is None:
        return jnp.where(visible, 0.0, NEG_INF)
    dist = jnp.abs(qpos - kpos).astype(F32)
    return jnp.where(visible, (-LOG2E * slope) * dist, NEG_INF)


def _split_halves(q):
    lane = lax.broadcasted_iota(jnp.int32, q.shape, 1)
    zero = jnp.zeros_like(q)
    return jnp.where(lane < DA_HALF, q, zero), jnp.where(lane >= DA_HALF, q, zero)


def _da_finish(o, subln, lam_init):
    return (_rms(o, subln) * (1.0 - lam_init)).astype(BF16)


BIAS_TERMS = 3


def _da_prompt_kernel(q_ref, k_ref, v_ref, tab_ref, slope_ref, lq1_ref, lk1_ref, lq2_ref, lk2_ref, subln_ref,
                      o_ref, diag_ref, k1_ref, k2_ref, vt_ref, *, tq, lam_init):
    t = q_ref.shape[0]
    lam = _lambda(lq1_ref, lk1_ref, lq2_ref, lk2_ref, lam_init)
    c = LOG2E * slope_ref[0][:, :1]

    lane = lax.broadcasted_iota(jnp.int32, (t, LANES), 1)
    k1_ref[...] = jnp.where(lane < DA_HALF, k_ref[...], tab_ref[0])
    k2_ref[...] = jnp.where(lane >= DA_HALF, k_ref[...], tab_ref[1])
    vt_ref[...] = v_ref[...].T

    kpos = lax.broadcasted_iota(jnp.int32, (tq, tq), 0)
    qpos = lax.broadcasted_iota(jnp.int32, (tq, tq), 1)
    visible = lax.shift_right_arithmetic(kpos, CHUNK_SHIFT) <= lax.shift_right_arithmetic(qpos, CHUNK_SHIFT)
    ahead = jnp.maximum(kpos - qpos, 0).astype(F32)
    diag_ref[...] = jnp.where(visible, (-2.0 * c) * ahead, NEG_INF)

    qlane = lax.broadcasted_iota(jnp.int32, (tq, LANES), 1)
    ones_hi = ((qlane >= DA_HALF) & (qlane < DA_HALF + BIAS_TERMS)).astype(F32).astype(BF16)
    ones_lo = (qlane < BIAS_TERMS).astype(F32).astype(BF16)

    def scores(qi):
        lo, ext = qi * tq, (qi + 1) * tq
        q = q_ref[lo:ext, :]
        q1 = jnp.where(qlane < DA_HALF, q, ones_hi)
        q2 = jnp.where(qlane >= DA_HALF, q, ones_lo)
        diag = (_dot_nt(k1_ref[lo:ext, :], q1) + diag_ref[...], _dot_nt(k2_ref[lo:ext, :], q2) + diag_ref[...])
        past = (_dot_nt(k1_ref[0:lo, :], q1), _dot_nt(k2_ref[0:lo, :], q2)) if qi > 0 else None
        return diag, past

    def softmax(sd, sp):
        m = jnp.max(sd, axis=0, keepdims=True)
        if sp is None:
            pd = jnp.exp2(sd - m)
            return pd, None, jnp.sum(pd, axis=0, keepdims=True)
        m = jnp.maximum(m, jnp.max(sp, axis=0, keepdims=True))
        pd = jnp.exp2(sd - m)
        pp = jnp.exp2(sp - m)
        return pd, pp, jnp.sum(pd, axis=0, keepdims=True) + jnp.sum(pp, axis=0, keepdims=True)

    nq = t // tq
    s_next = scores(0)
    for qi in range(nq):
        lo, ext = qi * tq, (qi + 1) * tq
        diag, past = s_next
        if qi + 1 < nq:
            s_next = scores(qi + 1)
        pd1, pp1, l1 = softmax(diag[0], None if past is None else past[0])
        pd2, pp2, l2 = softmax(diag[1], None if past is None else past[1])
        r = lam * l1 / l2
        ot = _dot(vt_ref[:, lo:ext], (pd1 - pd2 * r).astype(BF16))
        if past is not None:
            ot = ot + _dot(vt_ref[:, 0:lo], (pp1 - pp2 * r).astype(BF16))
        ot = ot * (1.0 / l1)
        ot = ot * lax.rsqrt(jnp.mean(ot * ot, axis=0, keepdims=True) + EPS) * (subln_ref[...] * (1.0 - lam_init))
        o_ref[lo:ext, :] = ot.T.astype(BF16)


def _alibi_key_tables(t):
    x = (np.float32(LOG2E) * ALIBI_SLOPES)[:, None] * np.arange(t, dtype=np.float32)[None, :]
    pieces = []
    for _ in range(BIAS_TERMS):
        piece = x.astype(BF16)
        pieces.append(piece)
        x = x - piece.astype(np.float32)
    terms = jnp.asarray(np.stack(pieces, axis=-1))
    pad = lambda before: jnp.pad(terms, ((0, 0), (0, 0), (before, LANES - BIAS_TERMS - before)))
    return jnp.stack([pad(DA_HALF), pad(0)], axis=1)


def _da_prompt(q, k, v, slopes, lams, subln, *, t, tq, lam_init):
    n = q.shape[0]
    subln = subln.reshape(DA_HEAD_DIM, 1)
    tq = min(tq, t)
    blk = pl.BlockSpec((t, LANES), lambda b, h: (b, h))
    vec = lambda a: pl.BlockSpec(a.shape, lambda b, h: (0, 0))
    return pl.pallas_call(
        functools.partial(_da_prompt_kernel, tq=tq, lam_init=lam_init),
        grid=(n // t, DA_HEADS),
        in_specs=[blk, blk, blk, pl.BlockSpec((None, 2, t, LANES), lambda b, h: (h, 0, 0, 0)),
                  pl.BlockSpec((1, 1, LANES), lambda b, h: (h, 0, 0))]
                 + [vec(a) for a in lams] + [vec(subln)],
        out_specs=blk,
        out_shape=jax.ShapeDtypeStruct((n, DA_WIDTH), BF16),
        scratch_shapes=[pltpu.VMEM((tq, tq), F32), pltpu.VMEM((t, LANES), BF16), pltpu.VMEM((t, LANES), BF16),
                        pltpu.VMEM((LANES, t), BF16)],
        compiler_params=_params("parallel", "parallel"),
        name="da_prompt",
    )(q, k, v, _alibi_key_tables(t), slopes, *lams, subln)


def _select_rope(qr, h):
    lane = lax.broadcasted_iota(jnp.int32, qr.shape, 1)
    mine = lax.shift_right_logical(lane, CHUNK_SHIFT) == (h % 2)
    return jnp.where(mine, qr, jnp.zeros_like(qr))


MLA_PAIR = 2


def _mla_prompt_kernel(qn_ref, qr_ref, kv_ref, krd_ref, o_ref, mask_ref, kfull_ref, vt_ref, *, tq):
    t = qn_ref.shape[0]
    kv_cols = QK_NOPE + V_HEAD
    kpos = lax.broadcasted_iota(jnp.int32, (tq, tq), 0)
    qpos = lax.broadcasted_iota(jnp.int32, (tq, tq), 1)
    mask_ref[...] = _pos_bias(qpos, kpos, None)
    for hh in range(MLA_PAIR):
        kfull_ref[hh, :, :LANES] = kv_ref[:, hh * kv_cols:hh * kv_cols + QK_NOPE]
        kfull_ref[hh, :, LANES:] = krd_ref[...]
        vt_ref[hh] = kv_ref[:, hh * kv_cols + QK_NOPE:(hh + 1) * kv_cols].T

    def scores(hh, qi):
        lo, ext = qi * tq, (qi + 1) * tq
        rope_block = qr_ref[lo:ext, (hh // 2) * LANES:(hh // 2 + 1) * LANES]
        q = jnp.concatenate([qn_ref[lo:ext, hh * LANES:(hh + 1) * LANES], _select_rope(rope_block, hh)], axis=1)
        sd = _dot_nt(kfull_ref[hh, lo:ext, :], q) + mask_ref[...]
        sp = _dot_nt(kfull_ref[hh, 0:lo, :], q) if qi > 0 else None
        return sd, sp

    def probabilities(sd, sp):
        m = jnp.max(sd, axis=0, keepdims=True)
        if sp is None:
            pd = jnp.exp2(sd - m)
            return pd.astype(BF16), None, jnp.sum(pd, axis=0, keepdims=True)
        m = jnp.maximum(m, jnp.max(sp, axis=0, keepdims=True))
        pd = jnp.exp2(sd - m)
        pp = jnp.exp2(sp - m)
        l = jnp.sum(pd, axis=0, keepdims=True) + jnp.sum(pp, axis=0, keepdims=True)
        return pd.astype(BF16), pp.astype(BF16), l

    def weighted_values(hh, qi, pd, pp, l):
        lo, ext = qi * tq, (qi + 1) * tq
        ot = _dot(vt_ref[hh, :, lo:ext], pd)
        if pp is not None:
            ot = ot + _dot(vt_ref[hh, :, 0:lo], pp)
        o_ref[lo:ext, hh * LANES:(hh + 1) * LANES] = (ot * (1.0 / l)).T.astype(BF16)

    tiles = [(hh, qi) for hh in range(MLA_PAIR) for qi in range(t // tq)]
    s_next = scores(*tiles[0])
    pending = None
    for i, tile in enumerate(tiles):
        sd, sp = s_next
        if i + 1 < len(tiles):
            s_next = scores(*tiles[i + 1])
        if pending is not None:
            weighted_values(*tiles[i - 1], *pending)
        pending = probabilities(sd, sp)
    weighted_values(*tiles[-1], *pending)


def _mla_prompt(qn, qr, kv, krd, *, t, tq):
    n = qn.shape[0]
    tq = min(tq, t)
    pair = lambda width: pl.BlockSpec((t, MLA_PAIR * width), lambda b, g: (b, g))
    return pl.pallas_call(
        functools.partial(_mla_prompt_kernel, tq=tq),
        grid=(n // t, MLA_HEADS // MLA_PAIR),
        in_specs=[pair(QK_NOPE), pair(QK_ROPE), pair(QK_NOPE + V_HEAD),
                  pl.BlockSpec((t, LANES), lambda b, g: (b, 0))],
        out_specs=pair(V_HEAD),
        out_shape=jax.ShapeDtypeStruct((n, MLA_WIDTH), BF16),
        scratch_shapes=[pltpu.VMEM((tq, tq), F32), pltpu.VMEM((MLA_PAIR, t, 2 * LANES), BF16),
                        pltpu.VMEM((MLA_PAIR, V_HEAD, t), BF16)],
        compiler_params=_params("parallel", "parallel"),
        name="mla_prompt",
    )(qn, qr, kv, krd)


def _positions(tq, tk, q0, k0):
    qpos = lax.broadcasted_iota(jnp.int32, (tq, tk), 0) + q0
    kpos = lax.broadcasted_iota(jnp.int32, (tq, tk), 1) + k0
    return qpos, kpos


def _da_sample_kernel(q_ref, ck_ref, cv_ref, nk_ref, nv_ref, slope_ref, lq1_ref, lk1_ref, lq2_ref, lk2_ref,
                      subln_ref, o_ref, acc_ref, m_ref, l_ref, *, lam_init):
    c = pl.program_id(1)
    nc = pl.num_programs(1)
    tq = q_ref.shape[0]
    pc = ck_ref.shape[0] // DA_HEADS
    past = pc * nc

    @pl.when(c == 0)
    def _():
        m_ref[...] = jnp.full_like(m_ref, NEG_INF)
        l_ref[...] = jnp.zeros_like(l_ref)
        acc_ref[...] = jnp.zeros_like(acc_ref)

    def attend(keys, values, unit_bias):
        unit2 = jnp.concatenate([unit_bias, unit_bias], axis=0)
        scores = []
        for h in range(DA_HEADS):
            q2 = jnp.concatenate(_split_halves(q_ref[:, h * LANES:(h + 1) * LANES]), axis=0)
            scores.append(_dot_nt(q2, keys(h)) + unit2 * slope_ref[h][:, :1])
        probs = []
        for h, s in enumerate(scores):
            rows = slice(2 * h * tq, (2 * h + 2) * tq)
            m_old = m_ref[rows, :]
            m_new = jnp.maximum(m_old, jnp.max(s, axis=-1, keepdims=True))
            alpha = jnp.exp2(m_old - m_new)
            p = jnp.exp2(s - m_new)
            m_ref[rows, :] = m_new
            l_ref[rows, :] = alpha * l_ref[rows, :] + jnp.sum(p, axis=-1, keepdims=True)
            probs.append((alpha, p.astype(BF16)))
        for h, (alpha, p) in enumerate(probs):
            rows = slice(2 * h * tq, (2 * h + 2) * tq)
            acc_ref[rows, :] = alpha * acc_ref[rows, :] + _dot(p, values(h))

    def head_rows(h):
        return pl.ds(h, pc, stride=DA_HEADS)

    attend(lambda h: ck_ref[head_rows(h), :].astype(BF16), lambda h: cv_ref[head_rows(h), :].astype(BF16),
           _pos_bias(*_positions(tq, pc, past, c * pc), 1.0))

    @pl.when(c == nc - 1)
    def _():
        lam = _lambda(lq1_ref, lk1_ref, lq2_ref, lk2_ref, lam_init)
        attend(lambda h: nk_ref[:, h * LANES:(h + 1) * LANES], lambda h: nv_ref[:, h * LANES:(h + 1) * LANES],
               _pos_bias(*_positions(tq, tq, past, past), 1.0))
        for h in range(DA_HEADS):
            r1 = slice(2 * h * tq, (2 * h + 1) * tq)
            r2 = slice((2 * h + 1) * tq, (2 * h + 2) * tq)
            o = acc_ref[r1, :] / l_ref[r1, :] - acc_ref[r2, :] * (lam / l_ref[r2, :])
            o_ref[:, h * LANES:(h + 1) * LANES] = _da_finish(o, subln_ref[...], lam_init)


def _da_sample(q, cache_k, cache_v, nk, nv, slopes, lams, subln, *, lam_init, pc):
    bd, rows, _ = cache_k.shape
    past = rows // DA_HEADS
    pc = min(pc, past)
    assert past % pc == 0
    n = q.shape[0]
    tq = n // bd
    new = pl.BlockSpec((tq, DA_WIDTH), lambda b, c: (b, 0))
    cache = pl.BlockSpec((None, pc * DA_HEADS, LANES), lambda b, c: (b, c, 0))
    whole = lambda a: pl.BlockSpec(a.shape, lambda b, c: (0,) * a.ndim)
    return pl.pallas_call(
        functools.partial(_da_sample_kernel, lam_init=lam_init),
        grid=(bd, past // pc),
        in_specs=[new, cache, cache, new, new, whole(slopes)] + [whole(a) for a in lams] + [whole(subln)],
        out_specs=new,
        out_shape=jax.ShapeDtypeStruct((n, DA_WIDTH), BF16),
        scratch_shapes=[pltpu.VMEM((2 * DA_HEADS * tq, LANES), F32),
                        pltpu.VMEM((2 * DA_HEADS * tq, 1), F32),
                        pltpu.VMEM((2 * DA_HEADS * tq, 1), F32)],
        compiler_params=_params("parallel", "arbitrary"),
        name="da_sample",
    )(q, cache_k, cache_v, nk, nv, slopes, *lams, subln)


MLA_GROUP = 4


def _mla_sample_kernel(qn_ref, qr_ref, cc_ref, ckrd_ref, nc_ref, nkrd_ref, wukv_ref, o_ref):
    tq = qn_ref.shape[0]
    past = cc_ref.shape[0]
    cc = cc_ref[...].astype(BF16)
    nc = nc_ref[...].astype(BF16)
    rows = MLA_GROUP * tq
    masked = ((past + tq - 1) >> CHUNK_SHIFT) > (past >> CHUNK_SHIFT)
    if masked:
        qrow = lax.broadcasted_iota(jnp.int32, (rows, 1), 0) & (tq - 1)
        bias_c = _pos_bias(qrow + past, lax.broadcasted_iota(jnp.int32, (rows, past), 1), None)
        bias_n = _pos_bias(qrow + past, lax.broadcasted_iota(jnp.int32, (rows, tq), 1) + past, None)
    kv_cols = QK_NOPE + V_HEAD
    for g in range(MLA_HEADS // MLA_GROUP):
        heads = range(g * MLA_GROUP, (g + 1) * MLA_GROUP)
        qa = jnp.concatenate(
            [_dot_nt(qn_ref[:, h * QK_NOPE:(h + 1) * QK_NOPE],
                     wukv_ref[:, h * kv_cols:h * kv_cols + QK_NOPE]).astype(BF16) for h in heads], axis=0)
        qr = jnp.concatenate(
            [_select_rope(qr_ref[:, (h // 2) * LANES:(h // 2 + 1) * LANES], h) for h in heads], axis=0)
        sc = _dot_nt(qa, cc) + _dot_nt(qr, ckrd_ref[...])
        sn = _dot_nt(qa, nc) + _dot_nt(qr, nkrd_ref[...])
        if masked:
            sc, sn = sc + bias_c, sn + bias_n
        m = jnp.maximum(jnp.max(sc, axis=-1, keepdims=True), jnp.max(sn, axis=-1, keepdims=True))
        pc = jnp.exp2(sc - m)
        pn = jnp.exp2(sn - m)
        w = 1.0 / (jnp.sum(pc, axis=-1, keepdims=True) + jnp.sum(pn, axis=-1, keepdims=True))
        lat = ((_dot(pc.astype(BF16), cc) + _dot(pn.astype(BF16), nc)) * w).astype(BF16)
        for j, h in enumerate(heads):
            w_uv = wukv_ref[:, h * kv_cols + QK_NOPE:(h + 1) * kv_cols]
            o_ref[:, h * V_HEAD:(h + 1) * V_HEAD] = _dot(lat[j * tq:(j + 1) * tq, :], w_uv).astype(BF16)


def _mla_sample(qn, qr, cache_ckv, krd_cache, ckv_new, krd_new, w_ukv):
    bd, past, _ = cache_ckv.shape
    n = qn.shape[0]
    tq = n // bd
    assert tq & (tq - 1) == 0
    row = lambda b: (b, 0)
    return pl.pallas_call(
        _mla_sample_kernel,
        grid=(bd,),
        in_specs=[
            pl.BlockSpec((tq, MLA_HEADS * QK_NOPE), row),
            pl.BlockSpec((tq, MLA_HEADS * QK_ROPE), row),
            pl.BlockSpec((None, past, KV_LORA), lambda b: (b, 0, 0)),
            pl.BlockSpec((None, past, LANES), lambda b: (b, 0, 0)),
            pl.BlockSpec((tq, KV_LORA), row),
            pl.BlockSpec((tq, LANES), row),
            _resident(w_ukv.shape),
        ],
        out_specs=pl.BlockSpec((tq, MLA_WIDTH), row),
        out_shape=jax.ShapeDtypeStruct((n, MLA_WIDTH), BF16),
        compiler_params=_params("parallel"),
        name="mla_sample",
    )(qn, qr, cache_ckv, krd_cache, ckv_new, krd_new, w_ukv)


def _proj_out_kernel(x_ref, a_ref, b_ref, w_ref, o_ref):
    o_ref[...] = (x_ref[...] + _dot(a_ref[...], w_ref[:DA_WIDTH, :]) + _dot(b_ref[...], w_ref[DA_WIDTH:, :]))


def _proj_out(x, a_da, a_mla, w_out, *, tm):
    n, d = x.shape
    tm = min(tm, n)
    assert n % tm == 0
    row = lambda i: (i, 0)
    return pl.pallas_call(
        _proj_out_kernel,
        grid=(n // tm,),
        in_specs=[pl.BlockSpec((tm, d), row), pl.BlockSpec((tm, DA_WIDTH), row),
                  pl.BlockSpec((tm, MLA_WIDTH), row), _resident(w_out.shape)],
        out_specs=pl.BlockSpec((tm, d), row),
        out_shape=jax.ShapeDtypeStruct((n, d), F32),
        compiler_params=_params("parallel"),
        name="proj_out",
    )(x, a_da, a_mla, w_out)


def _prep_weights(w_in, mix_norm, q_norm, w_uq, kv_norm, w_ukv, w_out):
    w_in = w_in.astype(BF16)
    w_kr = w_in[:, 3 * DA_WIDTH + Q_LORA + KV_LORA:]
    w_uq = w_uq.astype(BF16).reshape(Q_LORA, MLA_HEADS, QK_HEAD)
    w_uqr = w_uq[:, :, QK_NOPE:]
    return {
        "mix_norm": mix_norm[None, :],
        "w_in": w_in,
        "w_kr": jnp.concatenate([w_kr, w_kr], axis=1),
        "q_norm": q_norm[None, :], "kv_norm": kv_norm[None, :],
        "w_uqn": w_uq[:, :, :QK_NOPE].reshape(Q_LORA, MLA_HEADS * QK_NOPE),
        "w_uqr": w_uqr.reshape(Q_LORA, MLA_HEADS * QK_ROPE),
        "w_ukv": w_ukv.astype(BF16),
        "w_out": w_out.astype(BF16),
    }


def _rope_tables(pos):
    half = QK_ROPE // 2
    inv = ROPE_THETA ** (-jnp.arange(half, dtype=F32) / half)
    ang = pos.astype(F32)[:, None] * inv[None, :]
    c = jnp.concatenate([jnp.cos(ang)] * 2, axis=1)
    s = jnp.concatenate([-jnp.sin(ang), jnp.sin(ang)], axis=1)
    return {"c2": jnp.tile(c, (1, 2)), "s2": jnp.tile(s, (1, 2)),
            "c8": jnp.tile(c, (1, MLA_HEADS)), "s8": jnp.tile(s, (1, MLA_HEADS))}


def _layer(x, pos, past, ffn1, ffn2, wmix, slopes, lams, subln, final_norm, lam_init, *, batch):
    n, d = x.shape
    t = n // batch

    def swiglu(x, ffn, *, last):
        casting = ffn[1].dtype == F32
        out = _ffn(x, *ffn, final_norm, final_norm=last, tm=FFN_ROWS,
                   tf=FFN_COLS_CASTING if casting else FFN_COLS)
        return (out[0], (ffn[0],) + out[1:]) if casting else (out, ffn)

    x, ffn1 = swiglu(x, ffn1, last=False)
    tab_pos = pos if t % PROJ_IN_ROWS == 0 else jnp.tile(pos, batch)
    (q, k_f, k_b, v_f, v_b, ckv, kr, krd, qn, qr, kv) = _proj_in(x, wmix, _rope_tables(tab_pos), tm=PROJ_IN_ROWS)
    if past is None:
        a_da = _da_prompt(q, k_b, v_b, slopes, lams, subln, t=t, tq=QUERY_TILE, lam_init=lam_init)
        a_mla = _mla_prompt(qn, qr, kv, krd, t=t, tq=QUERY_TILE)
    else:
        cache_k, cache_v, cache_ckv, cache_kr = past
        p = cache_k.size // (batch * DA_WIDTH)
        a_da = _da_sample(q, cache_k.reshape(batch, p * DA_HEADS, DA_HEAD_DIM),
                          cache_v.reshape(batch, p * DA_HEADS, DA_HEAD_DIM),
                          k_b, v_b, slopes, lams, subln, lam_init=lam_init, pc=CACHE_CHUNK)
        cache_kr = cache_kr.reshape(batch, p, QK_ROPE)
        krd_cache = jnp.concatenate([cache_kr, cache_kr], axis=-1).astype(BF16)
        a_mla = _mla_sample(qn, qr, cache_ckv.reshape(batch, p, KV_LORA), krd_cache, ckv, krd, wmix["w_ukv"])
    x = _proj_out(x, a_da, a_mla, wmix["w_out"], tm=PROJ_OUT_ROWS)
    y, ffn2 = swiglu(x, ffn2, last=True)
    return y, (k_f, v_f, ckv, kr), ffn1, ffn2


def kernel(x_prompt, x_sample, cache_da_k, cache_da_v, cache_mla_ckv, cache_mla_krope, ffn1_norm, ffn1_w_gate, ffn1_w_up, ffn1_w_down, mix_norm, w_in, da_lambda_q1, da_lambda_k1, da_lambda_q2, da_lambda_k2, da_subln, mla_q_norm, mla_w_uq, mla_kv_norm, mla_w_ukv, w_out, ffn2_norm, ffn2_w_gate, ffn2_w_up, ffn2_w_down, final_norm):
    depth = w_in.shape[0]
    assert depth == 1, "single-layer trunk"
    b, t, d = x_prompt.shape
    bd, td, _ = x_sample.shape
    past_len = cache_da_k.shape[2]
    l = 0
    lam_init = 0.8 - 0.6 * math.exp(-0.3 * l)

    def one_layer(w):
        return w.reshape(w.shape[1:])

    ffn1 = (ffn1_norm[l][None, :], one_layer(ffn1_w_gate), one_layer(ffn1_w_up), one_layer(ffn1_w_down))
    ffn2 = (ffn2_norm[l][None, :], one_layer(ffn2_w_gate), one_layer(ffn2_w_up), one_layer(ffn2_w_down))
    wmix = _prep_weights(w_in[l], mix_norm[l], mla_q_norm[l], mla_w_uq[l], mla_kv_norm[l], mla_w_ukv[l],
                         w_out[l])
    slopes = jnp.asarray(np.broadcast_to(ALIBI_SLOPES[:, None, None], (DA_HEADS, 1, LANES)))
    lams = (da_lambda_q1[l][None, :], da_lambda_k1[l][None, :], da_lambda_q2[l][None, :],
            da_lambda_k2[l][None, :])
    subln = da_subln[l][None, :]
    fnorm = final_norm[None, :]

    past = (cache_da_k, cache_da_v, cache_mla_ckv, cache_mla_krope)
    assert bd * td <= 1024, "the running streams must fit one SwiGLU token tile"
    y_s, st_s, ffn1, ffn2 = _layer(x_sample.reshape(bd * td, d), past_len + jnp.arange(td, dtype=jnp.int32),
                                   past, ffn1, ffn2, wmix, slopes, lams, subln, fnorm, lam_init, batch=bd)
    y_p, st_p, _, _ = _layer(x_prompt.reshape(b * t, d), jnp.arange(t, dtype=jnp.int32), None,
                             ffn1, ffn2, wmix, slopes, lams, subln, fnorm, lam_init, batch=b)

    def state(st, nb, nt):
        k_f, v_f, ckv, kr = st
        return (k_f.reshape(1, nb, nt, DA_HEADS, DA_HEAD_DIM), v_f.reshape(1, nb, nt, DA_HEADS, DA_HEAD_DIM),
                ckv.reshape(1, nb, nt, KV_LORA), kr.reshape(1, nb, nt, QK_ROPE))

    return (y_p.reshape(b, t, d), y_s.reshape(bd, td, d)) + state(st_p, b, t) + state(st_s, bd, td)
```

```python
import functools
import math

import jax
import jax.numpy as jnp
import numpy as np
from jax import lax
from jax.experimental import pallas as pl
from jax.experimental.pallas import tpu as pltpu

F32 = jnp.float32
BF16 = jnp.bfloat16

CHUNK = 64
CHUNK_SHIFT = 6
assert 1 << CHUNK_SHIFT == CHUNK
EPS = 1e-6
NEG_INF = -1e30
DA_HEADS = 8
DA_HALF = 64
DA_HEAD_DIM = 2 * DA_HALF
DA_WIDTH = DA_HEADS * DA_HEAD_DIM
MLA_HEADS = 8
Q_LORA = 512
KV_LORA = 256
QK_NOPE = 128
QK_ROPE = 64
QK_HEAD = QK_NOPE + QK_ROPE
V_HEAD = 128
MLA_WIDTH = MLA_HEADS * V_HEAD
ROPE_THETA = 10000.0
LOG2E = math.log2(math.e)
ALIBI_SLOPES = np.array([2.0 ** (-8.0 * (i + 1) / DA_HEADS) for i in range(DA_HEADS)], np.float32)
LANES = 128
assert DA_HEAD_DIM == LANES and QK_NOPE == LANES and V_HEAD == LANES and 2 * QK_ROPE == LANES

MIB = 1024 * 1024
VMEM_LIMIT_BYTES = 56 * MIB
FFN_VMEM_LIMIT_BYTES = 62 * MIB
FFN_ROWS = 1024
FFN_COLS = 512
FFN_COLS_CASTING = 256
PROJ_IN_ROWS = 256
PROJ_OUT_ROWS = 512
QUERY_TILE = 256
CACHE_CHUNK = 2048


def _params(*semantics, vmem_limit_bytes=VMEM_LIMIT_BYTES):
    return pltpu.CompilerParams(dimension_semantics=semantics, vmem_limit_bytes=vmem_limit_bytes)


def _rms(x, g):
    return x * lax.rsqrt(jnp.mean(x * x, axis=-1, keepdims=True) + EPS) * g


def _dot(a, b):
    return jnp.dot(a, b, preferred_element_type=F32)


def _dot_nt(a, b):
    return lax.dot_general(a, b, (((1,), (1,)), ((), ())), preferred_element_type=F32)


def _resident(shape):
    return pl.BlockSpec(shape, lambda *_: (0,) * len(shape), pipeline_mode=pl.Buffered(1))


def _ffn_kernel(x_ref, g_ref, wg_ref, wu_ref, wd_ref, fn_ref, o_ref, *rest, final_norm, nj, emit_bf16):
    j = pl.program_id(1)
    xn_ref = rest[-1]

    @pl.when(j == 0)
    def _():
        xn_ref[...] = _rms(x_ref[...], g_ref[...]).astype(BF16)

    def weights():
        if not emit_bf16:
            return wg_ref[...], wu_ref[...], wd_ref[...]
        blocks = []
        for w_ref, wb_ref in zip((wg_ref, wu_ref, wd_ref), rest[:3]):
            wb = w_ref[...].astype(BF16)
            wb_ref[...] = wb
            blocks.append(wb)
        return blocks

    def half_down():
        xn = xn_ref[...]
        wg, wu, wd = weights()
        gate = _dot(xn, wg)
        up = _dot(xn, wu)
        h = (gate * jax.nn.sigmoid(gate) * (0.5 * up)).astype(BF16)
        return _dot(h, wd)

    def finish(y):
        o_ref[...] = _rms(y, fn_ref[...]) if final_norm else y

    if nj == 1:
        finish(x_ref[...] + half_down())
        return

    @pl.when(j == 0)
    def _():
        o_ref[...] = x_ref[...] + half_down()

    @pl.when(jnp.logical_and(j > 0, j < nj - 1))
    def _():
        o_ref[...] += half_down()

    @pl.when(j == nj - 1)
    def _():
        finish(o_ref[...] + half_down())


def _ffn(x, norm, wg, wu, wd, fnorm, *, final_norm, tm, tf):
    n, d = x.shape
    f = wg.shape[1]
    tm = min(tm, n)
    tf = min(tf, f)
    assert n % tm == 0 and f % tf == 0
    emit_bf16 = wg.dtype == F32
    assert not emit_bf16 or n == tm
    col_block = pl.BlockSpec((d, tf), lambda i, j: (0, j))
    row_block = pl.BlockSpec((tf, d), lambda i, j: (j, 0))
    y_spec = pl.BlockSpec((tm, d), lambda i, j: (i, 0))
    y_shape = jax.ShapeDtypeStruct((n, d), F32)
    out = pl.pallas_call(
        functools.partial(_ffn_kernel, final_norm=final_norm, nj=f // tf, emit_bf16=emit_bf16),
        grid=(n // tm, f // tf),
        in_specs=[
            pl.BlockSpec((tm, d), lambda i, j: (i, 0)),
            pl.BlockSpec((1, d), lambda i, j: (0, 0)),
            col_block, col_block, row_block,
            pl.BlockSpec((1, d), lambda i, j: (0, 0)),
        ],
        out_specs=[y_spec, col_block, col_block, row_block] if emit_bf16 else y_spec,
        out_shape=([y_shape] + [jax.ShapeDtypeStruct(w.shape, BF16) for w in (wg, wu, wd)]
                   if emit_bf16 else y_shape),
        scratch_shapes=[pltpu.VMEM((tm, d), BF16)],
        compiler_params=_params("parallel", "arbitrary", vmem_limit_bytes=FFN_VMEM_LIMIT_BYTES),
        name="ffn",
    )(x, norm, wg, wu, wd, fnorm)
    return tuple(out) if emit_bf16 else out


def _store_by_head(ref, z):
    tokens = z.shape[0]
    for h in range(DA_HEADS):
        ref[pl.ds(h, tokens, stride=DA_HEADS), :] = z[:, h * DA_HEAD_DIM:(h + 1) * DA_HEAD_DIM]


def _swap_rope_halves(x):
    n = x.shape[1]
    half = QK_ROPE // 2
    lane = lax.broadcasted_iota(jnp.int32, x.shape, 1)
    in_first_half = (lane & (QK_ROPE - 1)) < half
    return jnp.where(in_first_half, pltpu.roll(x, n - half, axis=1), pltpu.roll(x, half, axis=1))


def _proj_in_kernel(x_ref, g_ref, win_ref, wkr_ref,
                    qn_ref, kvn_ref, wuqn_ref, wuqr_ref, wukv_ref,
                    c2_ref, s2_ref, c8_ref, s8_ref,
                    q_ref, kf_ref, kb_ref, vf_ref, vb_ref, ckv_ref, kr_ref, krd_ref,
                    qnope_ref, qrope_ref, kv_ref):
    h = _rms(x_ref[...], g_ref[...]).astype(BF16)
    o1, o2, o3 = DA_WIDTH, 2 * DA_WIDTH, 3 * DA_WIDTH
    o4, o5 = o3 + Q_LORA, o3 + Q_LORA + KV_LORA
    q_ref[...] = (_dot(h, win_ref[:, :o1]) * (DA_HALF ** -0.5 * LOG2E)).astype(BF16)
    k = _dot(h, win_ref[:, o1:o2])
    _store_by_head(kf_ref, k)
    kb_ref[...] = k.astype(BF16)
    v = _dot(h, win_ref[:, o2:o3])
    _store_by_head(vf_ref, v)
    vb_ref[...] = v.astype(BF16)
    cq = _rms(_dot(h, win_ref[:, o3:o4]), qn_ref[...]).astype(BF16)
    ckv = _rms(_dot(h, win_ref[:, o4:o5]), kvn_ref[...])
    ckv_ref[...] = ckv
    kr = _dot(h, wkr_ref[...])
    krd = kr * c2_ref[...] + _swap_rope_halves(kr) * s2_ref[...]
    kr_ref[...] = krd[:, :QK_ROPE]
    krd_ref[...] = krd.astype(BF16)
    mla_scale = QK_HEAD ** -0.5 * LOG2E
    qnope_ref[...] = (_dot(cq, wuqn_ref[...]) * mla_scale).astype(BF16)
    qr = _dot(cq, wuqr_ref[...])
    qr = qr * c8_ref[...] + _swap_rope_halves(qr) * s8_ref[...]
    qrope_ref[...] = (qr * mla_scale).astype(BF16)
    kv_ref[...] = _dot(ckv.astype(BF16), wukv_ref[...]).astype(BF16)


def _proj_in(x, w, tabs, *, tm):
    n, d = x.shape
    tm = min(tm, n)
    assert n % tm == 0
    t_rows = tabs["c2"].shape[0]
    assert t_rows % tm == 0 or tm % t_rows == 0
    if tm > t_rows:
        tm = t_rows
    nt = t_rows // tm

    def row(i):
        return (i, 0)

    def trow(i):
        return (i % nt, 0)

    weights = [w["w_in"], w["w_kr"], w["q_norm"], w["kv_norm"], w["w_uqn"], w["w_uqr"], w["w_ukv"]]
    tables = [tabs["c2"], tabs["s2"], tabs["c8"], tabs["s8"]]
    outs = [
        ((n, DA_WIDTH), BF16),
        ((n * DA_HEADS, DA_HEAD_DIM), F32),
        ((n, DA_WIDTH), BF16),
        ((n * DA_HEADS, DA_HEAD_DIM), F32),
        ((n, DA_WIDTH), BF16),
        ((n, KV_LORA), F32),
        ((n, QK_ROPE), F32),
        ((n, LANES), BF16),
        ((n, MLA_HEADS * QK_NOPE), BF16),
        ((n, MLA_HEADS * QK_ROPE), BF16),
        ((n, MLA_HEADS * (QK_NOPE + V_HEAD)), BF16),
    ]
    return pl.pallas_call(
        _proj_in_kernel,
        grid=(n // tm,),
        in_specs=([pl.BlockSpec((tm, d), row), _resident((1, d))]
                  + [_resident(a.shape) for a in weights]
                  + [pl.BlockSpec((tm, a.shape[1]), trow) for a in tables]),
        out_specs=[pl.BlockSpec((tm * s[0] // n, s[1]), row) for s, _ in outs],
        out_shape=[jax.ShapeDtypeStruct(s, dt) for s, dt in outs],
        compiler_params=_params("parallel"),
        name="proj_in",
    )(x, w["mix_norm"], *weights, *tables)


def _lambda(lq1_ref, lk1_ref, lq2_ref, lk2_ref, lam_init):
    a = jnp.sum(lq1_ref[...] * lk1_ref[...], axis=-1, keepdims=True)
    b = jnp.sum(lq2_ref[...] * lk2_ref[...], axis=-1, keepdims=True)
    return jnp.exp(a) - jnp.exp(b) + lam_init


def _pos_bias(qpos, kpos, slope):
    visible = lax.shift_right_arithmetic(kpos, CHUNK_SHIFT) <= lax.shift_right_arithmetic(qpos, CHUNK_SHIFT)
    if slope is None:
        return jnp.where(visible, 0.0, NEG_INF)
    dist = jnp.abs(qpos - kpos).astype(F32)
    return jnp.where(visible, (-LOG2E * slope) * dist, NEG_INF)


def _strip(tq, t, slope):
    qpos = lax.broadcasted_iota(jnp.int32, (tq, t), 0)
    kpos = lax.broadcasted_iota(jnp.int32, (tq, t), 1) - (t - tq)
    return _pos_bias(qpos, kpos, slope)


def _split_halves(q):
    lane = lax.broadcasted_iota(jnp.int32, q.shape, 1)
    zero = jnp.zeros_like(q)
    return jnp.where(lane < DA_HALF, q, zero), jnp.where(lane >= DA_HALF, q, zero)


def _da_finish(o, subln, lam_init):
    return (_rms(o, subln) * (1.0 - lam_init)).astype(BF16)


BIAS_TERMS = 3


def _da_prompt_kernel(q_ref, k_ref, v_ref, tab_ref, slope_ref, lq1_ref, lk1_ref, lq2_ref, lk2_ref, subln_ref,
                      o_ref, diag_ref, k1_ref, k2_ref, vt_ref, *, tq, lam_init):
    t = q_ref.shape[0]
    lam = _lambda(lq1_ref, lk1_ref, lq2_ref, lk2_ref, lam_init)
    c = LOG2E * slope_ref[0][:, :1]

    lane = lax.broadcasted_iota(jnp.int32, (t, LANES), 1)
    k1_ref[...] = jnp.where(lane < DA_HALF, k_ref[...], tab_ref[0])
    k2_ref[...] = jnp.where(lane >= DA_HALF, k_ref[...], tab_ref[1])
    vt_ref[...] = v_ref[...].T

    kpos = lax.broadcasted_iota(jnp.int32, (tq, tq), 0)
    qpos = lax.broadcasted_iota(jnp.int32, (tq, tq), 1)
    visible = lax.shift_right_arithmetic(kpos, CHUNK_SHIFT) <= lax.shift_right_arithmetic(qpos, CHUNK_SHIFT)
    ahead = jnp.maximum(kpos - qpos, 0).astype(F32)
    diag_ref[...] = jnp.where(visible, (-2.0 * c) * ahead, NEG_INF)

    qlane = lax.broadcasted_iota(jnp.int32, (tq, LANES), 1)
    ones_hi = ((qlane >= DA_HALF) & (qlane < DA_HALF + BIAS_TERMS)).astype(F32).astype(BF16)
    ones_lo = (qlane < BIAS_TERMS).astype(F32).astype(BF16)

    def scores(qi):
        lo, ext = qi * tq, (qi + 1) * tq
        q = q_ref[lo:ext, :]
        q1 = jnp.where(qlane < DA_HALF, q, ones_hi)
        q2 = jnp.where(qlane >= DA_HALF, q, ones_lo)
        diag = (_dot_nt(k1_ref[lo:ext, :], q1) + diag_ref[...], _dot_nt(k2_ref[lo:ext, :], q2) + diag_ref[...])
        past = (_dot_nt(k1_ref[0:lo, :], q1), _dot_nt(k2_ref[0:lo, :], q2)) if qi > 0 else None
        return diag, past

    def softmax(sd, sp):
        m = jnp.max(sd, axis=0, keepdims=True)
        if sp is None:
            pd = jnp.exp2(sd - m)
            return pd, None, jnp.sum(pd, axis=0, keepdims=True)
        m = jnp.maximum(m, jnp.max(sp, axis=0, keepdims=True))
        pd = jnp.exp2(sd - m)
        pp = jnp.exp2(sp - m)
        return pd, pp, jnp.sum(pd, axis=0, keepdims=True) + jnp.sum(pp, axis=0, keepdims=True)

    nq = t // tq
    s_next = scores(0)
    for qi in range(nq):
        lo, ext = qi * tq, (qi + 1) * tq
        diag, past = s_next
        if qi + 1 < nq:
            s_next = scores(qi + 1)
        pd1, pp1, l1 = softmax(diag[0], None if past is None else past[0])
        pd2, pp2, l2 = softmax(diag[1], None if past is None else past[1])
        r = lam * l1 / l2
        ot = _dot(vt_ref[:, lo:ext], (pd1 - pd2 * r).astype(BF16))
        if past is not None:
            ot = ot + _dot(vt_ref[:, 0:lo], (pp1 - pp2 * r).astype(BF16))
        ot = ot * (1.0 / l1)
        ot = ot * lax.rsqrt(jnp.mean(ot * ot, axis=0, keepdims=True) + EPS) * (subln_ref[...] * (1.0 - lam_init))
        o_ref[lo:ext, :] = ot.T.astype(BF16)


def _alibi_key_tables(t):
    x = (np.float32(LOG2E) * ALIBI_SLOPES)[:, None] * np.arange(t, dtype=np.float32)[None, :]
    pieces = []
    for _ in range(BIAS_TERMS):
        piece = x.astype(BF16)
        pieces.append(piece)
        x = x - piece.astype(np.float32)
    terms = jnp.asarray(np.stack(pieces, axis=-1))
    pad = lambda before: jnp.pad(terms, ((0, 0), (0, 0), (before, LANES - BIAS_TERMS - before)))
    return jnp.stack([pad(DA_HALF), pad(0)], axis=1)


def _da_prompt(q, k, v, slopes, lams, subln, *, t, tq, lam_init):
    n = q.shape[0]
    subln = subln.reshape(DA_HEAD_DIM, 1)
    tq = min(tq, t)
    blk = pl.BlockSpec((t, LANES), lambda b, h: (b, h))
    vec = lambda a: pl.BlockSpec(a.shape, lambda b, h: (0, 0))
    return pl.pallas_call(
        functools.partial(_da_prompt_kernel, tq=tq, lam_init=lam_init),
        grid=(n // t, DA_HEADS),
        in_specs=[blk, blk, blk, pl.BlockSpec((None, 2, t, LANES), lambda b, h: (h, 0, 0, 0)),
                  pl.BlockSpec((1, 1, LANES), lambda b, h: (h, 0, 0))]
                 + [vec(a) for a in lams] + [vec(subln)],
        out_specs=blk,
        out_shape=jax.ShapeDtypeStruct((n, DA_WIDTH), BF16),
        scratch_shapes=[pltpu.VMEM((tq, tq), F32), pltpu.VMEM((t, LANES), BF16), pltpu.VMEM((t, LANES), BF16),
                        pltpu.VMEM((LANES, t), BF16)],
        compiler_params=_params("parallel", "parallel"),
        name="da_prompt",
    )(q, k, v, _alibi_key_tables(t), slopes, *lams, subln)


def _select_rope(qr, h):
    lane = lax.broadcasted_iota(jnp.int32, qr.shape, 1)
    mine = lax.shift_right_logical(lane, CHUNK_SHIFT) == (h % 2)
    return jnp.where(mine, qr, jnp.zeros_like(qr))


MLA_PAIR = 2


def _mla_prompt_kernel(qn_ref, qr_ref, kv_ref, krd_ref, o_ref, mask_ref, kfull_ref, *, tq):
    t = qn_ref.shape[0]
    kv_cols = QK_NOPE + V_HEAD
    mask_ref[...] = _strip(tq, tq, None)
    for hh in range(MLA_PAIR):
        kfull_ref[hh, :, :LANES] = kv_ref[:, hh * kv_cols:hh * kv_cols + QK_NOPE]
        kfull_ref[hh, :, LANES:] = krd_ref[...]

    def scores(hh, qi):
        lo, ext = qi * tq, (qi + 1) * tq
        rope_block = qr_ref[lo:ext, (hh // 2) * LANES:(hh // 2 + 1) * LANES]
        q = jnp.concatenate([qn_ref[lo:ext, hh * LANES:(hh + 1) * LANES], _select_rope(rope_block, hh)], axis=1)
        sd = _dot_nt(q, kfull_ref[hh, lo:ext, :]) + mask_ref[...]
        sp = _dot_nt(q, kfull_ref[hh, 0:lo, :]) if qi > 0 else None
        return sd, sp

    def probabilities(sd, sp):
        m = jnp.max(sd, axis=-1, keepdims=True)
        if sp is None:
            pd = jnp.exp2(sd - m)
            return pd.astype(BF16), None, jnp.sum(pd, axis=-1, keepdims=True)
        m = jnp.maximum(m, jnp.max(sp, axis=-1, keepdims=True))
        pd = jnp.exp2(sd - m)
        pp = jnp.exp2(sp - m)
        l = jnp.sum(pd, axis=-1, keepdims=True) + jnp.sum(pp, axis=-1, keepdims=True)
        return pd.astype(BF16), pp.astype(BF16), l

    def weighted_values(hh, qi, pd, pp, l):
        lo, ext = qi * tq, (qi + 1) * tq
        v_cols = slice(hh * kv_cols + QK_NOPE, (hh + 1) * kv_cols)
        o = _dot(pd, kv_ref[lo:ext, v_cols])
        if pp is not None:
            o = o + _dot(pp, kv_ref[0:lo, v_cols])
        o_ref[lo:ext, hh * LANES:(hh + 1) * LANES] = (o * (1.0 / l)).astype(BF16)

    tiles = [(hh, qi) for hh in range(MLA_PAIR) for qi in range(t // tq)]
    s_next = scores(*tiles[0])
    pending = None
    for i, tile in enumerate(tiles):
        sd, sp = s_next
        if i + 1 < len(tiles):
            s_next = scores(*tiles[i + 1])
        if pending is not None:
            weighted_values(*tiles[i - 1], *pending)
        pending = probabilities(sd, sp)
    weighted_values(*tiles[-1], *pending)


def _mla_prompt(qn, qr, kv, krd, *, t, tq):
    n = qn.shape[0]
    tq = min(tq, t)
    pair = lambda width: pl.BlockSpec((t, MLA_PAIR * width), lambda b, g: (b, g))
    return pl.pallas_call(
        functools.partial(_mla_prompt_kernel, tq=tq),
        grid=(n // t, MLA_HEADS // MLA_PAIR),
        in_specs=[pair(QK_NOPE), pair(QK_ROPE), pair(QK_NOPE + V_HEAD),
                  pl.BlockSpec((t, LANES), lambda b, g: (b, 0))],
        out_specs=pair(V_HEAD),
        out_shape=jax.ShapeDtypeStruct((n, MLA_WIDTH), BF16),
        scratch_shapes=[pltpu.VMEM((tq, tq), F32), pltpu.VMEM((MLA_PAIR, t, 2 * LANES), BF16)],
        compiler_params=_params("parallel", "parallel"),
        name="mla_prompt",
    )(qn, qr, kv, krd)


def _positions(tq, tk, q0, k0):
    qpos = lax.broadcasted_iota(jnp.int32, (tq, tk), 0) + q0
    kpos = lax.broadcasted_iota(jnp.int32, (tq, tk), 1) + k0
    return qpos, kpos


def _da_sample_kernel(q_ref, ck_ref, cv_ref, nk_ref, nv_ref, slope_ref, lq1_ref, lk1_ref, lq2_ref, lk2_ref,
                      subln_ref, o_ref, acc_ref, m_ref, l_ref, *, lam_init):
    c = pl.program_id(1)
    nc = pl.num_programs(1)
    tq = q_ref.shape[0]
    pc = ck_ref.shape[0] // DA_HEADS
    past = pc * nc

    @pl.when(c == 0)
    def _():
        m_ref[...] = jnp.full_like(m_ref, NEG_INF)
        l_ref[...] = jnp.zeros_like(l_ref)
        acc_ref[...] = jnp.zeros_like(acc_ref)

    def attend(keys, values, unit_bias):
        unit2 = jnp.concatenate([unit_bias, unit_bias], axis=0)
        scores = []
        for h in range(DA_HEADS):
            q2 = jnp.concatenate(_split_halves(q_ref[:, h * LANES:(h + 1) * LANES]), axis=0)
            scores.append(_dot_nt(q2, keys(h)) + unit2 * slope_ref[h][:, :1])
        probs = []
        for h, s in enumerate(scores):
            rows = slice(2 * h * tq, (2 * h + 2) * tq)
            m_old = m_ref[rows, :]
            m_new = jnp.maximum(m_old, jnp.max(s, axis=-1, keepdims=True))
            alpha = jnp.exp2(m_old - m_new)
            p = jnp.exp2(s - m_new)
            m_ref[rows, :] = m_new
            l_ref[rows, :] = alpha * l_ref[rows, :] + jnp.sum(p, axis=-1, keepdims=True)
            probs.append((alpha, p.astype(BF16)))
        for h, (alpha, p) in enumerate(probs):
            rows = slice(2 * h * tq, (2 * h + 2) * tq)
            acc_ref[rows, :] = alpha * acc_ref[rows, :] + _dot(p, values(h))

    def head_rows(h):
        return pl.ds(h, pc, stride=DA_HEADS)

    attend(lambda h: ck_ref[head_rows(h), :].astype(BF16), lambda h: cv_ref[head_rows(h), :].astype(BF16),
           _pos_bias(*_positions(tq, pc, past, c * pc), 1.0))

    @pl.when(c == nc - 1)
    def _():
        lam = _lambda(lq1_ref, lk1_ref, lq2_ref, lk2_ref, lam_init)
        attend(lambda h: nk_ref[:, h * LANES:(h + 1) * LANES], lambda h: nv_ref[:, h * LANES:(h + 1) * LANES],
               _pos_bias(*_positions(tq, tq, past, past), 1.0))
        for h in range(DA_HEADS):
            r1 = slice(2 * h * tq, (2 * h + 1) * tq)
            r2 = slice((2 * h + 1) * tq, (2 * h + 2) * tq)
            o = acc_ref[r1, :] / l_ref[r1, :] - acc_ref[r2, :] * (lam / l_ref[r2, :])
            o_ref[:, h * LANES:(h + 1) * LANES] = _da_finish(o, subln_ref[...], lam_init)


def _da_sample(q, cache_k, cache_v, nk, nv, slopes, lams, subln, *, lam_init, pc):
    bd, rows, _ = cache_k.shape
    past = rows // DA_HEADS
    pc = min(pc, past)
    assert past % pc == 0
    n = q.shape[0]
    tq = n // bd
    new = pl.BlockSpec((tq, DA_WIDTH), lambda b, c: (b, 0))
    cache = pl.BlockSpec((None, pc * DA_HEADS, LANES), lambda b, c: (b, c, 0))
    whole = lambda a: pl.BlockSpec(a.shape, lambda b, c: (0,) * a.ndim)
    return pl.pallas_call(
        functools.partial(_da_sample_kernel, lam_init=lam_init),
        grid=(bd, past // pc),
        in_specs=[new, cache, cache, new, new, whole(slopes)] + [whole(a) for a in lams] + [whole(subln)],
        out_specs=new,
        out_shape=jax.ShapeDtypeStruct((n, DA_WIDTH), BF16),
        scratch_shapes=[pltpu.VMEM((2 * DA_HEADS * tq, LANES), F32),
                        pltpu.VMEM((2 * DA_HEADS * tq, 1), F32),
                        pltpu.VMEM((2 * DA_HEADS * tq, 1), F32)],
        compiler_params=_params("parallel", "arbitrary"),
        name="da_sample",
    )(q, cache_k, cache_v, nk, nv, slopes, *lams, subln)


MLA_GROUP = 4


def _mla_sample_kernel(qn_ref, qr_ref, cc_ref, ckrd_ref, nc_ref, nkrd_ref, wukv_ref, o_ref):
    tq = qn_ref.shape[0]
    past = cc_ref.shape[0]
    cc = cc_ref[...].astype(BF16)
    nc = nc_ref[...].astype(BF16)
    rows = MLA_GROUP * tq
    masked = ((past + tq - 1) >> CHUNK_SHIFT) > (past >> CHUNK_SHIFT)
    if masked:
        qrow = lax.broadcasted_iota(jnp.int32, (rows, 1), 0) & (tq - 1)
        bias_c = _pos_bias(qrow + past, lax.broadcasted_iota(jnp.int32, (rows, past), 1), None)
        bias_n = _pos_bias(qrow + past, lax.broadcasted_iota(jnp.int32, (rows, tq), 1) + past, None)
    kv_cols = QK_NOPE + V_HEAD
    for g in range(MLA_HEADS // MLA_GROUP):
        heads = range(g * MLA_GROUP, (g + 1) * MLA_GROUP)
        qa = jnp.concatenate(
            [_dot_nt(qn_ref[:, h * QK_NOPE:(h + 1) * QK_NOPE],
                     wukv_ref[:, h * kv_cols:h * kv_cols + QK_NOPE]).astype(BF16) for h in heads], axis=0)
        qr = jnp.concatenate(
            [_select_rope(qr_ref[:, (h // 2) * LANES:(h // 2 + 1) * LANES], h) for h in heads], axis=0)
        sc = _dot_nt(qa, cc) + _dot_nt(qr, ckrd_ref[...])
        sn = _dot_nt(qa, nc) + _dot_nt(qr, nkrd_ref[...])
        if masked:
            sc, sn = sc + bias_c, sn + bias_n
        m = jnp.maximum(jnp.max(sc, axis=-1, keepdims=True), jnp.max(sn, axis=-1, keepdims=True))
        pc = jnp.exp2(sc - m)
        pn = jnp.exp2(sn - m)
        w = 1.0 / (jnp.sum(pc, axis=-1, keepdims=True) + jnp.sum(pn, axis=-1, keepdims=True))
        lat = ((_dot(pc.astype(BF16), cc) + _dot(pn.astype(BF16), nc)) * w).astype(BF16)
        for j, h in enumerate(heads):
            w_uv = wukv_ref[:, h * kv_cols + QK_NOPE:(h + 1) * kv_cols]
            o_ref[:, h * V_HEAD:(h + 1) * V_HEAD] = _dot(lat[j * tq:(j + 1) * tq, :], w_uv).astype(BF16)


def _mla_sample(qn, qr, cache_ckv, krd_cache, ckv_new, krd_new, w_ukv):
    bd, past, _ = cache_ckv.shape
    n = qn.shape[0]
    tq = n // bd
    assert tq & (tq - 1) == 0
    row = lambda b: (b, 0)
    return pl.pallas_call(
        _mla_sample_kernel,
        grid=(bd,),
        in_specs=[
            pl.BlockSpec((tq, MLA_HEADS * QK_NOPE), row),
            pl.BlockSpec((tq, MLA_HEADS * QK_ROPE), row),
            pl.BlockSpec((None, past, KV_LORA), lambda b: (b, 0, 0)),
            pl.BlockSpec((None, past, LANES), lambda b: (b, 0, 0)),
            pl.BlockSpec((tq, KV_LORA), row),
            pl.BlockSpec((tq, LANES), row),
            _resident(w_ukv.shape),
        ],
        out_specs=pl.BlockSpec((tq, MLA_WIDTH), row),
        out_shape=jax.ShapeDtypeStruct((n, MLA_WIDTH), BF16),
        compiler_params=_params("parallel"),
        name="mla_sample",
    )(qn, qr, cache_ckv, krd_cache, ckv_new, krd_new, w_ukv)


def _proj_out_kernel(x_ref, a_ref, b_ref, w_ref, o_ref):
    o_ref[...] = (x_ref[...] + _dot(a_ref[...], w_ref[:DA_WIDTH, :]) + _dot(b_ref[...], w_ref[DA_WIDTH:, :]))


def _proj_out(x, a_da, a_mla, w_out, *, tm):
    n, d = x.shape
    tm = min(tm, n)
    assert n % tm == 0
    row = lambda i: (i, 0)
    return pl.pallas_call(
        _proj_out_kernel,
        grid=(n // tm,),
        in_specs=[pl.BlockSpec((tm, d), row), pl.BlockSpec((tm, DA_WIDTH), row),
                  pl.BlockSpec((tm, MLA_WIDTH), row), _resident(w_out.shape)],
        out_specs=pl.BlockSpec((tm, d), row),
        out_shape=jax.ShapeDtypeStruct((n, d), F32),
        compiler_params=_params("parallel"),
        name="proj_out",
    )(x, a_da, a_mla, w_out)


def _prep_weights(w_in, mix_norm, q_norm, w_uq, kv_norm, w_ukv, w_out):
    w_in = w_in.astype(BF16)
    w_kr = w_in[:, 3 * DA_WIDTH + Q_LORA + KV_LORA:]
    w_uq = w_uq.astype(BF16).reshape(Q_LORA, MLA_HEADS, QK_HEAD)
    w_uqr = w_uq[:, :, QK_NOPE:]
    return {
        "mix_norm": mix_norm[None, :],
        "w_in": w_in,
        "w_kr": jnp.concatenate([w_kr, w_kr], axis=1),
        "q_norm": q_norm[None, :], "kv_norm": kv_norm[None, :],
        "w_uqn": w_uq[:, :, :QK_NOPE].reshape(Q_LORA, MLA_HEADS * QK_NOPE),
        "w_uqr": w_uqr.reshape(Q_LORA, MLA_HEADS * QK_ROPE),
        "w_ukv": w_ukv.astype(BF16),
        "w_out": w_out.astype(BF16),
    }


def _rope_tables(pos):
    half = QK_ROPE // 2
    inv = ROPE_THETA ** (-jnp.arange(half, dtype=F32) / half)
    ang = pos.astype(F32)[:, None] * inv[None, :]
    c = jnp.concatenate([jnp.cos(ang)] * 2, axis=1)
    s = jnp.concatenate([-jnp.sin(ang), jnp.sin(ang)], axis=1)
    return {"c2": jnp.tile(c, (1, 2)), "s2": jnp.tile(s, (1, 2)),
            "c8": jnp.tile(c, (1, MLA_HEADS)), "s8": jnp.tile(s, (1, MLA_HEADS))}


def _layer(x, pos, past, ffn1, ffn2, wmix, slopes, lams, subln, final_norm, lam_init, *, batch):
    n, d = x.shape
    t = n // batch

    def swiglu(x, ffn, *, last):
        casting = ffn[1].dtype == F32
        out = _ffn(x, *ffn, final_norm, final_norm=last, tm=FFN_ROWS,
                   tf=FFN_COLS_CASTING if casting else FFN_COLS)
        return (out[0], (ffn[0],) + out[1:]) if casting else (out, ffn)

    x, ffn1 = swiglu(x, ffn1, last=False)
    tab_pos = pos if t % PROJ_IN_ROWS == 0 else jnp.tile(pos, batch)
    (q, k_f, k_b, v_f, v_b, ckv, kr, krd, qn, qr, kv) = _proj_in(x, wmix, _rope_tables(tab_pos), tm=PROJ_IN_ROWS)
    if past is None:
        a_da = _da_prompt(q, k_b, v_b, slopes, lams, subln, t=t, tq=QUERY_TILE, lam_init=lam_init)
        a_mla = _mla_prompt(qn, qr, kv, krd, t=t, tq=QUERY_TILE)
    else:
        cache_k, cache_v, cache_ckv, cache_kr = past
        p = cache_k.size // (batch * DA_WIDTH)
        a_da = _da_sample(q, cache_k.reshape(batch, p * DA_HEADS, DA_HEAD_DIM),
                          cache_v.reshape(batch, p * DA_HEADS, DA_HEAD_DIM),
                          k_b, v_b, slopes, lams, subln, lam_init=lam_init, pc=CACHE_CHUNK)
        cache_kr = cache_kr.reshape(batch, p, QK_ROPE)
        krd_cache = jnp.concatenate([cache_kr, cache_kr], axis=-1).astype(BF16)
        a_mla = _mla_sample(qn, qr, cache_ckv.reshape(batch, p, KV_LORA), krd_cache, ckv, krd, wmix["w_ukv"])
    x = _proj_out(x, a_da, a_mla, wmix["w_out"], tm=PROJ_OUT_ROWS)
    y, ffn2 = swiglu(x, ffn2, last=True)
    return y, (k_f, v_f, ckv, kr), ffn1, ffn2


def kernel(x_prompt, x_sample, cache_da_k, cache_da_v, cache_mla_ckv, cache_mla_krope, ffn1_norm, ffn1_w_gate, ffn1_w_up, ffn1_w_down, mix_norm, w_in, da_lambda_q1, da_lambda_k1, da_lambda_q2, da_lambda_k2, da_subln, mla_q_norm, mla_w_uq, mla_kv_norm, mla_w_ukv, w_out, ffn2_norm, ffn2_w_gate, ffn2_w_up, ffn2_w_down, final_norm):
    depth = w_in.shape[0]
    assert depth == 1, "single-layer trunk"
    b, t, d = x_prompt.shape
    bd, td, _ = x_sample.shape
    past_len = cache_da_k.shape[2]
    l = 0
    lam_init = 0.8 - 0.6 * math.exp(-0.3 * l)

    def one_layer(w):
        return w.reshape(w.shape[1:])

    ffn1 = (ffn1_norm[l][None, :], one_layer(ffn1_w_gate), one_layer(ffn1_w_up), one_layer(ffn1_w_down))
    ffn2 = (ffn2_norm[l][None, :], one_layer(ffn2_w_gate), one_layer(ffn2_w_up), one_layer(ffn2_w_down))
    wmix = _prep_weights(w_in[l], mix_norm[l], mla_q_norm[l], mla_w_uq[l], mla_kv_norm[l], mla_w_ukv[l],
                         w_out[l])
    slopes = jnp.asarray(np.broadcast_to(ALIBI_SLOPES[:, None, None], (DA_HEADS, 1, LANES)))
    lams = (da_lambda_q1[l][None, :], da_lambda_k1[l][None, :], da_lambda_q2[l][None, :],
            da_lambda_k2[l][None, :])
    subln = da_subln[l][None, :]
    fnorm = final_norm[None, :]

    past = (cache_da_k, cache_da_v, cache_mla_ckv, cache_mla_krope)
    assert bd * td <= 1024, "the running streams must fit one SwiGLU token tile"
    y_s, st_s, ffn1, ffn2 = _layer(x_sample.reshape(bd * td, d), past_len + jnp.arange(td, dtype=jnp.int32),
                                   past, ffn1, ffn2, wmix, slopes, lams, subln, fnorm, lam_init, batch=bd)
    y_p, st_p, _, _ = _layer(x_prompt.reshape(b * t, d), jnp.arange(t, dtype=jnp.int32), None,
                             ffn1, ffn2, wmix, slopes, lams, subln, fnorm, lam_init, batch=b)

    def state(st, nb, nt):
        k_f, v_f, ckv, kr = st
        return (k_f.reshape(1, nb, nt, DA_HEADS, DA_HEAD_DIM), v_f.reshape(1, nb, nt, DA_HEADS, DA_HEAD_DIM),
                ckv.reshape(1, nb, nt, KV_LORA), kr.reshape(1, nb, nt, QK_ROPE))

    return (y_p.reshape(b, t, d), y_s.reshape(bd, td, d)) + state(st_p, b, t) + state(st_s, bd, td)
```

```python
import functools
import math

import jax
import jax.numpy as jnp
import numpy as np
from jax import lax
from jax.experimental import pallas as pl
from jax.experimental.pallas import tpu as pltpu

F32 = jnp.float32
BF16 = jnp.bfloat16

CHUNK = 64
CHUNK_SHIFT = 6
assert 1 << CHUNK_SHIFT == CHUNK
EPS = 1e-6
NEG_INF = -1e30
DA_HEADS = 8
DA_HALF = 64
DA_HEAD_DIM = 2 * DA_HALF
DA_WIDTH = DA_HEADS * DA_HEAD_DIM
MLA_HEADS = 8
Q_LORA = 512
KV_LORA = 256
QK_NOPE = 128
QK_ROPE = 64
QK_HEAD = QK_NOPE + QK_ROPE
V_HEAD = 128
MLA_WIDTH = MLA_HEADS * V_HEAD
ROPE_THETA = 10000.0
LOG2E = math.log2(math.e)
ALIBI_SLOPES = np.array([2.0 ** (-8.0 * (i + 1) / DA_HEADS) for i in range(DA_HEADS)], np.float32)
LANES = 128
assert DA_HEAD_DIM == LANES and QK_NOPE == LANES and V_HEAD == LANES and 2 * QK_ROPE == LANES

MIB = 1024 * 1024
VMEM_LIMIT_BYTES = 56 * MIB
FFN_VMEM_LIMIT_BYTES = 62 * MIB
FFN_ROWS = 1024
FFN_COLS = 512
FFN_COLS_CASTING = 256
PROJ_IN_ROWS = 512
PROJ_OUT_ROWS = 512
QUERY_TILE = 256
CACHE_CHUNK = 2048


def _params(*semantics, vmem_limit_bytes=VMEM_LIMIT_BYTES):
    return pltpu.CompilerParams(dimension_semantics=semantics, vmem_limit_bytes=vmem_limit_bytes)


def _rms(x, g):
    return x * lax.rsqrt(jnp.mean(x * x, axis=-1, keepdims=True) + EPS) * g


def _dot(a, b):
    return jnp.dot(a, b, preferred_element_type=F32)


def _dot_nt(a, b):
    return lax.dot_general(a, b, (((1,), (1,)), ((), ())), preferred_element_type=F32)


def _resident(shape):
    return pl.BlockSpec(shape, lambda *_: (0,) * len(shape), pipeline_mode=pl.Buffered(1))


def _ffn_kernel(x_ref, g_ref, wg_ref, wu_ref, wd_ref, fn_ref, o_ref, *rest, final_norm, nj, emit_bf16):
    j = pl.program_id(1)
    xn_ref = rest[-1]

    @pl.when(j == 0)
    def _():
        xn_ref[...] = _rms(x_ref[...], g_ref[...]).astype(BF16)

    def weights():
        if not emit_bf16:
            return wg_ref[...], wu_ref[...], wd_ref[...]
        blocks = []
        for w_ref, wb_ref in zip((wg_ref, wu_ref, wd_ref), rest[:3]):
            wb = w_ref[...].astype(BF16)
            wb_ref[...] = wb
            blocks.append(wb)
        return blocks

    def half_down():
        xn = xn_ref[...]
        wg, wu, wd = weights()
        gate = _dot(xn, wg)
        up = _dot(xn, wu)
        h = (gate * jax.nn.sigmoid(gate) * (0.5 * up)).astype(BF16)
        return _dot(h, wd)

    def finish(y):
        o_ref[...] = _rms(y, fn_ref[...]) if final_norm else y

    if nj == 1:
        finish(x_ref[...] + half_down())
        return

    @pl.when(j == 0)
    def _():
        o_ref[...] = x_ref[...] + half_down()

    @pl.when(jnp.logical_and(j > 0, j < nj - 1))
    def _():
        o_ref[...] += half_down()

    @pl.when(j == nj - 1)
    def _():
        finish(o_ref[...] + half_down())


def _ffn(x, norm, wg, wu, wd, fnorm, *, final_norm, tm, tf):
    n, d = x.shape
    f = wg.shape[1]
    tm = min(tm, n)
    tf = min(tf, f)
    assert n % tm == 0 and f % tf == 0
    emit_bf16 = wg.dtype == F32
    assert not emit_bf16 or n == tm
    col_block = pl.BlockSpec((d, tf), lambda i, j: (0, j))
    row_block = pl.BlockSpec((tf, d), lambda i, j: (j, 0))
    y_spec = pl.BlockSpec((tm, d), lambda i, j: (i, 0))
    y_shape = jax.ShapeDtypeStruct((n, d), F32)
    out = pl.pallas_call(
        functools.partial(_ffn_kernel, final_norm=final_norm, nj=f // tf, emit_bf16=emit_bf16),
        grid=(n // tm, f // tf),
        in_specs=[
            pl.BlockSpec((tm, d), lambda i, j: (i, 0)),
            pl.BlockSpec((1, d), lambda i, j: (0, 0)),
            col_block, col_block, row_block,
            pl.BlockSpec((1, d), lambda i, j: (0, 0)),
        ],
        out_specs=[y_spec, col_block, col_block, row_block] if emit_bf16 else y_spec,
        out_shape=([y_shape] + [jax.ShapeDtypeStruct(w.shape, BF16) for w in (wg, wu, wd)]
                   if emit_bf16 else y_shape),
        scratch_shapes=[pltpu.VMEM((tm, d), BF16)],
        compiler_params=_params("parallel", "arbitrary", vmem_limit_bytes=FFN_VMEM_LIMIT_BYTES),
        name="ffn",
    )(x, norm, wg, wu, wd, fnorm)
    return tuple(out) if emit_bf16 else out


def _store_by_head(ref, z):
    tokens = z.shape[0]
    for h in range(DA_HEADS):
        ref[pl.ds(h, tokens, stride=DA_HEADS), :] = z[:, h * DA_HEAD_DIM:(h + 1) * DA_HEAD_DIM]


def _swap_rope_halves(x):
    n = x.shape[1]
    half = QK_ROPE // 2
    lane = lax.broadcasted_iota(jnp.int32, x.shape, 1)
    in_first_half = (lane & (QK_ROPE - 1)) < half
    return jnp.where(in_first_half, pltpu.roll(x, n - half, axis=1), pltpu.roll(x, half, axis=1))


def _proj_in_kernel(x_ref, g_ref, win_ref, wkr_ref,
                    qn_ref, kvn_ref, wuqn_ref, wuqr_ref, wukv_ref,
                    c2_ref, s2_ref, c8_ref, s8_ref,
                    q_ref, kf_ref, kb_ref, vf_ref, vb_ref, ckv_ref, kr_ref, krd_ref,
                    qnope_ref, qrope_ref, kv_ref):
    h = _rms(x_ref[...], g_ref[...]).astype(BF16)
    o1, o2, o3 = DA_WIDTH, 2 * DA_WIDTH, 3 * DA_WIDTH
    o4, o5 = o3 + Q_LORA, o3 + Q_LORA + KV_LORA
    q_ref[...] = (_dot(h, win_ref[:, :o1]) * (DA_HALF ** -0.5 * LOG2E)).astype(BF16)
    k = _dot(h, win_ref[:, o1:o2])
    _store_by_head(kf_ref, k)
    kb_ref[...] = k.astype(BF16)
    v = _dot(h, win_ref[:, o2:o3])
    _store_by_head(vf_ref, v)
    vb_ref[...] = v.astype(BF16)
    cq = _rms(_dot(h, win_ref[:, o3:o4]), qn_ref[...]).astype(BF16)
    ckv = _rms(_dot(h, win_ref[:, o4:o5]), kvn_ref[...])
    ckv_ref[...] = ckv
    kr = _dot(h, wkr_ref[...])
    krd = kr * c2_ref[...] + _swap_rope_halves(kr) * s2_ref[...]
    kr_ref[...] = krd[:, :QK_ROPE]
    krd_ref[...] = krd.astype(BF16)
    mla_scale = QK_HEAD ** -0.5 * LOG2E
    qnope_ref[...] = (_dot(cq, wuqn_ref[...]) * mla_scale).astype(BF16)
    qr = _dot(cq, wuqr_ref[...])
    qr = qr * c8_ref[...] + _swap_rope_halves(qr) * s8_ref[...]
    qrope_ref[...] = (qr * mla_scale).astype(BF16)
    kv_ref[...] = _dot(ckv.astype(BF16), wukv_ref[...]).astype(BF16)


def _proj_in(x, w, tabs, *, tm):
    n, d = x.shape
    tm = min(tm, n)
    assert n % tm == 0
    t_rows = tabs["c2"].shape[0]
    assert t_rows % tm == 0 or tm % t_rows == 0
    if tm > t_rows:
        tm = t_rows
    nt = t_rows // tm

    def row(i):
        return (i, 0)

    def trow(i):
        return (i % nt, 0)

    weights = [w["w_in"], w["w_kr"], w["q_norm"], w["kv_norm"], w["w_uqn"], w["w_uqr"], w["w_ukv"]]
    tables = [tabs["c2"], tabs["s2"], tabs["c8"], tabs["s8"]]
    outs = [
        ((n, DA_WIDTH), BF16),
        ((n * DA_HEADS, DA_HEAD_DIM), F32),
        ((n, DA_WIDTH), BF16),
        ((n * DA_HEADS, DA_HEAD_DIM), F32),
        ((n, DA_WIDTH), BF16),
        ((n, KV_LORA), F32),
        ((n, QK_ROPE), F32),
        ((n, LANES), BF16),
        ((n, MLA_HEADS * QK_NOPE), BF16),
        ((n, MLA_HEADS * QK_ROPE), BF16),
        ((n, MLA_HEADS * (QK_NOPE + V_HEAD)), BF16),
    ]
    return pl.pallas_call(
        _proj_in_kernel,
        grid=(n // tm,),
        in_specs=([pl.BlockSpec((tm, d), row), _resident((1, d))]
                  + [_resident(a.shape) for a in weights]
                  + [pl.BlockSpec((tm, a.shape[1]), trow) for a in tables]),
        out_specs=[pl.BlockSpec((tm * s[0] // n, s[1]), row) for s, _ in outs],
        out_shape=[jax.ShapeDtypeStruct(s, dt) for s, dt in outs],
        compiler_params=_params("parallel", vmem_limit_bytes=FFN_VMEM_LIMIT_BYTES),
        name="proj_in",
    )(x, w["mix_norm"], *weights, *tables)


def _lambda(lq1_ref, lk1_ref, lq2_ref, lk2_ref, lam_init):
    a = jnp.sum(lq1_ref[...] * lk1_ref[...], axis=-1, keepdims=True)
    b = jnp.sum(lq2_ref[...] * lk2_ref[...], axis=-1, keepdims=True)
    return jnp.exp(a) - jnp.exp(b) + lam_init


def _pos_bias(qpos, kpos, slope):
    visible = lax.shift_right_arithmetic(kpos, CHUNK_SHIFT) <= lax.shift_right_arithmetic(qpos, CHUNK_SHIFT)
    if slope is None:
        return jnp.where(visible, 0.0, NEG_INF)
    dist = jnp.abs(qpos - kpos).astype(F32)
    return jnp.where(visible, (-LOG2E * slope) * dist, NEG_INF)


def _strip(tq, t, slope):
    qpos = lax.broadcasted_iota(jnp.int32, (tq, t), 0)
    kpos = lax.broadcasted_iota(jnp.int32, (tq, t), 1) - (t - tq)
    return _pos_bias(qpos, kpos, slope)


def _split_halves(q):
    lane = lax.broadcasted_iota(jnp.int32, q.shape, 1)
    zero = jnp.zeros_like(q)
    return jnp.where(lane < DA_HALF, q, zero), jnp.where(lane >= DA_HALF, q, zero)


def _da_finish(o, subln, lam_init):
    return (_rms(o, subln) * (1.0 - lam_init)).astype(BF16)


BIAS_TERMS = 3


def _da_prompt_kernel(q_ref, k_ref, v_ref, tab_ref, slope_ref, lq1_ref, lk1_ref, lq2_ref, lk2_ref, subln_ref,
                      o_ref, diag_ref, k1_ref, k2_ref, vt_ref, *, tq, lam_init):
    t = q_ref.shape[0]
    lam = _lambda(lq1_ref, lk1_ref, lq2_ref, lk2_ref, lam_init)
    c = LOG2E * slope_ref[0][:, :1]

    lane = lax.broadcasted_iota(jnp.int32, (t, LANES), 1)
    k1_ref[...] = jnp.where(lane < DA_HALF, k_ref[...], tab_ref[0])
    k2_ref[...] = jnp.where(lane >= DA_HALF, k_ref[...], tab_ref[1])
    vt_ref[...] = v_ref[...].T

    kpos = lax.broadcasted_iota(jnp.int32, (tq, tq), 0)
    qpos = lax.broadcasted_iota(jnp.int32, (tq, tq), 1)
    visible = lax.shift_right_arithmetic(kpos, CHUNK_SHIFT) <= lax.shift_right_arithmetic(qpos, CHUNK_SHIFT)
    ahead = jnp.maximum(kpos - qpos, 0).astype(F32)
    diag_ref[...] = jnp.where(visible, (-2.0 * c) * ahead, NEG_INF)

    qlane = lax.broadcasted_iota(jnp.int32, (tq, LANES), 1)
    ones_hi = ((qlane >= DA_HALF) & (qlane < DA_HALF + BIAS_TERMS)).astype(F32).astype(BF16)
    ones_lo = (qlane < BIAS_TERMS).astype(F32).astype(BF16)

    def scores(qi):
        lo, ext = qi * tq, (qi + 1) * tq
        q = q_ref[lo:ext, :]
        q1 = jnp.where(qlane < DA_HALF, q, ones_hi)
        q2 = jnp.where(qlane >= DA_HALF, q, ones_lo)
        diag = (_dot_nt(k1_ref[lo:ext, :], q1) + diag_ref[...], _dot_nt(k2_ref[lo:ext, :], q2) + diag_ref[...])
        past = (_dot_nt(k1_ref[0:lo, :], q1), _dot_nt(k2_ref[0:lo, :], q2)) if qi > 0 else None
        return diag, past

    def softmax(sd, sp):
        m = jnp.max(sd, axis=0, keepdims=True)
        if sp is None:
            pd = jnp.exp2(sd - m)
            return pd, None, jnp.sum(pd, axis=0, keepdims=True)
        m = jnp.maximum(m, jnp.max(sp, axis=0, keepdims=True))
        pd = jnp.exp2(sd - m)
        pp = jnp.exp2(sp - m)
        return pd, pp, jnp.sum(pd, axis=0, keepdims=True) + jnp.sum(pp, axis=0, keepdims=True)

    nq = t // tq
    s_next = scores(0)
    for qi in range(nq):
        lo, ext = qi * tq, (qi + 1) * tq
        diag, past = s_next
        if qi + 1 < nq:
            s_next = scores(qi + 1)
        pd1, pp1, l1 = softmax(diag[0], None if past is None else past[0])
        pd2, pp2, l2 = softmax(diag[1], None if past is None else past[1])
        r = lam * l1 / l2
        ot = _dot(vt_ref[:, lo:ext], (pd1 - pd2 * r).astype(BF16))
        if past is not None:
            ot = ot + _dot(vt_ref[:, 0:lo], (pp1 - pp2 * r).astype(BF16))
        ot = ot * (1.0 / l1)
        ot = ot * lax.rsqrt(jnp.mean(ot * ot, axis=0, keepdims=True) + EPS) * (subln_ref[...] * (1.0 - lam_init))
        o_ref[lo:ext, :] = ot.T.astype(BF16)


def _alibi_key_tables(t):
    x = (np.float32(LOG2E) * ALIBI_SLOPES)[:, None] * np.arange(t, dtype=np.float32)[None, :]
    pieces = []
    for _ in range(BIAS_TERMS):
        piece = x.astype(BF16)
        pieces.append(piece)
        x = x - piece.astype(np.float32)
    terms = jnp.asarray(np.stack(pieces, axis=-1))
    pad = lambda before: jnp.pad(terms, ((0, 0), (0, 0), (before, LANES - BIAS_TERMS - before)))
    return jnp.stack([pad(DA_HALF), pad(0)], axis=1)


def _da_prompt(q, k, v, slopes, lams, subln, *, t, tq, lam_init):
    n = q.shape[0]
    subln = subln.reshape(DA_HEAD_DIM, 1)
    tq = min(tq, t)
    blk = pl.BlockSpec((t, LANES), lambda b, h: (b, h))
    vec = lambda a: pl.BlockSpec(a.shape, lambda b, h: (0, 0))
    return pl.pallas_call(
        functools.partial(_da_prompt_kernel, tq=tq, lam_init=lam_init),
        grid=(n // t, DA_HEADS),
        in_specs=[blk, blk, blk, pl.BlockSpec((None, 2, t, LANES), lambda b, h: (h, 0, 0, 0)),
                  pl.BlockSpec((1, 1, LANES), lambda b, h: (h, 0, 0))]
                 + [vec(a) for a in lams] + [vec(subln)],
        out_specs=blk,
        out_shape=jax.ShapeDtypeStruct((n, DA_WIDTH), BF16),
        scratch_shapes=[pltpu.VMEM((tq, tq), F32), pltpu.VMEM((t, LANES), BF16), pltpu.VMEM((t, LANES), BF16),
                        pltpu.VMEM((LANES, t), BF16)],
        compiler_params=_params("parallel", "parallel"),
        name="da_prompt",
    )(q, k, v, _alibi_key_tables(t), slopes, *lams, subln)


def _select_rope(qr, h):
    lane = lax.broadcasted_iota(jnp.int32, qr.shape, 1)
    mine = lax.shift_right_logical(lane, CHUNK_SHIFT) == (h % 2)
    return jnp.where(mine, qr, jnp.zeros_like(qr))


MLA_PAIR = 2


def _mla_prompt_kernel(qn_ref, qr_ref, kv_ref, krd_ref, o_ref, mask_ref, kfull_ref, *, tq):
    t = qn_ref.shape[0]
    kv_cols = QK_NOPE + V_HEAD
    mask_ref[...] = _strip(tq, tq, None)
    for hh in range(MLA_PAIR):
        kfull_ref[hh, :, :LANES] = kv_ref[:, hh * kv_cols:hh * kv_cols + QK_NOPE]
        kfull_ref[hh, :, LANES:] = krd_ref[...]

    def scores(hh, qi):
        lo, ext = qi * tq, (qi + 1) * tq
        rope_block = qr_ref[lo:ext, (hh // 2) * LANES:(hh // 2 + 1) * LANES]
        q = jnp.concatenate([qn_ref[lo:ext, hh * LANES:(hh + 1) * LANES], _select_rope(rope_block, hh)], axis=1)
        sd = _dot_nt(q, kfull_ref[hh, lo:ext, :]) + mask_ref[...]
        sp = _dot_nt(q, kfull_ref[hh, 0:lo, :]) if qi > 0 else None
        return sd, sp

    def probabilities(sd, sp):
        m = jnp.max(sd, axis=-1, keepdims=True)
        if sp is None:
            pd = jnp.exp2(sd - m)
            return pd.astype(BF16), None, jnp.sum(pd, axis=-1, keepdims=True)
        m = jnp.maximum(m, jnp.max(sp, axis=-1, keepdims=True))
        pd = jnp.exp2(sd - m)
        pp = jnp.exp2(sp - m)
        l = jnp.sum(pd, axis=-1, keepdims=True) + jnp.sum(pp, axis=-1, keepdims=True)
        return pd.astype(BF16), pp.astype(BF16), l

    def weighted_values(hh, qi, pd, pp, l):
        lo, ext = qi * tq, (qi + 1) * tq
        v_cols = slice(hh * kv_cols + QK_NOPE, (hh + 1) * kv_cols)
        o = _dot(pd, kv_ref[lo:ext, v_cols])
        if pp is not None:
            o = o + _dot(pp, kv_ref[0:lo, v_cols])
        o_ref[lo:ext, hh * LANES:(hh + 1) * LANES] = (o * (1.0 / l)).astype(BF16)

    tiles = [(hh, qi) for hh in range(MLA_PAIR) for qi in range(t // tq)]
    s_next = scores(*tiles[0])
    pending = None
    for i, tile in enumerate(tiles):
        sd, sp = s_next
        if i + 1 < len(tiles):
            s_next = scores(*tiles[i + 1])
        if pending is not None:
            weighted_values(*tiles[i - 1], *pending)
        pending = probabilities(sd, sp)
    weighted_values(*tiles[-1], *pending)


def _mla_prompt(qn, qr, kv, krd, *, t, tq):
    n = qn.shape[0]
    tq = min(tq, t)
    pair = lambda width: pl.BlockSpec((t, MLA_PAIR * width), lambda b, g: (b, g))
    return pl.pallas_call(
        functools.partial(_mla_prompt_kernel, tq=tq),
        grid=(n // t, MLA_HEADS // MLA_PAIR),
        in_specs=[pair(QK_NOPE), pair(QK_ROPE), pair(QK_NOPE + V_HEAD),
                  pl.BlockSpec((t, LANES), lambda b, g: (b, 0))],
        out_specs=pair(V_HEAD),
        out_shape=jax.ShapeDtypeStruct((n, MLA_WIDTH), BF16),
        scratch_shapes=[pltpu.VMEM((tq, tq), F32), pltpu.VMEM((MLA_PAIR, t, 2 * LANES), BF16)],
        compiler_params=_params("parallel", "parallel"),
        name="mla_prompt",
    )(qn, qr, kv, krd)


def _positions(tq, tk, q0, k0):
    qpos = lax.broadcasted_iota(jnp.int32, (tq, tk), 0) + q0
    kpos = lax.broadcasted_iota(jnp.int32, (tq, tk), 1) + k0
    return qpos, kpos


def _da_sample_kernel(q_ref, ck_ref, cv_ref, nk_ref, nv_ref, slope_ref, lq1_ref, lk1_ref, lq2_ref, lk2_ref,
                      subln_ref, o_ref, acc_ref, m_ref, l_ref, *, lam_init):
    c = pl.program_id(1)
    nc = pl.num_programs(1)
    tq = q_ref.shape[0]
    pc = ck_ref.shape[0] // DA_HEADS
    past = pc * nc

    @pl.when(c == 0)
    def _():
        m_ref[...] = jnp.full_like(m_ref, NEG_INF)
        l_ref[...] = jnp.zeros_like(l_ref)
        acc_ref[...] = jnp.zeros_like(acc_ref)

    def attend(keys, values, unit_bias):
        unit2 = jnp.concatenate([unit_bias, unit_bias], axis=0)
        scores = []
        for h in range(DA_HEADS):
            q2 = jnp.concatenate(_split_halves(q_ref[:, h * LANES:(h + 1) * LANES]), axis=0)
            scores.append(_dot_nt(q2, keys(h)) + unit2 * slope_ref[h][:, :1])
        probs = []
        for h, s in enumerate(scores):
            rows = slice(2 * h * tq, (2 * h + 2) * tq)
            m_old = m_ref[rows, :]
            m_new = jnp.maximum(m_old, jnp.max(s, axis=-1, keepdims=True))
            alpha = jnp.exp2(m_old - m_new)
            p = jnp.exp2(s - m_new)
            m_ref[rows, :] = m_new
            l_ref[rows, :] = alpha * l_ref[rows, :] + jnp.sum(p, axis=-1, keepdims=True)
            probs.append((alpha, p.astype(BF16)))
        for h, (alpha, p) in enumerate(probs):
            rows = slice(2 * h * tq, (2 * h + 2) * tq)
            acc_ref[rows, :] = alpha * acc_ref[rows, :] + _dot(p, values(h))

    def head_rows(h):
        return pl.ds(h, pc, stride=DA_HEADS)

    attend(lambda h: ck_ref[head_rows(h), :].astype(BF16), lambda h: cv_ref[head_rows(h), :].astype(BF16),
           _pos_bias(*_positions(tq, pc, past, c * pc), 1.0))

    @pl.when(c == nc - 1)
    def _():
        lam = _lambda(lq1_ref, lk1_ref, lq2_ref, lk2_ref, lam_init)
        attend(lambda h: nk_ref[:, h * LANES:(h + 1) * LANES], lambda h: nv_ref[:, h * LANES:(h + 1) * LANES],
               _pos_bias(*_positions(tq, tq, past, past), 1.0))
        for h in range(DA_HEADS):
            r1 = slice(2 * h * tq, (2 * h + 1) * tq)
            r2 = slice((2 * h + 1) * tq, (2 * h + 2) * tq)
            o = acc_ref[r1, :] / l_ref[r1, :] - acc_ref[r2, :] * (lam / l_ref[r2, :])
            o_ref[:, h * LANES:(h + 1) * LANES] = _da_finish(o, subln_ref[...], lam_init)


def _da_sample(q, cache_k, cache_v, nk, nv, slopes, lams, subln, *, lam_init, pc):
    bd, rows, _ = cache_k.shape
    past = rows // DA_HEADS
    pc = min(pc, past)
    assert past % pc == 0
    n = q.shape[0]
    tq = n // bd
    new = pl.BlockSpec((tq, DA_WIDTH), lambda b, c: (b, 0))
    cache = pl.BlockSpec((None, pc * DA_HEADS, LANES), lambda b, c: (b, c, 0))
    whole = lambda a: pl.BlockSpec(a.shape, lambda b, c: (0,) * a.ndim)
    return pl.pallas_call(
        functools.partial(_da_sample_kernel, lam_init=lam_init),
        grid=(bd, past // pc),
        in_specs=[new, cache, cache, new, new, whole(slopes)] + [whole(a) for a in lams] + [whole(subln)],
        out_specs=new,
        out_shape=jax.ShapeDtypeStruct((n, DA_WIDTH), BF16),
        scratch_shapes=[pltpu.VMEM((2 * DA_HEADS * tq, LANES), F32),
                        pltpu.VMEM((2 * DA_HEADS * tq, 1), F32),
                        pltpu.VMEM((2 * DA_HEADS * tq, 1), F32)],
        compiler_params=_params("parallel", "arbitrary"),
        name="da_sample",
    )(q, cache_k, cache_v, nk, nv, slopes, *lams, subln)


MLA_GROUP = 4


def _mla_sample_kernel(qn_ref, qr_ref, cc_ref, ckrd_ref, nc_ref, nkrd_ref, wukv_ref, o_ref):
    tq = qn_ref.shape[0]
    past = cc_ref.shape[0]
    cc = cc_ref[...].astype(BF16)
    nc = nc_ref[...].astype(BF16)
    rows = MLA_GROUP * tq
    masked = ((past + tq - 1) >> CHUNK_SHIFT) > (past >> CHUNK_SHIFT)
    if masked:
        qrow = lax.broadcasted_iota(jnp.int32, (rows, 1), 0) & (tq - 1)
        bias_c = _pos_bias(qrow + past, lax.broadcasted_iota(jnp.int32, (rows, past), 1), None)
        bias_n = _pos_bias(qrow + past, lax.broadcasted_iota(jnp.int32, (rows, tq), 1) + past, None)
    kv_cols = QK_NOPE + V_HEAD
    for g in range(MLA_HEADS // MLA_GROUP):
        heads = range(g * MLA_GROUP, (g + 1) * MLA_GROUP)
        qa = jnp.concatenate(
            [_dot_nt(qn_ref[:, h * QK_NOPE:(h + 1) * QK_NOPE],
                     wukv_ref[:, h * kv_cols:h * kv_cols + QK_NOPE]).astype(BF16) for h in heads], axis=0)
        qr = jnp.concatenate(
            [_select_rope(qr_ref[:, (h // 2) * LANES:(h // 2 + 1) * LANES], h) for h in heads], axis=0)
        sc = _dot_nt(qa, cc) + _dot_nt(qr, ckrd_ref[...])
        sn = _dot_nt(qa, nc) + _dot_nt(qr, nkrd_ref[...])
        if masked:
            sc, sn = sc + bias_c, sn + bias_n
        m = jnp.maximum(jnp.max(sc, axis=-1, keepdims=True), jnp.max(sn, axis=-1, keepdims=True))
        pc = jnp.exp2(sc - m)
        pn = jnp.exp2(sn - m)
        w = 1.0 / (jnp.sum(pc, axis=-1, keepdims=True) + jnp.sum(pn, axis=-1, keepdims=True))
        lat = ((_dot(pc.astype(BF16), cc) + _dot(pn.astype(BF16), nc)) * w).astype(BF16)
        for j, h in enumerate(heads):
            w_uv = wukv_ref[:, h * kv_cols + QK_NOPE:(h + 1) * kv_cols]
            o_ref[:, h * V_HEAD:(h + 1) * V_HEAD] = _dot(lat[j * tq:(j + 1) * tq, :], w_uv).astype(BF16)


def _mla_sample(qn, qr, cache_ckv, krd_cache, ckv_new, krd_new, w_ukv):
    bd, past, _ = cache_ckv.shape
    n = qn.shape[0]
    tq = n // bd
    assert tq & (tq - 1) == 0
    row = lambda b: (b, 0)
    return pl.pallas_call(
        _mla_sample_kernel,
        grid=(bd,),
        in_specs=[
            pl.BlockSpec((tq, MLA_HEADS * QK_NOPE), row),
            pl.BlockSpec((tq, MLA_HEADS * QK_ROPE), row),
            pl.BlockSpec((None, past, KV_LORA), lambda b: (b, 0, 0)),
            pl.BlockSpec((None, past, LANES), lambda b: (b, 0, 0)),
            pl.BlockSpec((tq, KV_LORA), row),
            pl.BlockSpec((tq, LANES), row),
            _resident(w_ukv.shape),
        ],
        out_specs=pl.BlockSpec((tq, MLA_WIDTH), row),
        out_shape=jax.ShapeDtypeStruct((n, MLA_WIDTH), BF16),
        compiler_params=_params("parallel"),
        name="mla_sample",
    )(qn, qr, cache_ckv, krd_cache, ckv_new, krd_new, w_ukv)


def _proj_out_kernel(x_ref, a_ref, b_ref, w_ref, o_ref):
    o_ref[...] = (x_ref[...] + _dot(a_ref[...], w_ref[:DA_WIDTH, :]) + _dot(b_ref[...], w_ref[DA_WIDTH:, :]))


def _proj_out(x, a_da, a_mla, w_out, *, tm):
    n, d = x.shape
    tm = min(tm, n)
    assert n % tm == 0
    row = lambda i: (i, 0)
    return pl.pallas_call(
        _proj_out_kernel,
        grid=(n // tm,),
        in_specs=[pl.BlockSpec((tm, d), row), pl.BlockSpec((tm, DA_WIDTH), row),
                  pl.BlockSpec((tm, MLA_WIDTH), row), _resident(w_out.shape)],
        out_specs=pl.BlockSpec((tm, d), row),
        out_shape=jax.ShapeDtypeStruct((n, d), F32),
        compiler_params=_params("parallel"),
        name="proj_out",
    )(x, a_da, a_mla, w_out)


def _prep_weights(w_in, mix_norm, q_norm, w_uq, kv_norm, w_ukv, w_out):
    w_in = w_in.astype(BF16)
    w_kr = w_in[:, 3 * DA_WIDTH + Q_LORA + KV_LORA:]
    w_uq = w_uq.astype(BF16).reshape(Q_LORA, MLA_HEADS, QK_HEAD)
    w_uqr = w_uq[:, :, QK_NOPE:]
    return {
        "mix_norm": mix_norm[None, :],
        "w_in": w_in,
        "w_kr": jnp.concatenate([w_kr, w_kr], axis=1),
        "q_norm": q_norm[None, :], "kv_norm": kv_norm[None, :],
        "w_uqn": w_uq[:, :, :QK_NOPE].reshape(Q_LORA, MLA_HEADS * QK_NOPE),
        "w_uqr": w_uqr.reshape(Q_LORA, MLA_HEADS * QK_ROPE),
        "w_ukv": w_ukv.astype(BF16),
        "w_out": w_out.astype(BF16),
    }


def _rope_tables(pos):
    half = QK_ROPE // 2
    inv = ROPE_THETA ** (-jnp.arange(half, dtype=F32) / half)
    ang = pos.astype(F32)[:, None] * inv[None, :]
    c = jnp.concatenate([jnp.cos(ang)] * 2, axis=1)
    s = jnp.concatenate([-jnp.sin(ang), jnp.sin(ang)], axis=1)
    return {"c2": jnp.tile(c, (1, 2)), "s2": jnp.tile(s, (1, 2)),
            "c8": jnp.tile(c, (1, MLA_HEADS)), "s8": jnp.tile(s, (1, MLA_HEADS))}


def _layer(x, pos, past, ffn1, ffn2, wmix, slopes, lams, subln, final_norm, lam_init, *, batch):
    n, d = x.shape
    t = n // batch

    def swiglu(x, ffn, *, last):
        casting = ffn[1].dtype == F32
        out = _ffn(x, *ffn, final_norm, final_norm=last, tm=FFN_ROWS,
                   tf=FFN_COLS_CASTING if casting else FFN_COLS)
        return (out[0], (ffn[0],) + out[1:]) if casting else (out, ffn)

    x, ffn1 = swiglu(x, ffn1, last=False)
    tab_pos = pos if t % PROJ_IN_ROWS == 0 else jnp.tile(pos, batch)
    (q, k_f, k_b, v_f, v_b, ckv, kr, krd, qn, qr, kv) = _proj_in(x, wmix, _rope_tables(tab_pos), tm=PROJ_IN_ROWS)
    if past is None:
        a_da = _da_prompt(q, k_b, v_b, slopes, lams, subln, t=t, tq=QUERY_TILE, lam_init=lam_init)
        a_mla = _mla_prompt(qn, qr, kv, krd, t=t, tq=QUERY_TILE)
    else:
        cache_k, cache_v, cache_ckv, cache_kr = past
        p = cache_k.size // (batch * DA_WIDTH)
        a_da = _da_sample(q, cache_k.reshape(batch, p * DA_HEADS, DA_HEAD_DIM),
                          cache_v.reshape(batch, p * DA_HEADS, DA_HEAD_DIM),
                          k_b, v_b, slopes, lams, subln, lam_init=lam_init, pc=CACHE_CHUNK)
        cache_kr = cache_kr.reshape(batch, p, QK_ROPE)
        krd_cache = jnp.concatenate([cache_kr, cache_kr], axis=-1).astype(BF16)
        a_mla = _mla_sample(qn, qr, cache_ckv.reshape(batch, p, KV_LORA), krd_cache, ckv, krd, wmix["w_ukv"])
    x = _proj_out(x, a_da, a_mla, wmix["w_out"], tm=PROJ_OUT_ROWS)
    y, ffn2 = swiglu(x, ffn2, last=True)
    return y, (k_f, v_f, ckv, kr), ffn1, ffn2


def kernel(x_prompt, x_sample, cache_da_k, cache_da_v, cache_mla_ckv, cache_mla_krope, ffn1_norm, ffn1_w_gate, ffn1_w_up, ffn1_w_down, mix_norm, w_in, da_lambda_q1, da_lambda_k1, da_lambda_q2, da_lambda_k2, da_subln, mla_q_norm, mla_w_uq, mla_kv_norm, mla_w_ukv, w_out, ffn2_norm, ffn2_w_gate, ffn2_w_up, ffn2_w_down, final_norm):
    depth = w_in.shape[0]
    assert depth == 1, "single-layer trunk"
    b, t, d = x_prompt.shape
    bd, td, _ = x_sample.shape
    past_len = cache_da_k.shape[2]
    l = 0
    lam_init = 0.8 - 0.6 * math.exp(-0.3 * l)

    def one_layer(w):
        return w.reshape(w.shape[1:])

    ffn1 = (ffn1_norm[l][None, :], one_layer(ffn1_w_gate), one_layer(ffn1_w_up), one_layer(ffn1_w_down))
    ffn2 = (ffn2_norm[l][None, :], one_layer(ffn2_w_gate), one_layer(ffn2_w_up), one_layer(ffn2_w_down))
    wmix = _prep_weights(w_in[l], mix_norm[l], mla_q_norm[l], mla_w_uq[l], mla_kv_norm[l], mla_w_ukv[l],
                         w_out[l])
    slopes = jnp.asarray(np.broadcast_to(ALIBI_SLOPES[:, None, None], (DA_HEADS, 1, LANES)))
    lams = (da_lambda_q1[l][None, :], da_lambda_k1[l][None, :], da_lambda_q2[l][None, :],
            da_lambda_k2[l][None, :])
    subln = da_subln[l][None, :]
    fnorm = final_norm[None, :]

    past = (cache_da_k, cache_da_v, cache_mla_ckv, cache_mla_krope)
    assert bd * td <= 1024, "the running streams must fit one SwiGLU token tile"
    y_s, st_s, ffn1, ffn2 = _layer(x_sample.reshape(bd * td, d), past_len + jnp.arange(td, dtype=jnp.int32),
                                   past, ffn1, ffn2, wmix, slopes, lams, subln, fnorm, lam_init, batch=bd)
    y_p, st_p, _, _ = _layer(x_prompt.reshape(b * t, d), jnp.arange(t, dtype=jnp.int32), None,
                             ffn1, ffn2, wmix, slopes, lams, subln, fnorm, lam_init, batch=b)

    def state(st, nb, nt):
        k_f, v_f, ckv, kr = st
        return (k_f.reshape(1, nb, nt, DA_HEADS, DA_HEAD_DIM), v_f.reshape(1, nb, nt, DA_HEADS, DA_HEAD_DIM),
                ckv.reshape(1, nb, nt, KV_LORA), kr.reshape(1, nb, nt, QK_ROPE))

    return (y_p.reshape(b, t, d), y_s.reshape(bd, td, d)) + state(st_p, b, t) + state(st_s, bd, td)
```

```python
import functools
import math

import jax
import jax.numpy as jnp
import numpy as np
from jax import lax
from jax.experimental import pallas as pl
from jax.experimental.pallas import tpu as pltpu

F32 = jnp.float32
BF16 = jnp.bfloat16

CHUNK = 64
CHUNK_SHIFT = 6
assert 1 << CHUNK_SHIFT == CHUNK
EPS = 1e-6
NEG_INF = -1e30
DA_HEADS = 8
DA_HALF = 64
DA_HEAD_DIM = 2 * DA_HALF
DA_WIDTH = DA_HEADS * DA_HEAD_DIM
MLA_HEADS = 8
Q_LORA = 512
KV_LORA = 256
QK_NOPE = 128
QK_ROPE = 64
QK_HEAD = QK_NOPE + QK_ROPE
V_HEAD = 128
MLA_WIDTH = MLA_HEADS * V_HEAD
ROPE_THETA = 10000.0
LOG2E = math.log2(math.e)
ALIBI_SLOPES = np.array([2.0 ** (-8.0 * (i + 1) / DA_HEADS) for i in range(DA_HEADS)], np.float32)
LANES = 128
assert DA_HEAD_DIM == LANES and QK_NOPE == LANES and V_HEAD == LANES and 2 * QK_ROPE == LANES

MIB = 1024 * 1024
VMEM_LIMIT_BYTES = 56 * MIB
FFN_VMEM_LIMIT_BYTES = 62 * MIB
FFN_ROWS = 1024
FFN_COLS = 512
PROJ_IN_ROWS = 512
PROJ_OUT_ROWS = 512
QUERY_TILE = 256
CACHE_CHUNK = 2048


def _params(*semantics, vmem_limit_bytes=VMEM_LIMIT_BYTES):
    return pltpu.CompilerParams(dimension_semantics=semantics, vmem_limit_bytes=vmem_limit_bytes)


def _rms(x, g):
    return x * lax.rsqrt(jnp.mean(x * x, axis=-1, keepdims=True) + EPS) * g


def _dot(a, b):
    return jnp.dot(a, b, preferred_element_type=F32)


def _dot_nt(a, b):
    return lax.dot_general(a, b, (((1,), (1,)), ((), ())), preferred_element_type=F32)


def _resident(shape):
    return pl.BlockSpec(shape, lambda *_: (0,) * len(shape), pipeline_mode=pl.Buffered(1))


def _ffn_kernel(x_ref, g_ref, *refs, final_norm, nj, emit_bf16):
    j = pl.program_id(1)
    xn_ref = refs[-1]
    if emit_bf16:
        wg_ref, wu_ref, wd_ref, fn_ref, o_ref, wgu_out_ref, wd_out_ref = refs[:-1]
    else:
        wgu_ref, wd_ref, fn_ref, o_ref = refs[:-1]

    @pl.when(j == 0)
    def _():
        xn_ref[...] = _rms(x_ref[...], g_ref[...]).astype(BF16)

    def weights():
        if not emit_bf16:
            return wgu_ref[...], wd_ref[...]
        wgu = jnp.concatenate([wg_ref[...].astype(BF16), wu_ref[...].astype(BF16)], axis=1)
        wd = wd_ref[...].astype(BF16)
        wgu_out_ref[...] = wgu
        wd_out_ref[...] = wd
        return wgu, wd

    def half_down():
        wgu, wd = weights()
        tf = wd.shape[0]
        gate_up = _dot(xn_ref[...], wgu)
        gate, up = gate_up[:, :tf], gate_up[:, tf:]
        h = (gate * jax.nn.sigmoid(gate) * (0.5 * up)).astype(BF16)
        return _dot(h, wd)

    def finish(y):
        o_ref[...] = _rms(y, fn_ref[...]) if final_norm else y

    if nj == 1:
        finish(x_ref[...] + half_down())
        return

    @pl.when(j == 0)
    def _():
        o_ref[...] = x_ref[...] + half_down()

    @pl.when(jnp.logical_and(j > 0, j < nj - 1))
    def _():
        o_ref[...] += half_down()

    @pl.when(j == nj - 1)
    def _():
        finish(o_ref[...] + half_down())


def _ffn(x, norm, weights, fnorm, *, final_norm, tm, tf):
    n, d = x.shape
    emit_bf16 = len(weights) == 3
    f = weights[-1].shape[0]
    tm = min(tm, n)
    tf = min(tf, f)
    assert n % tm == 0 and f % tf == 0
    assert not emit_bf16 or n == tm
    col_block = pl.BlockSpec((d, tf), lambda i, j: (0, j))
    pair_block = pl.BlockSpec((d, 2 * tf), lambda i, j: (0, j))
    row_block = pl.BlockSpec((tf, d), lambda i, j: (j, 0))
    y_spec = pl.BlockSpec((tm, d), lambda i, j: (i, 0))
    y_shape = jax.ShapeDtypeStruct((n, d), F32)
    out = pl.pallas_call(
        functools.partial(_ffn_kernel, final_norm=final_norm, nj=f // tf, emit_bf16=emit_bf16),
        grid=(n // tm, f // tf),
        in_specs=[pl.BlockSpec((tm, d), lambda i, j: (i, 0)), pl.BlockSpec((1, d), lambda i, j: (0, 0))]
                 + ([col_block, col_block, row_block] if emit_bf16 else [pair_block, row_block])
                 + [pl.BlockSpec((1, d), lambda i, j: (0, 0))],
        out_specs=[y_spec, pair_block, row_block] if emit_bf16 else y_spec,
        out_shape=([y_shape, jax.ShapeDtypeStruct((d, 2 * f), BF16), jax.ShapeDtypeStruct((f, d), BF16)]
                   if emit_bf16 else y_shape),
        scratch_shapes=[pltpu.VMEM((tm, d), BF16)],
        compiler_params=_params("parallel", "arbitrary", vmem_limit_bytes=FFN_VMEM_LIMIT_BYTES),
        name="ffn",
    )(x, norm, *weights, fnorm)
    return tuple(out) if emit_bf16 else out


def _store_by_head(ref, z):
    tokens = z.shape[0]
    for h in range(DA_HEADS):
        ref[pl.ds(h, tokens, stride=DA_HEADS), :] = z[:, h * DA_HEAD_DIM:(h + 1) * DA_HEAD_DIM]


def _swap_rope_halves(x):
    n = x.shape[1]
    half = QK_ROPE // 2
    lane = lax.broadcasted_iota(jnp.int32, x.shape, 1)
    in_first_half = (lane & (QK_ROPE - 1)) < half
    return jnp.where(in_first_half, pltpu.roll(x, n - half, axis=1), pltpu.roll(x, half, axis=1))


def _proj_in_kernel(x_ref, g_ref, win_ref, wkr_ref,
                    qn_ref, kvn_ref, wuqn_ref, wuqr_ref, wukv_ref,
                    c2_ref, s2_ref, c8_ref, s8_ref,
                    q_ref, kf_ref, kb_ref, vf_ref, vb_ref, ckv_ref, kr_ref, krd_ref,
                    qnope_ref, qrope_ref, kv_ref):
    h = _rms(x_ref[...], g_ref[...]).astype(BF16)
    o1, o2, o3 = DA_WIDTH, 2 * DA_WIDTH, 3 * DA_WIDTH
    o4, o5 = o3 + Q_LORA, o3 + Q_LORA + KV_LORA
    q_ref[...] = (_dot(h, win_ref[:, :o1]) * (DA_HALF ** -0.5 * LOG2E)).astype(BF16)
    k = _dot(h, win_ref[:, o1:o2])
    _store_by_head(kf_ref, k)
    kb_ref[...] = k.astype(BF16)
    v = _dot(h, win_ref[:, o2:o3])
    _store_by_head(vf_ref, v)
    vb_ref[...] = v.astype(BF16)
    cq = _rms(_dot(h, win_ref[:, o3:o4]), qn_ref[...]).astype(BF16)
    ckv = _rms(_dot(h, win_ref[:, o4:o5]), kvn_ref[...])
    ckv_ref[...] = ckv
    kr = _dot(h, wkr_ref[...])
    krd = kr * c2_ref[...] + _swap_rope_halves(kr) * s2_ref[...]
    kr_ref[...] = krd[:, :QK_ROPE]
    krd_ref[...] = krd.astype(BF16)
    mla_scale = QK_HEAD ** -0.5 * LOG2E
    qnope_ref[...] = (_dot(cq, wuqn_ref[...]) * mla_scale).astype(BF16)
    qr = _dot(cq, wuqr_ref[...])
    qr = qr * c8_ref[...] + _swap_rope_halves(qr) * s8_ref[...]
    qrope_ref[...] = (qr * mla_scale).astype(BF16)
    kv_ref[...] = _dot(ckv.astype(BF16), wukv_ref[...]).astype(BF16)


def _proj_in(x, w, tabs, *, tm):
    n, d = x.shape
    tm = min(tm, n)
    assert n % tm == 0
    t_rows = tabs["c2"].shape[0]
    assert t_rows % tm == 0 or tm % t_rows == 0
    if tm > t_rows:
        tm = t_rows
    nt = t_rows // tm

    def row(i):
        return (i, 0)

    def trow(i):
        return (i % nt, 0)

    weights = [w["w_in"], w["w_kr"], w["q_norm"], w["kv_norm"], w["w_uqn"], w["w_uqr"], w["w_ukv"]]
    tables = [tabs["c2"], tabs["s2"], tabs["c8"], tabs["s8"]]
    outs = [
        ((n, DA_WIDTH), BF16),
        ((n * DA_HEADS, DA_HEAD_DIM), F32),
        ((n, DA_WIDTH), BF16),
        ((n * DA_HEADS, DA_HEAD_DIM), F32),
        ((n, DA_WIDTH), BF16),
        ((n, KV_LORA), F32),
        ((n, QK_ROPE), F32),
        ((n, LANES), BF16),
        ((n, MLA_HEADS * QK_NOPE), BF16),
        ((n, MLA_HEADS * QK_ROPE), BF16),
        ((n, MLA_HEADS * (QK_NOPE + V_HEAD)), BF16),
    ]
    return pl.pallas_call(
        _proj_in_kernel,
        grid=(n // tm,),
        in_specs=([pl.BlockSpec((tm, d), row), _resident((1, d))]
                  + [_resident(a.shape) for a in weights]
                  + [pl.BlockSpec((tm, a.shape[1]), trow) for a in tables]),
        out_specs=[pl.BlockSpec((tm * s[0] // n, s[1]), row) for s, _ in outs],
        out_shape=[jax.ShapeDtypeStruct(s, dt) for s, dt in outs],
        compiler_params=_params("parallel", vmem_limit_bytes=FFN_VMEM_LIMIT_BYTES),
        name="proj_in",
    )(x, w["mix_norm"], *weights, *tables)


def _lambda(lq1_ref, lk1_ref, lq2_ref, lk2_ref, lam_init):
    a = jnp.sum(lq1_ref[...] * lk1_ref[...], axis=-1, keepdims=True)
    b = jnp.sum(lq2_ref[...] * lk2_ref[...], axis=-1, keepdims=True)
    return jnp.exp(a) - jnp.exp(b) + lam_init


def _pos_bias(qpos, kpos, slope):
    visible = lax.shift_right_arithmetic(kpos, CHUNK_SHIFT) <= lax.shift_right_arithmetic(qpos, CHUNK_SHIFT)
    if slope is None:
        return jnp.where(visible, 0.0, NEG_INF)
    dist = jnp.abs(qpos - kpos).astype(F32)
    return jnp.where(visible, (-LOG2E * slope) * dist, NEG_INF)


def _strip(tq, t, slope):
    qpos = lax.broadcasted_iota(jnp.int32, (tq, t), 0)
    kpos = lax.broadcasted_iota(jnp.int32, (tq, t), 1) - (t - tq)
    return _pos_bias(qpos, kpos, slope)


def _split_halves(q):
    lane = lax.broadcasted_iota(jnp.int32, q.shape, 1)
    zero = jnp.zeros_like(q)
    return jnp.where(lane < DA_HALF, q, zero), jnp.where(lane >= DA_HALF, q, zero)


def _da_finish(o, subln, lam_init):
    return (_rms(o, subln) * (1.0 - lam_init)).astype(BF16)


BIAS_TERMS = 3


def _da_prompt_kernel(q_ref, k_ref, v_ref, tab_ref, slope_ref, lq1_ref, lk1_ref, lq2_ref, lk2_ref, subln_ref,
                      o_ref, diag_ref, k1_ref, k2_ref, vt_ref, *, tq, lam_init):
    t = q_ref.shape[0]
    lam = _lambda(lq1_ref, lk1_ref, lq2_ref, lk2_ref, lam_init)
    c = LOG2E * slope_ref[0][:, :1]

    lane = lax.broadcasted_iota(jnp.int32, (t, LANES), 1)
    k1_ref[...] = jnp.where(lane < DA_HALF, k_ref[...], tab_ref[0])
    k2_ref[...] = jnp.where(lane >= DA_HALF, k_ref[...], tab_ref[1])
    vt_ref[...] = v_ref[...].T

    kpos = lax.broadcasted_iota(jnp.int32, (tq, tq), 0)
    qpos = lax.broadcasted_iota(jnp.int32, (tq, tq), 1)
    visible = lax.shift_right_arithmetic(kpos, CHUNK_SHIFT) <= lax.shift_right_arithmetic(qpos, CHUNK_SHIFT)
    ahead = jnp.maximum(kpos - qpos, 0).astype(F32)
    diag_ref[...] = jnp.where(visible, (-2.0 * c) * ahead, NEG_INF)

    qlane = lax.broadcasted_iota(jnp.int32, (tq, LANES), 1)
    ones_hi = ((qlane >= DA_HALF) & (qlane < DA_HALF + BIAS_TERMS)).astype(F32).astype(BF16)
    ones_lo = (qlane < BIAS_TERMS).astype(F32).astype(BF16)

    def scores(qi):
        lo, ext = qi * tq, (qi + 1) * tq
        q = q_ref[lo:ext, :]
        q1 = jnp.where(qlane < DA_HALF, q, ones_hi)
        q2 = jnp.where(qlane >= DA_HALF, q, ones_lo)
        diag = (_dot_nt(k1_ref[lo:ext, :], q1) + diag_ref[...], _dot_nt(k2_ref[lo:ext, :], q2) + diag_ref[...])
        past = (_dot_nt(k1_ref[0:lo, :], q1), _dot_nt(k2_ref[0:lo, :], q2)) if qi > 0 else None
        return diag, past

    def softmax(sd, sp):
        m = jnp.max(sd, axis=0, keepdims=True)
        if sp is None:
            pd = jnp.exp2(sd - m)
            return pd, None, jnp.sum(pd, axis=0, keepdims=True)
        m = jnp.maximum(m, jnp.max(sp, axis=0, keepdims=True))
        pd = jnp.exp2(sd - m)
        pp = jnp.exp2(sp - m)
        return pd, pp, jnp.sum(pd, axis=0, keepdims=True) + jnp.sum(pp, axis=0, keepdims=True)

    nq = t // tq
    s_next = scores(0)
    for qi in range(nq):
        lo, ext = qi * tq, (qi + 1) * tq
        diag, past = s_next
        if qi + 1 < nq:
            s_next = scores(qi + 1)
        pd1, pp1, l1 = softmax(diag[0], None if past is None else past[0])
        pd2, pp2, l2 = softmax(diag[1], None if past is None else past[1])
        r = lam * l1 / l2
        ot = _dot(vt_ref[:, lo:ext], (pd1 - pd2 * r).astype(BF16))
        if past is not None:
            ot = ot + _dot(vt_ref[:, 0:lo], (pp1 - pp2 * r).astype(BF16))
        ot = ot * (1.0 / l1)
        ot = ot * lax.rsqrt(jnp.mean(ot * ot, axis=0, keepdims=True) + EPS) * (subln_ref[...] * (1.0 - lam_init))
        o_ref[lo:ext, :] = ot.T.astype(BF16)


def _alibi_key_tables(t):
    x = (np.float32(LOG2E) * ALIBI_SLOPES)[:, None] * np.arange(t, dtype=np.float32)[None, :]
    pieces = []
    for _ in range(BIAS_TERMS):
        piece = x.astype(BF16)
        pieces.append(piece)
        x = x - piece.astype(np.float32)
    terms = jnp.asarray(np.stack(pieces, axis=-1))
    pad = lambda before: jnp.pad(terms, ((0, 0), (0, 0), (before, LANES - BIAS_TERMS - before)))
    return jnp.stack([pad(DA_HALF), pad(0)], axis=1)


def _da_prompt(q, k, v, slopes, lams, subln, *, t, tq, lam_init):
    n = q.shape[0]
    subln = subln.reshape(DA_HEAD_DIM, 1)
    tq = min(tq, t)
    blk = pl.BlockSpec((t, LANES), lambda b, h: (b, h))
    vec = lambda a: pl.BlockSpec(a.shape, lambda b, h: (0, 0))
    return pl.pallas_call(
        functools.partial(_da_prompt_kernel, tq=tq, lam_init=lam_init),
        grid=(n // t, DA_HEADS),
        in_specs=[blk, blk, blk, pl.BlockSpec((None, 2, t, LANES), lambda b, h: (h, 0, 0, 0)),
                  pl.BlockSpec((1, 1, LANES), lambda b, h: (h, 0, 0))]
                 + [vec(a) for a in lams] + [vec(subln)],
        out_specs=blk,
        out_shape=jax.ShapeDtypeStruct((n, DA_WIDTH), BF16),
        scratch_shapes=[pltpu.VMEM((tq, tq), F32), pltpu.VMEM((t, LANES), BF16), pltpu.VMEM((t, LANES), BF16),
                        pltpu.VMEM((LANES, t), BF16)],
        compiler_params=_params("parallel", "parallel"),
        name="da_prompt",
    )(q, k, v, _alibi_key_tables(t), slopes, *lams, subln)


def _select_rope(qr, h):
    lane = lax.broadcasted_iota(jnp.int32, qr.shape, 1)
    mine = lax.shift_right_logical(lane, CHUNK_SHIFT) == (h % 2)
    return jnp.where(mine, qr, jnp.zeros_like(qr))


MLA_PAIR = 2


def _mla_prompt_kernel(qn_ref, qr_ref, kv_ref, krd_ref, o_ref, mask_ref, kfull_ref, *, tq):
    t = qn_ref.shape[0]
    kv_cols = QK_NOPE + V_HEAD
    mask_ref[...] = _strip(tq, tq, None)
    for hh in range(MLA_PAIR):
        kfull_ref[hh, :, :LANES] = kv_ref[:, hh * kv_cols:hh * kv_cols + QK_NOPE]
        kfull_ref[hh, :, LANES:] = krd_ref[...]

    def scores(hh, qi):
        lo, ext = qi * tq, (qi + 1) * tq
        rope_block = qr_ref[lo:ext, (hh // 2) * LANES:(hh // 2 + 1) * LANES]
        q = jnp.concatenate([qn_ref[lo:ext, hh * LANES:(hh + 1) * LANES], _select_rope(rope_block, hh)], axis=1)
        sd = _dot_nt(q, kfull_ref[hh, lo:ext, :]) + mask_ref[...]
        sp = _dot_nt(q, kfull_ref[hh, 0:lo, :]) if qi > 0 else None
        return sd, sp

    def probabilities(sd, sp):
        m = jnp.max(sd, axis=-1, keepdims=True)
        if sp is None:
            pd = jnp.exp2(sd - m)
            return pd.astype(BF16), None, jnp.sum(pd, axis=-1, keepdims=True)
        m = jnp.maximum(m, jnp.max(sp, axis=-1, keepdims=True))
        pd = jnp.exp2(sd - m)
        pp = jnp.exp2(sp - m)
        l = jnp.sum(pd, axis=-1, keepdims=True) + jnp.sum(pp, axis=-1, keepdims=True)
        return pd.astype(BF16), pp.astype(BF16), l

    def weighted_values(hh, qi, pd, pp, l):
        lo, ext = qi * tq, (qi + 1) * tq
        v_cols = slice(hh * kv_cols + QK_NOPE, (hh + 1) * kv_cols)
        o = _dot(pd, kv_ref[lo:ext, v_cols])
        if pp is not None:
            o = o + _dot(pp, kv_ref[0:lo, v_cols])
        o_ref[lo:ext, hh * LANES:(hh + 1) * LANES] = (o * (1.0 / l)).astype(BF16)

    tiles = [(hh, qi) for hh in range(MLA_PAIR) for qi in range(t // tq)]
    s_next = scores(*tiles[0])
    pending = None
    for i, tile in enumerate(tiles):
        sd, sp = s_next
        if i + 1 < len(tiles):
            s_next = scores(*tiles[i + 1])
        if pending is not None:
            weighted_values(*tiles[i - 1], *pending)
        pending = probabilities(sd, sp)
    weighted_values(*tiles[-1], *pending)


def _mla_prompt(qn, qr, kv, krd, *, t, tq):
    n = qn.shape[0]
    tq = min(tq, t)
    pair = lambda width: pl.BlockSpec((t, MLA_PAIR * width), lambda b, g: (b, g))
    return pl.pallas_call(
        functools.partial(_mla_prompt_kernel, tq=tq),
        grid=(n // t, MLA_HEADS // MLA_PAIR),
        in_specs=[pair(QK_NOPE), pair(QK_ROPE), pair(QK_NOPE + V_HEAD),
                  pl.BlockSpec((t, LANES), lambda b, g: (b, 0))],
        out_specs=pair(V_HEAD),
        out_shape=jax.ShapeDtypeStruct((n, MLA_WIDTH), BF16),
        scratch_shapes=[pltpu.VMEM((tq, tq), F32), pltpu.VMEM((MLA_PAIR, t, 2 * LANES), BF16)],
        compiler_params=_params("parallel", "parallel"),
        name="mla_prompt",
    )(qn, qr, kv, krd)


def _positions(tq, tk, q0, k0):
    qpos = lax.broadcasted_iota(jnp.int32, (tq, tk), 0) + q0
    kpos = lax.broadcasted_iota(jnp.int32, (tq, tk), 1) + k0
    return qpos, kpos


def _da_sample_kernel(q_ref, ck_ref, cv_ref, nk_ref, nv_ref, slope_ref, lq1_ref, lk1_ref, lq2_ref, lk2_ref,
                      subln_ref, o_ref, acc_ref, m_ref, l_ref, *, lam_init):
    c = pl.program_id(1)
    nc = pl.num_programs(1)
    tq = q_ref.shape[0]
    pc = ck_ref.shape[0] // DA_HEADS
    past = pc * nc

    @pl.when(c == 0)
    def _():
        m_ref[...] = jnp.full_like(m_ref, NEG_INF)
        l_ref[...] = jnp.zeros_like(l_ref)
        acc_ref[...] = jnp.zeros_like(acc_ref)

    def attend(keys, values, unit_bias):
        unit2 = jnp.concatenate([unit_bias, unit_bias], axis=0)
        scores = []
        for h in range(DA_HEADS):
            q2 = jnp.concatenate(_split_halves(q_ref[:, h * LANES:(h + 1) * LANES]), axis=0)
            scores.append(_dot_nt(q2, keys(h)) + unit2 * slope_ref[h][:, :1])
        probs = []
        for h, s in enumerate(scores):
            rows = slice(2 * h * tq, (2 * h + 2) * tq)
            m_old = m_ref[rows, :]
            m_new = jnp.maximum(m_old, jnp.max(s, axis=-1, keepdims=True))
            alpha = jnp.exp2(m_old - m_new)
            p = jnp.exp2(s - m_new)
            m_ref[rows, :] = m_new
            l_ref[rows, :] = alpha * l_ref[rows, :] + jnp.sum(p, axis=-1, keepdims=True)
            probs.append((alpha, p.astype(BF16)))
        for h, (alpha, p) in enumerate(probs):
            rows = slice(2 * h * tq, (2 * h + 2) * tq)
            acc_ref[rows, :] = alpha * acc_ref[rows, :] + _dot(p, values(h))

    def head_rows(h):
        return pl.ds(h, pc, stride=DA_HEADS)

    attend(lambda h: ck_ref[head_rows(h), :].astype(BF16), lambda h: cv_ref[head_rows(h), :].astype(BF16),
           _pos_bias(*_positions(tq, pc, past, c * pc), 1.0))

    @pl.when(c == nc - 1)
    def _():
        lam = _lambda(lq1_ref, lk1_ref, lq2_ref, lk2_ref, lam_init)
        attend(lambda h: nk_ref[:, h * LANES:(h + 1) * LANES], lambda h: nv_ref[:, h * LANES:(h + 1) * LANES],
               _pos_bias(*_positions(tq, tq, past, past), 1.0))
        for h in range(DA_HEADS):
            r1 = slice(2 * h * tq, (2 * h + 1) * tq)
            r2 = slice((2 * h + 1) * tq, (2 * h + 2) * tq)
            o = acc_ref[r1, :] / l_ref[r1, :] - acc_ref[r2, :] * (lam / l_ref[r2, :])
            o_ref[:, h * LANES:(h + 1) * LANES] = _da_finish(o, subln_ref[...], lam_init)


def _da_sample(q, cache_k, cache_v, nk, nv, slopes, lams, subln, *, lam_init, pc):
    bd, rows, _ = cache_k.shape
    past = rows // DA_HEADS
    pc = min(pc, past)
    assert past % pc == 0
    n = q.shape[0]
    tq = n // bd
    new = pl.BlockSpec((tq, DA_WIDTH), lambda b, c: (b, 0))
    cache = pl.BlockSpec((None, pc * DA_HEADS, LANES), lambda b, c: (b, c, 0))
    whole = lambda a: pl.BlockSpec(a.shape, lambda b, c: (0,) * a.ndim)
    return pl.pallas_call(
        functools.partial(_da_sample_kernel, lam_init=lam_init),
        grid=(bd, past // pc),
        in_specs=[new, cache, cache, new, new, whole(slopes)] + [whole(a) for a in lams] + [whole(subln)],
        out_specs=new,
        out_shape=jax.ShapeDtypeStruct((n, DA_WIDTH), BF16),
        scratch_shapes=[pltpu.VMEM((2 * DA_HEADS * tq, LANES), F32),
                        pltpu.VMEM((2 * DA_HEADS * tq, 1), F32),
                        pltpu.VMEM((2 * DA_HEADS * tq, 1), F32)],
        compiler_params=_params("parallel", "arbitrary"),
        name="da_sample",
    )(q, cache_k, cache_v, nk, nv, slopes, *lams, subln)


MLA_GROUP = 4


def _mla_sample_kernel(qn_ref, qr_ref, cc_ref, ckrd_ref, nc_ref, nkrd_ref, wukv_ref, o_ref):
    tq = qn_ref.shape[0]
    past = cc_ref.shape[0]
    cc = cc_ref[...].astype(BF16)
    nc = nc_ref[...].astype(BF16)
    rows = MLA_GROUP * tq
    masked = ((past + tq - 1) >> CHUNK_SHIFT) > (past >> CHUNK_SHIFT)
    if masked:
        qrow = lax.broadcasted_iota(jnp.int32, (rows, 1), 0) & (tq - 1)
        bias_c = _pos_bias(qrow + past, lax.broadcasted_iota(jnp.int32, (rows, past), 1), None)
        bias_n = _pos_bias(qrow + past, lax.broadcasted_iota(jnp.int32, (rows, tq), 1) + past, None)
    kv_cols = QK_NOPE + V_HEAD
    for g in range(MLA_HEADS // MLA_GROUP):
        heads = range(g * MLA_GROUP, (g + 1) * MLA_GROUP)
        qa = jnp.concatenate(
            [_dot_nt(qn_ref[:, h * QK_NOPE:(h + 1) * QK_NOPE],
                     wukv_ref[:, h * kv_cols:h * kv_cols + QK_NOPE]).astype(BF16) for h in heads], axis=0)
        qr = jnp.concatenate(
            [_select_rope(qr_ref[:, (h // 2) * LANES:(h // 2 + 1) * LANES], h) for h in heads], axis=0)
        sc = _dot_nt(qa, cc) + _dot_nt(qr, ckrd_ref[...])
        sn = _dot_nt(qa, nc) + _dot_nt(qr, nkrd_ref[...])
        if masked:
            sc, sn = sc + bias_c, sn + bias_n
        m = jnp.maximum(jnp.max(sc, axis=-1, keepdims=True), jnp.max(sn, axis=-1, keepdims=True))
        pc = jnp.exp2(sc - m)
        pn = jnp.exp2(sn - m)
        w = 1.0 / (jnp.sum(pc, axis=-1, keepdims=True) + jnp.sum(pn, axis=-1, keepdims=True))
        lat = ((_dot(pc.astype(BF16), cc) + _dot(pn.astype(BF16), nc)) * w).astype(BF16)
        for j, h in enumerate(heads):
            w_uv = wukv_ref[:, h * kv_cols + QK_NOPE:(h + 1) * kv_cols]
            o_ref[:, h * V_HEAD:(h + 1) * V_HEAD] = _dot(lat[j * tq:(j + 1) * tq, :], w_uv).astype(BF16)


def _mla_sample(qn, qr, cache_ckv, krd_cache, ckv_new, krd_new, w_ukv):
    bd, past, _ = cache_ckv.shape
    n = qn.shape[0]
    tq = n // bd
    assert tq & (tq - 1) == 0
    row = lambda b: (b, 0)
    return pl.pallas_call(
        _mla_sample_kernel,
        grid=(bd,),
        in_specs=[
            pl.BlockSpec((tq, MLA_HEADS * QK_NOPE), row),
            pl.BlockSpec((tq, MLA_HEADS * QK_ROPE), row),
            pl.BlockSpec((None, past, KV_LORA), lambda b: (b, 0, 0)),
            pl.BlockSpec((None, past, LANES), lambda b: (b, 0, 0)),
            pl.BlockSpec((tq, KV_LORA), row),
            pl.BlockSpec((tq, LANES), row),
            _resident(w_ukv.shape),
        ],
        out_specs=pl.BlockSpec((tq, MLA_WIDTH), row),
        out_shape=jax.ShapeDtypeStruct((n, MLA_WIDTH), BF16),
        compiler_params=_params("parallel"),
        name="mla_sample",
    )(qn, qr, cache_ckv, krd_cache, ckv_new, krd_new, w_ukv)


def _proj_out_kernel(x_ref, a_ref, b_ref, w_ref, o_ref):
    o_ref[...] = (x_ref[...] + _dot(a_ref[...], w_ref[:DA_WIDTH, :]) + _dot(b_ref[...], w_ref[DA_WIDTH:, :]))


def _proj_out(x, a_da, a_mla, w_out, *, tm):
    n, d = x.shape
    tm = min(tm, n)
    assert n % tm == 0
    row = lambda i: (i, 0)
    return pl.pallas_call(
        _proj_out_kernel,
        grid=(n // tm,),
        in_specs=[pl.BlockSpec((tm, d), row), pl.BlockSpec((tm, DA_WIDTH), row),
                  pl.BlockSpec((tm, MLA_WIDTH), row), _resident(w_out.shape)],
        out_specs=pl.BlockSpec((tm, d), row),
        out_shape=jax.ShapeDtypeStruct((n, d), F32),
        compiler_params=_params("parallel"),
        name="proj_out",
    )(x, a_da, a_mla, w_out)


def _prep_weights(w_in, mix_norm, q_norm, w_uq, kv_norm, w_ukv, w_out):
    w_in = w_in.astype(BF16)
    w_kr = w_in[:, 3 * DA_WIDTH + Q_LORA + KV_LORA:]
    w_uq = w_uq.astype(BF16).reshape(Q_LORA, MLA_HEADS, QK_HEAD)
    w_uqr = w_uq[:, :, QK_NOPE:]
    return {
        "mix_norm": mix_norm[None, :],
        "w_in": w_in,
        "w_kr": jnp.concatenate([w_kr, w_kr], axis=1),
        "q_norm": q_norm[None, :], "kv_norm": kv_norm[None, :],
        "w_uqn": w_uq[:, :, :QK_NOPE].reshape(Q_LORA, MLA_HEADS * QK_NOPE),
        "w_uqr": w_uqr.reshape(Q_LORA, MLA_HEADS * QK_ROPE),
        "w_ukv": w_ukv.astype(BF16),
        "w_out": w_out.astype(BF16),
    }


def _rope_tables(pos):
    half = QK_ROPE // 2
    inv = ROPE_THETA ** (-jnp.arange(half, dtype=F32) / half)
    ang = pos.astype(F32)[:, None] * inv[None, :]
    c = jnp.concatenate([jnp.cos(ang)] * 2, axis=1)
    s = jnp.concatenate([-jnp.sin(ang), jnp.sin(ang)], axis=1)
    return {"c2": jnp.tile(c, (1, 2)), "s2": jnp.tile(s, (1, 2)),
            "c8": jnp.tile(c, (1, MLA_HEADS)), "s8": jnp.tile(s, (1, MLA_HEADS))}


def _layer(x, pos, past, ffn1, ffn2, wmix, slopes, lams, subln, final_norm, lam_init, *, batch):
    n, d = x.shape
    t = n // batch

    def swiglu(x, ffn, *, last):
        casting = len(ffn) == 4
        out = _ffn(x, ffn[0], ffn[1:], final_norm, final_norm=last, tm=FFN_ROWS, tf=FFN_COLS)
        return (out[0], (ffn[0],) + out[1:]) if casting else (out, ffn)

    x, ffn1 = swiglu(x, ffn1, last=False)
    tab_pos = pos if t % PROJ_IN_ROWS == 0 else jnp.tile(pos, batch)
    (q, k_f, k_b, v_f, v_b, ckv, kr, krd, qn, qr, kv) = _proj_in(x, wmix, _rope_tables(tab_pos), tm=PROJ_IN_ROWS)
    if past is None:
        a_da = _da_prompt(q, k_b, v_b, slopes, lams, subln, t=t, tq=QUERY_TILE, lam_init=lam_init)
        a_mla = _mla_prompt(qn, qr, kv, krd, t=t, tq=QUERY_TILE)
    else:
        cache_k, cache_v, cache_ckv, cache_kr = past
        p = cache_k.size // (batch * DA_WIDTH)
        a_da = _da_sample(q, cache_k.reshape(batch, p * DA_HEADS, DA_HEAD_DIM),
                          cache_v.reshape(batch, p * DA_HEADS, DA_HEAD_DIM),
                          k_b, v_b, slopes, lams, subln, lam_init=lam_init, pc=CACHE_CHUNK)
        cache_kr = cache_kr.reshape(batch, p, QK_ROPE)
        krd_cache = jnp.concatenate([cache_kr, cache_kr], axis=-1).astype(BF16)
        a_mla = _mla_sample(qn, qr, cache_ckv.reshape(batch, p, KV_LORA), krd_cache, ckv, krd, wmix["w_ukv"])
    x = _proj_out(x, a_da, a_mla, wmix["w_out"], tm=PROJ_OUT_ROWS)
    y, ffn2 = swiglu(x, ffn2, last=True)
    return y, (k_f, v_f, ckv, kr), ffn1, ffn2


def kernel(x_prompt, x_sample, cache_da_k, cache_da_v, cache_mla_ckv, cache_mla_krope, ffn1_norm, ffn1_w_gate, ffn1_w_up, ffn1_w_down, mix_norm, w_in, da_lambda_q1, da_lambda_k1, da_lambda_q2, da_lambda_k2, da_subln, mla_q_norm, mla_w_uq, mla_kv_norm, mla_w_ukv, w_out, ffn2_norm, ffn2_w_gate, ffn2_w_up, ffn2_w_down, final_norm):
    depth = w_in.shape[0]
    assert depth == 1, "single-layer trunk"
    b, t, d = x_prompt.shape
    bd, td, _ = x_sample.shape
    past_len = cache_da_k.shape[2]
    l = 0
    lam_init = 0.8 - 0.6 * math.exp(-0.3 * l)

    def one_layer(w):
        return w.reshape(w.shape[1:])

    ffn1 = (ffn1_norm[l][None, :], one_layer(ffn1_w_gate), one_layer(ffn1_w_up), one_layer(ffn1_w_down))
    ffn2 = (ffn2_norm[l][None, :], one_layer(ffn2_w_gate), one_layer(ffn2_w_up), one_layer(ffn2_w_down))
    wmix = _prep_weights(w_in[l], mix_norm[l], mla_q_norm[l], mla_w_uq[l], mla_kv_norm[l], mla_w_ukv[l],
                         w_out[l])
    slopes = jnp.asarray(np.broadcast_to(ALIBI_SLOPES[:, None, None], (DA_HEADS, 1, LANES)))
    lams = (da_lambda_q1[l][None, :], da_lambda_k1[l][None, :], da_lambda_q2[l][None, :],
            da_lambda_k2[l][None, :])
    subln = da_subln[l][None, :]
    fnorm = final_norm[None, :]

    past = (cache_da_k, cache_da_v, cache_mla_ckv, cache_mla_krope)
    assert bd * td <= 1024, "the running streams must fit one SwiGLU token tile"
    y_s, st_s, ffn1, ffn2 = _layer(x_sample.reshape(bd * td, d), past_len + jnp.arange(td, dtype=jnp.int32),
                                   past, ffn1, ffn2, wmix, slopes, lams, subln, fnorm, lam_init, batch=bd)
    y_p, st_p, _, _ = _layer(x_prompt.reshape(b * t, d), jnp.arange(t, dtype=jnp.int32), None,
                             ffn1, ffn2, wmix, slopes, lams, subln, fnorm, lam_init, batch=b)

    def state(st, nb, nt):
        k_f, v_f, ckv, kr = st
        return (k_f.reshape(1, nb, nt, DA_HEADS, DA_HEAD_DIM), v_f.reshape(1, nb, nt, DA_HEADS, DA_HEAD_DIM),
                ckv.reshape(1, nb, nt, KV_LORA), kr.reshape(1, nb, nt, QK_ROPE))

    return (y_p.reshape(b, t, d), y_s.reshape(bd, td, d)) + state(st_p, b, t) + state(st_s, bd, td)
```
